```python
import math
import jax, jax.numpy as jnp
from jax import lax
import numpy as np

D_MODEL = 1024
BATCH = 4
SEQ = 4096
DEPTH = 1

NSA_HEADS = 8
NSA_KV_GROUPS = 2
NSA_HPG = NSA_HEADS // NSA_KV_GROUPS
NSA_HEAD_DIM = 64
NSA_WIDTH = NSA_HEADS * NSA_HEAD_DIM
KV_WIDTH = NSA_KV_GROUPS * NSA_HEAD_DIM
CMP_BLOCK = 32
CMP_STRIDE = 16
CMP_HIDDEN = 128
SEL_BLOCK = 64
SEL_TOPK = 16
WINDOW = 512
Q_BLOCK = 128

S5_GROUP = 16
S5_WIDTH = 512
S5_GROUPS = S5_WIDTH // S5_GROUP
S5_STATE = 64
DT_MIN = 1e-3
DT_MAX = 1e-1

D_FF = 2816
CONV_WIDTH = 3

COLS_Q = NSA_WIDTH
COLS_KV = 6 * KV_WIDTH
COLS_NSA_GATE = 3 * NSA_HEADS
COLS_S5 = S5_WIDTH
COLS_MERGE = 2 * D_MODEL
IN_COLS = COLS_Q + COLS_KV + COLS_NSA_GATE + COLS_S5 + COLS_MERGE

LN_EPS = 1e-5
NEG_INF = -1e30
SEL_FORCE = 1e9

kernel_name = 'hybrid_nsa_s5_convffn_deepnorm_adaln'


def deepnorm_alpha():
    return (2.0 * DEPTH) ** 0.25


def deepnorm_beta():
    return (8.0 * DEPTH) ** -0.25


def layer_norm(x):
    xf = x.astype(jnp.float32)
    mu = jnp.mean(xf, axis=-1, keepdims=True)
    var = jnp.mean(jnp.square(xf - mu), axis=-1, keepdims=True)
    return ((xf - mu) * lax.rsqrt(var + LN_EPS)).astype(x.dtype)


def masked_softmax(s, mask):
    s = jnp.where(mask, s.astype(jnp.float32), NEG_INF)
    p = jax.nn.softmax(s, axis=-1)
    return p * jnp.any(mask, axis=-1, keepdims=True)


def alibi_slopes(n):
    return jnp.asarray(2.0 ** (-8.0 * (np.arange(n) + 1) / n), dtype=jnp.float32)


def compress(raw, pe, w1, w2):
    b, s, g, dh = raw.shape
    ch = raw.reshape(b, s // CMP_STRIDE, CMP_STRIDE, g, dh)
    blocks = jnp.concatenate([ch[:, :-1], ch[:, 1:]], axis=2)
    blocks = blocks + pe[None, None, :, None, :]
    h = jax.nn.silu(jnp.einsum('bnlgd,lde->bnge', blocks, w1))
    return jnp.einsum('bnge,ed->bngd', h, w2)


def nsa_attention(q, kc, vc, ks, vs, kw, vw, gate_logits):
    b, s, g, hpg, dh = q.shape
    nqb = s // Q_BLOCK
    nc = kc.shape[1]
    ns = s // SEL_BLOCK
    topk = min(SEL_TOPK, ns)
    scale = dh ** -0.5
    slopes = alibi_slopes(NSA_HEADS).reshape(g, hpg)
    cmp_pos = jnp.arange(nc) * CMP_STRIDE + CMP_BLOCK - 1
    cstart = jnp.arange(nc) * CMP_STRIDE
    sstart = jnp.arange(ns) * SEL_BLOCK
    overlap = ((cstart[:, None] < sstart[None, :] + SEL_BLOCK)
               & (cstart[:, None] + CMP_BLOCK > sstart[None, :])).astype(jnp.float32)
    ks_blk = ks.reshape(b, ns, SEL_BLOCK, g, dh).transpose(0, 3, 1, 2, 4)
    vs_blk = vs.reshape(b, ns, SEL_BLOCK, g, dh).transpose(0, 3, 1, 2, 4)
    n_back = WINDOW // Q_BLOCK
    pad = ((0, 0), (WINDOW, 0), (0, 0), (0, 0))
    kw_pad = jnp.pad(kw, pad).reshape(b, nqb + n_back, Q_BLOCK, g, dh)
    vw_pad = jnp.pad(vw, pad).reshape(b, nqb + n_back, Q_BLOCK, g, dh)
    kw_band = jnp.concatenate([kw_pad[:, i:i + nqb] for i in range(n_back + 1)], axis=2)
    vw_band = jnp.concatenate([vw_pad[:, i:i + nqb] for i in range(n_back + 1)], axis=2)
    b_ix = jnp.arange(b)[:, None, None, None]
    g_ix = jnp.arange(g)[None, :, None, None]
    sel_iota = jnp.arange(ns)

    def block(args):
        qi, qb, gb, kwb, vwb = args
        t = qi * Q_BLOCK + jnp.arange(Q_BLOCK)
        qb = qb * scale
        dist_c = t[:, None] - cmp_pos[None, :]
        s_c = (jnp.einsum('btghd,bngd->bghtn', qb, kc).astype(jnp.float32)
               - slopes[:, :, None, None] * dist_c.astype(jnp.float32))
        p_c = masked_softmax(s_c, dist_c >= 0)
        o_c = jnp.einsum('bghtn,bngd->btghd', p_c.astype(vc.dtype), vc)
        imp = jnp.einsum('bghtn,ns->bgts', p_c, overlap)
        cur = t // SEL_BLOCK
        forced = ((sel_iota[None, :] == 0) | (sel_iota[None, :] == cur[:, None])
                  | (sel_iota[None, :] == cur[:, None] - 1))
        valid = sel_iota[None, :] <= cur[:, None]
        imp = jnp.where(forced, SEL_FORCE, jnp.where(valid, imp, -SEL_FORCE))
        _, idx = lax.top_k(imp, topk)
        sel_ok = idx <= cur[None, None, :, None]
        kg = ks_blk[b_ix, g_ix, idx]
        vg = vs_blk[b_ix, g_ix, idx]
        kpos = idx[..., None] * SEL_BLOCK + jnp.arange(SEL_BLOCK)
        dist_s = t[None, None, :, None, None] - kpos
        s_s = (jnp.einsum('btghd,bgtnkd->bghtnk', qb, kg).astype(jnp.float32)
               - slopes[None, :, :, None, None, None] * dist_s[:, :, None].astype(jnp.float32))
        mask_s = (sel_ok[..., None] & (dist_s >= 0))[:, :, None]
        p_s = masked_softmax(s_s.reshape(b, g, hpg, Q_BLOCK, topk * SEL_BLOCK),
                             mask_s.reshape(b, g, 1, Q_BLOCK, topk * SEL_BLOCK))
        o_s = jnp.einsum('bghtm,bgtmd->btghd', p_s.astype(vg.dtype),
                         vg.reshape(b, g, Q_BLOCK, topk * SEL_BLOCK, dh))
        kpos_w = qi * Q_BLOCK - WINDOW + jnp.arange(WINDOW + Q_BLOCK)
        dist_w = t[:, None] - kpos_w[None, :]
        mask_w = (dist_w >= 0) & (dist_w < WINDOW) & (kpos_w[None, :] >= 0)
        s_w = (jnp.einsum('btghd,bsgd->bghts', qb, kwb).astype(jnp.float32)
               - slopes[:, :, None, None] * dist_w.astype(jnp.float32))
        p_w = masked_softmax(s_w, mask_w)
        o_w = jnp.einsum('bghts,bsgd->btghd', p_w.astype(vwb.dtype), vwb)
        gt = jax.nn.sigmoid(gb)
        return gt[..., 0:1] * o_c + gt[..., 1:2] * o_s + gt[..., 2:3] * o_w

    xs = (jnp.arange(nqb),
          jnp.moveaxis(q.reshape(b, nqb, Q_BLOCK, g, hpg, dh), 1, 0),
          jnp.moveaxis(gate_logits.reshape(b, nqb, Q_BLOCK, g, hpg, 3), 1, 0),
          jnp.moveaxis(kw_band, 1, 0),
          jnp.moveaxis(vw_band, 1, 0))
    out = lax.map(block, xs)
    return jnp.moveaxis(out, 0, 1).reshape(b, s, g * hpg * dh)


def complex_affine_combine(e1, e2):
    a1r, a1i, b1r, b1i = e1
    a2r, a2i, b2r, b2i = e2
    return (a2r * a1r - a2i * a1i,
            a2r * a1i + a2i * a1r,
            a2r * b1r - a2i * b1i + b2r,
            a2r * b1i + a2i * b1r + b2i)


def s5_ssm(u, a_re, a_im, b_re, b_im, c_re, c_im, d_skip, log_dt):
    b, s, _ = u.shape
    u5 = u.reshape(b, s, S5_GROUPS, S5_GROUP)
    dt = jnp.exp(log_dt)[:, None]
    lam_re = jnp.minimum(a_re, -1e-4)
    lam_im = a_im
    mag = jnp.exp(lam_re * dt)
    ang = lam_im * dt
    lb_re = mag * jnp.cos(ang)
    lb_im = mag * jnp.sin(ang)
    den = lam_re * lam_re + lam_im * lam_im
    nr = lb_re - 1.0
    coef_re = (nr * lam_re + lb_im * lam_im) / den
    coef_im = (lb_im * lam_re - nr * lam_im) / den
    bb_re = coef_re[..., None] * b_re - coef_im[..., None] * b_im
    bb_im = coef_re[..., None] * b_im + coef_im[..., None] * b_re
    bu_re = jnp.einsum('bsgh,gph->bsgp', u5, bb_re)
    bu_im = jnp.einsum('bsgh,gph->bsgp', u5, bb_im)
    elems = (jnp.broadcast_to(lb_re, bu_re.shape), jnp.broadcast_to(lb_im, bu_im.shape), bu_re, bu_im)
    _, _, x_re, x_im = lax.associative_scan(complex_affine_combine, elems, axis=1)
    y = (jnp.einsum('bsgp,ghp->bsgh', x_re, c_re) - jnp.einsum('bsgp,ghp->bsgh', x_im, c_im)
         + d_skip * u5)
    return y.reshape(b, s, S5_WIDTH)


def causal_depthwise_conv(h, w, bias):
    ch = h.shape[-1]
    out = lax.conv_general_dilated(h, w[:, None, :], window_strides=(1,),
                                   padding=[(CONV_WIDTH - 1, 0)],
                                   dimension_numbers=('NWC', 'WIO', 'NWC'),
                                   feature_group_count=ch)
    return out + bias


def setup_inputs(seed: int = 0) -> dict:
    key = jax.random.key(seed)
    ks = jax.random.split(key, 32)
    f32 = jnp.float32
    L = DEPTH
    beta = deepnorm_beta()

    def nrm(k, shape, std):
        return jax.random.normal(k, shape, f32) * std

    a_re = -0.5 * (1.0 + 0.01 * jax.random.normal(ks[12], (L, S5_GROUPS, S5_STATE), f32))
    a_im = (jnp.pi * jnp.arange(S5_STATE, dtype=f32))[None, None, :] + 0.01 * jax.random.normal(ks[13], (L, S5_GROUPS, S5_STATE), f32)
    return {
        'x': nrm(ks[0], (BATCH, SEQ, D_MODEL), 1.0),
        'c': nrm(ks[1], (BATCH, D_MODEL), 1.0),
        'w_ada': nrm(ks[2], (L, D_MODEL, 6 * D_MODEL), D_MODEL ** -0.5),
        'b_ada': nrm(ks[3], (L, 6 * D_MODEL), 0.01),
        'w_in': nrm(ks[4], (L, D_MODEL, IN_COLS), D_MODEL ** -0.5),
        'pe_ck': nrm(ks[5], (L, CMP_BLOCK, NSA_HEAD_DIM), 0.1),
        'w_ck1': nrm(ks[6], (L, CMP_BLOCK, NSA_HEAD_DIM, CMP_HIDDEN), (CMP_BLOCK * NSA_HEAD_DIM) ** -0.5),
        'w_ck2': nrm(ks[7], (L, CMP_HIDDEN, NSA_HEAD_DIM), CMP_HIDDEN ** -0.5),
        'pe_cv': nrm(ks[8], (L, CMP_BLOCK, NSA_HEAD_DIM), 0.1),
        'w_cv1': nrm(ks[9], (L, CMP_BLOCK, NSA_HEAD_DIM, CMP_HIDDEN), (CMP_BLOCK * NSA_HEAD_DIM) ** -0.5),
        'w_cv2': nrm(ks[10], (L, CMP_HIDDEN, NSA_HEAD_DIM), CMP_HIDDEN ** -0.5),
        'w_nsa_out': nrm(ks[11], (L, NSA_WIDTH, D_MODEL), NSA_WIDTH ** -0.5),
        's5_a_re': a_re,
        's5_a_im': a_im,
        's5_b_re': nrm(ks[14], (L, S5_GROUPS, S5_STATE, S5_GROUP), (2 * S5_GROUP) ** -0.5),
        's5_b_im': nrm(ks[15], (L, S5_GROUPS, S5_STATE, S5_GROUP), (2 * S5_GROUP) ** -0.5),
        's5_c_re': nrm(ks[16], (L, S5_GROUPS, S5_GROUP, S5_STATE), S5_STATE ** -0.5),
        's5_c_im': nrm(ks[17], (L, S5_GROUPS, S5_GROUP, S5_STATE), S5_STATE ** -0.5),
        's5_d': nrm(ks[18], (L, S5_GROUPS, S5_GROUP), 1.0),
        's5_log_dt': jax.random.uniform(ks[19], (L, S5_GROUPS), f32, math.log(DT_MIN), math.log(DT_MAX)),
        'w_s5_glu': nrm(ks[20], (L, S5_WIDTH, 2 * D_MODEL), S5_WIDTH ** -0.5),
        'w_o': nrm(ks[21], (L, D_MODEL, D_MODEL), beta * D_MODEL ** -0.5),
        'ln1_g': 1.0 + nrm(ks[22], (L, D_MODEL), 0.01),
        'ln1_b': nrm(ks[23], (L, D_MODEL), 0.01),
        'w_up': nrm(ks[24], (L, D_MODEL, 2 * D_FF), D_MODEL ** -0.5),
        'conv_w': nrm(ks[25], (L, CONV_WIDTH, 2 * D_FF), CONV_WIDTH ** -0.5),
        'conv_b': nrm(ks[26], (L, 2 * D_FF), 0.01),
        'w_down': nrm(ks[27], (L, D_FF, D_MODEL), beta * D_FF ** -0.5),
        'ln2_g': 1.0 + nrm(ks[28], (L, D_MODEL), 0.01),
        'ln2_b': nrm(ks[29], (L, D_MODEL), 0.01),
    }


def reference(x, c, w_ada, b_ada, w_in, pe_ck, w_ck1, w_ck2, pe_cv, w_cv1, w_cv2, w_nsa_out,
              s5_a_re, s5_a_im, s5_b_re, s5_b_im, s5_c_re, s5_c_im, s5_d, s5_log_dt, w_s5_glu,
              w_o, ln1_g, ln1_b, w_up, conv_w, conv_b, w_down, ln2_g, ln2_b):
    b, s, d = x.shape
    alpha = deepnorm_alpha()
    splits = np.cumsum([COLS_Q, COLS_KV, COLS_NSA_GATE, COLS_S5]).tolist()
    for l in range(DEPTH):
        mod = jax.nn.silu(c) @ w_ada[l] + b_ada[l]
        shift1, scale1, gate1, shift2, scale2, gate2 = [m[:, None, :] for m in jnp.split(mod, 6, axis=-1)]

        h = layer_norm(x) * (1.0 + scale1) + shift1
        z = h @ w_in[l]
        zq, zkv, zg, zs, zm = jnp.split(z, splits, axis=-1)
        q = zq.reshape(b, s, NSA_KV_GROUPS, NSA_HPG, NSA_HEAD_DIM)
        kv = zkv.reshape(b, s, 6, NSA_KV_GROUPS, NSA_HEAD_DIM)
        kc = compress(kv[:, :, 0], pe_ck[l], w_ck1[l], w_ck2[l])
        vc = compress(kv[:, :, 1], pe_cv[l], w_cv1[l], w_cv2[l])
        o_a = nsa_attention(q, kc, vc, kv[:, :, 2], kv[:, :, 3], kv[:, :, 4], kv[:, :, 5],
                            zg.reshape(b, s, NSA_KV_GROUPS, NSA_HPG, 3))
        y_a = o_a @ w_nsa_out[l]
        y_s = s5_ssm(zs, s5_a_re[l], s5_a_im[l], s5_b_re[l], s5_b_im[l], s5_c_re[l], s5_c_im[l],
                     s5_d[l], s5_log_dt[l])
        zz = jax.nn.gelu(y_s) @ w_s5_glu[l]
        y_b = zz[..., :d] * jax.nn.sigmoid(zz[..., d:])
        g_a, g_b = jnp.split(zm, 2, axis=-1)
        mix = (jax.nn.sigmoid(g_a) * y_a + jax.nn.sigmoid(g_b) * y_b) @ w_o[l]
        x = layer_norm(alpha * x + gate1 * mix) * ln1_g[l] + ln1_b[l]

        h2 = layer_norm(x) * (1.0 + scale2) + shift2
        up = causal_depthwise_conv(h2 @ w_up[l], conv_w[l], conv_b[l])
        val, gte = jnp.split(up, 2, axis=-1)
        ff = (jax.nn.silu(gte) * val) @ w_down[l]
        x = layer_norm(alpha * x + gate2 * ff) * ln2_g[l] + ln2_b[l]
    return x
```

```python
import functools
import math

import jax
import jax.numpy as jnp
import numpy as np
from jax import lax
from jax.experimental import pallas as pl
from jax.experimental.pallas import tpu as pltpu

F32 = jnp.float32
BF16 = jnp.bfloat16

D_MODEL = 1024
NSA_HEADS = 8
KV_GROUPS = 2
HPG = NSA_HEADS // KV_GROUPS
HEAD_DIM = 64
CMP_BLOCK = 32
CMP_STRIDE = 16
CMP_HIDDEN = 128
SEL_BLOCK = 64
SEL_TOPK = 16
WINDOW = 512
Q_BLOCK = 128
S5_GROUP = 16
S5_WIDTH = 512
S5_GROUPS = S5_WIDTH // S5_GROUP
S5_STATE = 64
D_FF = 2816
CONV_WIDTH = 3
LN_EPS = 1e-5
NEG_INF = -1e30
SEL_FORCE = 1e9

LANES = 128
SUBLANES = 8
VMEM_LIMIT = 56 * 1024 * 1024

TM_IN = 512
TM_MERGE = 512
TM_FFN = 512
FF_CHUNK = 256
SEL_TK = 512
WIN_KEYS = WINDOW + Q_BLOCK
S5_CHUNK = 16
S5_PAIR = 2

Q_COLS = NSA_HEADS * LANES
KV_SLAB = KV_GROUPS * HEAD_DIM
GATE_COLS = LANES
MERGE_COLS = 2 * D_MODEL


def _cparams(sem):
    return pltpu.CompilerParams(dimension_semantics=sem, vmem_limit_bytes=VMEM_LIMIT)


def _const_spec(shape):
    n = len(shape)
    return pl.BlockSpec(shape, lambda *_: (0,) * n)


def _layer_norm(x):
    mu = jnp.mean(x, axis=-1, keepdims=True)
    xc = x - mu
    var = jnp.mean(xc * xc, axis=-1, keepdims=True)
    return xc * lax.rsqrt(var + LN_EPS)


def _nt_dot(a, b):
    return lax.dot_general(a, b, (((1,), (1,)), ((), ())), preferred_element_type=F32)


def _ada_kernel(c_ref, w_ref, b_ref, o_ref):
    c = c_ref[...]
    a = c * jax.nn.sigmoid(c)
    o_ref[...] = jnp.dot(a, w_ref[...], preferred_element_type=F32,
                         precision=lax.Precision.HIGHEST) + b_ref[...]


def _ada(c, w_ada, b_ada):
    b, d = c.shape
    n = w_ada.shape[1]
    blk = 1024
    return pl.pallas_call(
        _ada_kernel,
        grid=(n // blk,),
        in_specs=[pl.BlockSpec((b, d), lambda j: (0, 0)),
                  pl.BlockSpec((d, blk), lambda j: (0, j)),
                  pl.BlockSpec((1, blk), lambda j: (0, j))],
        out_specs=pl.BlockSpec((b, blk), lambda j: (0, j)),
        out_shape=jax.ShapeDtypeStruct((b, n), F32),
        compiler_params=_cparams(("arbitrary",)),
        name="ada",
    )(c, w_ada, b_ada.reshape(1, n))


def _inproj_kernel(x_ref, mod_ref, w_ref, q_ref, ck_ref, cv_ref, kv_ref, g_ref, s_ref, m_ref):
    hn = _layer_norm(x_ref[...])
    shift = mod_ref[0, 0:1, :]
    scale = mod_ref[0, 1:2, :]
    h = (hn * (1.0 + scale) + shift).astype(BF16)

    col = 0
    for ref in (q_ref, ck_ref, cv_ref, kv_ref, g_ref, s_ref, m_ref):
        width = ref.shape[-1]
        z = jnp.dot(h, w_ref[:, col:col + width], preferred_element_type=F32)
        ref[...] = z.astype(ref.dtype)
        col += width


def _inproj(x2, mod3, w_all, seq):
    n, d = x2.shape
    tiles_per_batch = seq // TM_IN
    widths = (Q_COLS, KV_SLAB, KV_SLAB, 4 * KV_SLAB, GATE_COLS, S5_WIDTH, MERGE_COLS)
    dtypes = (BF16, F32, F32, BF16, F32, F32, F32)
    assert sum(widths) == w_all.shape[1]
    return pl.pallas_call(
        _inproj_kernel,
        grid=(n // TM_IN,),
        in_specs=[pl.BlockSpec((TM_IN, d), lambda i: (i, 0)),
                  pl.BlockSpec((1, 6, d), lambda i: (i // tiles_per_batch, 0, 0)),
                  _const_spec(w_all.shape)],
        out_specs=[pl.BlockSpec((TM_IN, w), lambda i: (i, 0)) for w in widths],
        out_shape=[jax.ShapeDtypeStruct((n, w), dt) for w, dt in zip(widths, dtypes)],
        compiler_params=_cparams(("arbitrary",)),
        name="inproj",
    )(x2, mod3, w_all)


def _build_w_all(w_in):
    d = w_in.shape[0]
    cq = NSA_HEADS * HEAD_DIM
    ckv = 6 * KV_SLAB
    cg = 3 * NSA_HEADS
    wq = w_in[:, :cq].reshape(d, NSA_HEADS, HEAD_DIM) * (HEAD_DIM ** -0.5)
    zeros = jnp.zeros_like(wq)
    is_g0 = (np.arange(NSA_HEADS) < HPG)[None, :, None]
    wq_pad = jnp.concatenate([jnp.where(is_g0, wq, zeros), jnp.where(is_g0, zeros, wq)], axis=-1)
    wq_pad = wq_pad.reshape(d, Q_COLS)
    wkv = w_in[:, cq:cq + ckv]
    wg = jnp.pad(w_in[:, cq + ckv:cq + ckv + cg], ((0, 0), (0, GATE_COLS - cg)))
    rest = w_in[:, cq + ckv + cg:]
    return jnp.concatenate([wq_pad, wkv, wg, rest], axis=1).astype(BF16)


def _cmp_kernel(ch_ref, w1_ref, pe_ref, w2_ref, o_ref):
    ch = ch_ref[0, 0]
    a = jnp.dot((ch + pe_ref[0, 0]).astype(BF16), w1_ref[0, 0], preferred_element_type=F32)
    b = jnp.dot((ch + pe_ref[0, 1]).astype(BF16), w1_ref[0, 1], preferred_element_type=F32)
    n = a.shape[0]
    hsum = a + pltpu.roll(b, n - 1, axis=0)
    hact = hsum * jax.nn.sigmoid(hsum)
    o_ref[0, 0] = jnp.dot(hact.astype(BF16), w2_ref[0], preferred_element_type=F32).astype(o_ref.dtype)


def _cmp_weights(pe, w1, w2):
    half = CMP_BLOCK // 2
    halves = []
    pes = []
    for lo in (0, half):
        w = w1[lo:lo + half]
        z = jnp.zeros_like(w)
        w_g0 = jnp.concatenate([w, z], axis=-1)
        w_g1 = jnp.concatenate([z, w], axis=-1)
        halves.append(jnp.stack([w_g0, w_g1], axis=1).reshape(half * KV_SLAB, KV_GROUPS * CMP_HIDDEN))
        p = pe[lo:lo + half]
        pes.append(jnp.broadcast_to(p[:, None, :], (half, KV_GROUPS, HEAD_DIM)).reshape(1, half * KV_SLAB))
    z2 = jnp.zeros_like(w2)
    w2_blk = jnp.concatenate([jnp.concatenate([w2, z2], axis=1), jnp.concatenate([z2, w2], axis=1)], axis=0)
    return jnp.stack(halves).astype(BF16), jnp.stack(pes), w2_blk.astype(BF16)


def _compress(ch, w1s, pes, w2s):
    kinds, b, nchunk, width = ch.shape
    hid = w1s.shape[-1]
    return pl.pallas_call(
        _cmp_kernel,
        grid=(kinds, b),
        in_specs=[pl.BlockSpec((1, 1, nchunk, width), lambda k, i: (k, i, 0, 0)),
                  pl.BlockSpec((1, 2, width, hid), lambda k, i: (k, 0, 0, 0)),
                  pl.BlockSpec((1, 2, 1, width), lambda k, i: (k, 0, 0, 0)),
                  pl.BlockSpec((1, hid, KV_SLAB), lambda k, i: (k, 0, 0))],
        out_specs=pl.BlockSpec((1, 1, nchunk, KV_SLAB), lambda k, i: (k, i, 0, 0)),
        out_shape=jax.ShapeDtypeStruct((kinds, b, nchunk, KV_SLAB), BF16),
        compiler_params=_cparams(("arbitrary", "arbitrary")),
        name="compress",
    )(ch, w1s, pes, w2s)


def _softmax_rows(sh):
    m = jnp.max(sh, axis=-1, keepdims=True)
    e = jnp.exp(sh - m)
    return e, jnp.sum(e, axis=-1, keepdims=True)


def _attn_kernel(q_ref, kc_ref, vc_ref, ks_ref, vs_ref, kw_ref, vw_ref, zg_ref, ov_ref, ex_ref,
                 o_ref, mask_scr, v_scr, m_scr, l_scr, acc_scr, *, n_cmp, n_sel):
    qb = pl.program_id(1)
    t0 = qb * Q_BLOCK
    ncp = kc_ref.shape[2]
    gates = jax.nn.sigmoid(zg_ref[...])
    r_col = lax.broadcasted_iota(jnp.int32, (Q_BLOCK, 1), 0)
    lane = lax.broadcasted_iota(jnp.int32, (Q_BLOCK, LANES), 1)

    for g in range(KV_GROUPS):
        slopes = [2.0 ** -(g * HPG + h + 1) for h in range(HPG)]
        q4 = jnp.concatenate(
            [q_ref[:, (g * HPG + h) * LANES:(g * HPG + h + 1) * LANES] for h in range(HPG)], axis=0)

        sc = _nt_dot(q4, kc_ref[0, 0])
        n_i = lax.broadcasted_iota(jnp.int32, (Q_BLOCK, ncp), 1)
        r_i = lax.broadcasted_iota(jnp.int32, (Q_BLOCK, ncp), 0)
        dist_c = (t0 + r_i) - (n_i * CMP_STRIDE + (CMP_BLOCK - 1))
        valid_c = jnp.logical_and(dist_c >= 0, n_i < n_cmp)
        distf_c = dist_c.astype(F32)
        row_any = ((t0 + r_col) >= (CMP_BLOCK - 1)).astype(F32)
        p_heads = []
        for h in range(HPG):
            sh = sc[h * Q_BLOCK:(h + 1) * Q_BLOCK] - slopes[h] * distf_c
            sh = jnp.where(valid_c, sh, NEG_INF)
            e, l = _softmax_rows(sh)
            p_heads.append(e * (row_any / l))
        p_c = jnp.concatenate(p_heads, axis=0)
        o_c = jnp.dot(p_c.astype(BF16), vc_ref[0, 0], preferred_element_type=F32)
        imp = jnp.dot(jnp.concatenate(p_heads, axis=1).astype(BF16), ov_ref[...],
                      preferred_element_type=F32)

        imp_t = imp.T[:n_sel]
        j_i = lax.broadcasted_iota(jnp.int32, (n_sel, Q_BLOCK), 0)
        cur = lax.shift_right_logical(t0 + lax.broadcasted_iota(jnp.int32, (n_sel, Q_BLOCK), 1),
                                      int(math.log2(SEL_BLOCK)))
        forced = jnp.logical_or(j_i == 0, jnp.logical_or(j_i == cur, j_i == cur - 1))
        valid = j_i <= cur
        v = jnp.where(forced, SEL_FORCE, jnp.where(valid, imp_t, -SEL_FORCE))
        v_scr[...] = v
        n_chunk = n_sel // SUBLANES
        chunks = [v[k * SUBLANES:(k + 1) * SUBLANES] for k in range(n_chunk)]
        ranks = [jnp.zeros((SUBLANES, Q_BLOCK), jnp.int32) for _ in range(n_chunk)]
        sub_i = lax.broadcasted_iota(jnp.int32, (SUBLANES, Q_BLOCK), 0)
        for i in range(n_sel):
            vi = v_scr[i:i + 1, :]
            ki = i // SUBLANES
            for k in range(n_chunk):
                if k > ki:
                    beats = jnp.where(vi >= chunks[k], 1, 0)
                elif k < ki:
                    beats = jnp.where(vi > chunks[k], 1, 0)
                else:
                    beats = jnp.where(sub_i > (i - ki * SUBLANES),
                                      jnp.where(vi >= chunks[k], 1, 0),
                                      jnp.where(vi > chunks[k], 1, 0))
                ranks[k] = ranks[k] + beats
        rank = jnp.concatenate(ranks, axis=0)
        sel_t = jnp.where(jnp.logical_and(rank < SEL_TOPK, valid), 1.0, 0.0)
        sel_t = jnp.concatenate([sel_t, jnp.zeros((LANES - n_sel, Q_BLOCK), F32)], axis=0)
        sel = sel_t.T.astype(BF16)
        n_mask_tiles = mask_scr.shape[0]
        for jj in range(n_mask_tiles):
            mask_scr[jj] = jnp.dot(sel, ex_ref[:, jj * SEL_TK:(jj + 1) * SEL_TK],
                                   preferred_element_type=F32)

        m_scr[...] = jnp.full(m_scr.shape, NEG_INF, F32)
        l_scr[...] = jnp.zeros(l_scr.shape, F32)
        acc_scr[...] = jnp.zeros(acc_scr.shape, F32)
        n_tiles = (t0 + Q_BLOCK + SEL_TK - 1) // SEL_TK

        def sel_body(j, carry):
            off = pl.multiple_of(j * SEL_TK, SEL_TK)
            k_t = ks_ref[pl.ds(off, SEL_TK), :]
            v_t = vs_ref[pl.ds(off, SEL_TK), :]
            s = _nt_dot(q4, k_t)
            kpos = off + lax.broadcasted_iota(jnp.int32, (Q_BLOCK, SEL_TK), 1)
            tpos = t0 + lax.broadcasted_iota(jnp.int32, (Q_BLOCK, SEL_TK), 0)
            dist = tpos - kpos
            distf = dist.astype(F32)
            allowed = jnp.logical_and(mask_scr[j] > 0.5, dist >= 0)
            ps = []
            for h in range(HPG):
                rows = slice(h * Q_BLOCK, (h + 1) * Q_BLOCK)
                sh = jnp.where(allowed, s[rows] - slopes[h] * distf, NEG_INF)
                m_old = m_scr[rows]
                m_new = jnp.maximum(m_old, jnp.max(sh, axis=-1, keepdims=True))
                p = jnp.where(allowed, jnp.exp(sh - m_new), 0.0)
                alpha = jnp.exp(m_old - m_new)
                l_scr[rows] = alpha * l_scr[rows] + jnp.sum(p, axis=-1, keepdims=True)
                m_scr[rows] = m_new
                acc_scr[rows] = alpha * acc_scr[rows]
                ps.append(p)
            pv = jnp.dot(jnp.concatenate(ps, axis=0).astype(BF16), v_t, preferred_element_type=F32)
            acc_scr[...] = acc_scr[...] + pv
            return carry

        lax.fori_loop(0, n_tiles, sel_body, 0)
        o_s = acc_scr[...] * (1.0 / l_scr[...])

        w0 = pl.multiple_of(jnp.maximum(qb - WINDOW // Q_BLOCK, 0) * Q_BLOCK, Q_BLOCK)
        k_w = kw_ref[pl.ds(w0, WIN_KEYS), :]
        v_w = vw_ref[pl.ds(w0, WIN_KEYS), :]
        sw = _nt_dot(q4, k_w)
        kpos_w = w0 + lax.broadcasted_iota(jnp.int32, (Q_BLOCK, WIN_KEYS), 1)
        tpos_w = t0 + lax.broadcasted_iota(jnp.int32, (Q_BLOCK, WIN_KEYS), 0)
        dist_w = tpos_w - kpos_w
        allowed_w = jnp.logical_and(dist_w >= 0, dist_w < WINDOW)
        distf_w = dist_w.astype(F32)
        pw = []
        for h in range(HPG):
            sh = jnp.where(allowed_w, sw[h * Q_BLOCK:(h + 1) * Q_BLOCK] - slopes[h] * distf_w, NEG_INF)
            e, l = _softmax_rows(sh)
            pw.append(e * (1.0 / l))
        o_w = jnp.dot(jnp.concatenate(pw, axis=0).astype(BF16), v_w, preferred_element_type=F32)

        keep = (lane < HEAD_DIM) if g == 0 else (lane >= HEAD_DIM)
        for h in range(HPG):
            rows = slice(h * Q_BLOCK, (h + 1) * Q_BLOCK)
            c = (g * HPG + h) * 3
            out = (gates[:, c:c + 1] * o_c[rows] + gates[:, c + 1:c + 2] * o_s[rows]
                   + gates[:, c + 2:c + 3] * o_w[rows])
            out = jnp.where(keep, out, 0.0)
            slot = (g * HPG + h) * LANES
            o_ref[:, slot:slot + LANES] = out.astype(o_ref.dtype)


def _attention(zq, kvc, zkv, zg, batch, seq):
    n = zq.shape[0]
    nqb = seq // Q_BLOCK
    ncp = kvc.shape[2]
    n_cmp = seq // CMP_STRIDE - 1
    n_sel = seq // SEL_BLOCK
    assert n_sel <= LANES and n_sel % SUBLANES == 0 and seq % SEL_TK == 0

    cstart = np.arange(ncp) * CMP_STRIDE
    sstart = np.arange(n_sel) * SEL_BLOCK
    overlap = ((cstart[:, None] < sstart[None, :] + SEL_BLOCK)
               & (cstart[:, None] + CMP_BLOCK > sstart[None, :])
               & (np.arange(ncp)[:, None] < n_cmp)).astype(np.float32)
    overlap = np.pad(overlap, ((0, 0), (0, LANES - n_sel)))
    ov4 = jnp.asarray(np.tile(overlap, (HPG, 1)), BF16)
    expand = (np.arange(seq)[None, :] // SEL_BLOCK == np.arange(LANES)[:, None]).astype(np.float32)
    expand = jnp.asarray(expand, BF16)

    kernel = functools.partial(_attn_kernel, n_cmp=n_cmp, n_sel=n_sel)
    kv_spec = lambda idx: pl.BlockSpec((seq, KV_SLAB), lambda b, i, idx=idx: (b, idx))
    return pl.pallas_call(
        kernel,
        grid=(batch, nqb),
        in_specs=[pl.BlockSpec((Q_BLOCK, Q_COLS), lambda b, i: (b * nqb + i, 0)),
                  pl.BlockSpec((1, 1, ncp, KV_SLAB), lambda b, i: (0, b, 0, 0)),
                  pl.BlockSpec((1, 1, ncp, KV_SLAB), lambda b, i: (1, b, 0, 0)),
                  kv_spec(0), kv_spec(1), kv_spec(2), kv_spec(3),
                  pl.BlockSpec((Q_BLOCK, GATE_COLS), lambda b, i: (b * nqb + i, 0)),
                  _const_spec(ov4.shape),
                  _const_spec(expand.shape)],
        out_specs=pl.BlockSpec((Q_BLOCK, Q_COLS), lambda b, i: (b * nqb + i, 0)),
        out_shape=jax.ShapeDtypeStruct((n, Q_COLS), BF16),
        scratch_shapes=[pltpu.VMEM((seq // SEL_TK, Q_BLOCK, SEL_TK), F32),
                        pltpu.VMEM((n_sel, Q_BLOCK), F32),
                        pltpu.VMEM((HPG * Q_BLOCK, 1), F32),
                        pltpu.VMEM((HPG * Q_BLOCK, 1), F32),
                        pltpu.VMEM((HPG * Q_BLOCK, LANES), F32)],
        compiler_params=_cparams(("arbitrary", "arbitrary")),
        name="nsa_attn",
    )(zq, kvc, kvc, zkv, zkv, zkv, zkv, zg, ov4, expand)


def _s5_kernel(u_ref, tp_ref, mre_ref, mim_ref, cre_ref, cim_ref, lre_ref, lim_ref, d_ref, y_ref,
               ere, eim, xre, xim, *, batch, n_chunks):
    u = u_ref[...]
    ub = u.astype(BF16)
    y = jnp.dot(ub, tp_ref[0], preferred_element_type=F32)
    ere[...] = jnp.dot(ub, mre_ref[0], preferred_element_type=F32)
    eim[...] = jnp.dot(ub, mim_ref[0], preferred_element_type=F32)
    lr = lre_ref[0]
    li = lim_ref[0]

    def body(c, carry):
        xr, xi = carry
        rows = pl.ds(c, batch, stride=n_chunks)
        xre[rows, :] = xr
        xim[rows, :] = xi
        er = ere[rows, :]
        ei = eim[rows, :]
        return lr * xr - li * xi + er, lr * xi + li * xr + ei

    zero = jnp.zeros((batch, ere.shape[1]), F32)
    lax.fori_loop(0, n_chunks, body, (zero, zero))
    y = y + jnp.dot(xre[...].astype(BF16), cre_ref[0], preferred_element_type=F32)
    y = y + jnp.dot(xim[...].astype(BF16), cim_ref[0], preferred_element_type=F32)
    y_ref[...] = y + d_ref[0] * u


def _pair_diag(a):
    g, r, c = a.shape
    a = a.reshape(g // S5_PAIR, S5_PAIR, r, c)
    z = jnp.zeros_like(a[:, 0])
    top = jnp.concatenate([a[:, 0], z], axis=2)
    bot = jnp.concatenate([z, a[:, 1]], axis=2)
    return jnp.concatenate([top, bot], axis=1)


def _s5_matrices(a_re, a_im, b_re, b_im, c_re, c_im, d_skip, log_dt):
    t = S5_CHUNK
    hp = lax.Precision.HIGHEST
    dt = jnp.exp(log_dt)[:, None]
    lam_re = jnp.minimum(a_re, -1e-4)
    lam_im = a_im
    mag = jnp.exp(lam_re * dt)
    ang = lam_im * dt
    lb_re = mag * jnp.cos(ang)
    lb_im = mag * jnp.sin(ang)
    den = lam_re * lam_re + lam_im * lam_im
    nr = lb_re - 1.0
    coef_re = (nr * lam_re + lb_im * lam_im) / den
    coef_im = (lb_im * lam_re - nr * lam_im) / den
    bb_re = coef_re[..., None] * b_re - coef_im[..., None] * b_im
    bb_im = coef_re[..., None] * b_im + coef_im[..., None] * b_re
    j = jnp.arange(t + 1, dtype=F32)[:, None, None]
    pmag = jnp.exp(j * (lam_re * dt)[None])
    pw_re = pmag * jnp.cos(j * ang[None])
    pw_im = pmag * jnp.sin(j * ang[None])
    cl_re = c_re[None] * pw_re[:, :, None, :] - c_im[None] * pw_im[:, :, None, :]
    cl_im = c_re[None] * pw_im[:, :, None, :] + c_im[None] * pw_re[:, :, None, :]
    kern = (jnp.einsum('jgop,gpi->gjoi', cl_re[:t], bb_re, precision=hp)
            - jnp.einsum('jgop,gpi->gjoi', cl_im[:t], bb_im, precision=hp))
    lag = np.arange(t)[None, :] - np.arange(t)[:, None]
    toep = kern[:, np.clip(lag, 0, t - 1)]
    toep = jnp.where((lag >= 0)[None, :, :, None, None], toep, 0.0)
    toep = toep.transpose(0, 1, 4, 2, 3).reshape(S5_GROUPS, t * S5_GROUP, t * S5_GROUP)
    rev = pw_re[t - 1 - np.arange(t)], pw_im[t - 1 - np.arange(t)]
    m_re = rev[0][..., None] * bb_re[None] - rev[1][..., None] * bb_im[None]
    m_im = rev[0][..., None] * bb_im[None] + rev[1][..., None] * bb_re[None]
    m_re = m_re.transpose(1, 0, 3, 2).reshape(S5_GROUPS, t * S5_GROUP, S5_STATE)
    m_im = m_im.transpose(1, 0, 3, 2).reshape(S5_GROUPS, t * S5_GROUP, S5_STATE)
    cp_re = cl_re[1:].transpose(1, 3, 0, 2).reshape(S5_GROUPS, S5_STATE, t * S5_GROUP)
    cp_im = (-cl_im[1:]).transpose(1, 3, 0, 2).reshape(S5_GROUPS, S5_STATE, t * S5_GROUP)
    npair = S5_GROUPS // S5_PAIR
    lam_t_re = pw_re[t].reshape(npair, 1, S5_PAIR * S5_STATE)
    lam_t_im = pw_im[t].reshape(npair, 1, S5_PAIR * S5_STATE)
    d_tile = jnp.broadcast_to(d_skip[:, None, :], (S5_GROUPS, t, S5_GROUP)).reshape(
        npair, 1, S5_PAIR * t * S5_GROUP)
    return (_pair_diag(toep).astype(BF16), _pair_diag(m_re).astype(BF16), _pair_diag(m_im).astype(BF16),
            _pair_diag(cp_re).astype(BF16), _pair_diag(cp_im).astype(BF16), lam_t_re, lam_t_im, d_tile)


def _s5(zs, mats, batch, seq):
    n_chunks = seq // S5_CHUNK
    rows = batch * n_chunks
    u = zs.reshape(batch, n_chunks, S5_CHUNK, S5_GROUPS, S5_GROUP).transpose(0, 1, 3, 2, 4)
    u = u.reshape(rows, S5_GROUPS * S5_CHUNK * S5_GROUP)
    toep, m_re, m_im, cp_re, cp_im, l_re, l_im, d_tile = mats
    npair = toep.shape[0]
    wcol = S5_PAIR * S5_CHUNK * S5_GROUP
    wst = S5_PAIR * S5_STATE
    kernel = functools.partial(_s5_kernel, batch=batch, n_chunks=n_chunks)
    p3 = lambda r, c: pl.BlockSpec((1, r, c), lambda k: (k, 0, 0))
    y = pl.pallas_call(
        kernel,
        grid=(npair,),
        in_specs=[pl.BlockSpec((rows, wcol), lambda k: (0, k)),
                  p3(wcol, wcol), p3(wcol, wst), p3(wcol, wst), p3(wst, wcol), p3(wst, wcol),
                  p3(1, wst), p3(1, wst), p3(1, wcol)],
        out_specs=pl.BlockSpec((rows, wcol), lambda k: (0, k)),
        out_shape=jax.ShapeDtypeStruct(u.shape, F32),
        scratch_shapes=[pltpu.VMEM((rows, wst), F32) for _ in range(4)],
        compiler_params=_cparams(("arbitrary",)),
        name="s5",
    )(u, toep, m_re, m_im, cp_re, cp_im, l_re, l_im, d_tile)
    y = y.reshape(batch, n_chunks, S5_GROUPS, S5_CHUNK, S5_GROUP).transpose(0, 1, 3, 2, 4)
    return y.reshape(batch * seq, S5_WIDTH)


def _merge_kernel(oa_ref, ys_ref, zm_ref, x_ref, mod_ref, wn_ref, wg_ref, wo_ref, g_ref, b_ref, o_ref,
                  *, alpha):
    d = x_ref.shape[-1]
    y_a = jnp.dot(oa_ref[...], wn_ref[...], preferred_element_type=F32)
    gl = jax.nn.gelu(ys_ref[...], approximate=True)
    zz = jnp.dot(gl.astype(BF16), wg_ref[...], preferred_element_type=F32)
    y_b = zz[:, :d] * jax.nn.sigmoid(zz[:, d:])
    zm = zm_ref[...]
    mix_in = jax.nn.sigmoid(zm[:, :d]) * y_a + jax.nn.sigmoid(zm[:, d:]) * y_b
    mix = jnp.dot(mix_in.astype(BF16), wo_ref[...], preferred_element_type=F32)
    gate = mod_ref[0, 2:3, :]
    r = alpha * x_ref[...] + gate * mix
    o_ref[...] = _layer_norm(r) * g_ref[...] + b_ref[...]


def _merge(oa, ys, zm, x2, mod3, wn_pad, wg, wo, ln_g, ln_b, seq, alpha):
    n, d = x2.shape
    tiles_per_batch = seq // TM_MERGE
    row = lambda w: pl.BlockSpec((TM_MERGE, w), lambda i: (i, 0))
    return pl.pallas_call(
        functools.partial(_merge_kernel, alpha=alpha),
        grid=(n // TM_MERGE,),
        in_specs=[row(oa.shape[1]), row(ys.shape[1]), row(zm.shape[1]), row(d),
                  pl.BlockSpec((1, 6, d), lambda i: (i // tiles_per_batch, 0, 0)),
                  _const_spec(wn_pad.shape), _const_spec(wg.shape), _const_spec(wo.shape),
                  _const_spec((1, d)), _const_spec((1, d))],
        out_specs=row(d),
        out_shape=jax.ShapeDtypeStruct((n, d), F32),
        compiler_params=_cparams(("arbitrary",)),
        name="merge",
    )(oa, ys, zm, x2, mod3, wn_pad, wg, wo, ln_g.reshape(1, d), ln_b.reshape(1, d))


def _pad_nsa_out(w):
    d = w.shape[1]
    w = w.reshape(NSA_HEADS, HEAD_DIM, d)
    z = jnp.zeros_like(w)
    is_g0 = (np.arange(NSA_HEADS) < HPG)[:, None, None]
    out = jnp.concatenate([jnp.where(is_g0, w, z), jnp.where(is_g0, z, w)], axis=1)
    return out.reshape(NSA_HEADS * LANES, d).astype(BF16)


def _ffn_kernel(x_ref, mod_ref, wup_ref, cw_ref, cb_ref, wdn_ref, g_ref, b_ref, o_ref, tail_ref,
                *, alpha, tiles_per_batch):
    i = pl.program_id(0)
    tm = x_ref.shape[0]

    @pl.when(i % tiles_per_batch == 0)
    def _():
        tail_ref[...] = jnp.zeros(tail_ref.shape, F32)

    x = x_ref[...]
    shift = mod_ref[0, 3:4, :]
    scale = mod_ref[0, 4:5, :]
    gate = mod_ref[0, 5:6, :]
    h2 = (_layer_norm(x) * (1.0 + scale) + shift).astype(BF16)
    row = lax.broadcasted_iota(jnp.int32, (tm, FF_CHUNK), 0)

    def conv_cols(c0):
        cols = slice(c0, c0 + FF_CHUNK)
        up = jnp.dot(h2, wup_ref[:, cols], preferred_element_type=F32)
        tail = tail_ref[:, cols]
        prev1 = tail[SUBLANES - 1:SUBLANES]
        prev2 = tail[SUBLANES - 2:SUBLANES - 1]
        up1 = jnp.where(row == 0, prev1, pltpu.roll(up, 1, axis=0))
        up2 = jnp.where(row == 0, prev2, jnp.where(row == 1, prev1, pltpu.roll(up, 2, axis=0)))
        tail_ref[:, cols] = up[tm - SUBLANES:tm]
        w = cw_ref[:, cols]
        return w[0:1] * up2 + w[1:2] * up1 + w[2:3] * up + cb_ref[:, cols]

    acc = jnp.zeros((tm, x_ref.shape[1]), F32)
    for k in range(D_FF // FF_CHUNK):
        val = conv_cols(k * FF_CHUNK)
        gte = conv_cols(D_FF + k * FF_CHUNK)
        act = (gte * jax.nn.sigmoid(gte) * val).astype(BF16)
        acc = acc + jnp.dot(act, wdn_ref[k * FF_CHUNK:(k + 1) * FF_CHUNK, :], preferred_element_type=F32)
    r = alpha * x + gate * acc
    o_ref[...] = _layer_norm(r) * g_ref[...] + b_ref[...]


def _ffn(x1, mod3, wup, conv_w, conv_b, wdn, ln_g, ln_b, seq, alpha):
    n, d = x1.shape
    tiles_per_batch = seq // TM_FFN
    ff2 = wup.shape[1]
    return pl.pallas_call(
        functools.partial(_ffn_kernel, alpha=alpha, tiles_per_batch=tiles_per_batch),
        grid=(n // TM_FFN,),
        in_specs=[pl.BlockSpec((TM_FFN, d), lambda i: (i, 0)),
                  pl.BlockSpec((1, 6, d), lambda i: (i // tiles_per_batch, 0, 0)),
                  _const_spec(wup.shape), _const_spec(conv_w.shape), _const_spec((1, ff2)),
                  _const_spec(wdn.shape), _const_spec((1, d)), _const_spec((1, d))],
        out_specs=pl.BlockSpec((TM_FFN, d), lambda i: (i, 0)),
        out_shape=jax.ShapeDtypeStruct((n, d), F32),
        scratch_shapes=[pltpu.VMEM((SUBLANES, ff2), F32)],
        compiler_params=_cparams(("arbitrary",)),
        name="ffn",
    )(x1, mod3, wup, conv_w, conv_b.reshape(1, ff2), wdn, ln_g.reshape(1, d), ln_b.reshape(1, d))


def kernel(x, c, w_ada, b_ada, w_in, pe_ck, w_ck1, w_ck2, pe_cv, w_cv1, w_cv2, w_nsa_out,
           s5_a_re, s5_a_im, s5_b_re, s5_b_im, s5_c_re, s5_c_im, s5_d, s5_log_dt, w_s5_glu,
           w_o, ln1_g, ln1_b, w_up, conv_w, conv_b, w_down, ln2_g, ln2_b):
    batch, seq, d = x.shape
    depth = w_ada.shape[0]
    alpha = (2.0 * depth) ** 0.25
    n = batch * seq
    n_chunk16 = seq // CMP_STRIDE
    xf = x.reshape(n, d)
    for l in range(depth):
        mod3 = _ada(c, w_ada[l], b_ada[l]).reshape(batch, 6, d)

        zq, zck, zcv, zkv, zg, zs, zm = _inproj(xf, mod3, _build_w_all(w_in[l]), seq)

        wk1, pek, wk2 = _cmp_weights(pe_ck[l], w_ck1[l], w_ck2[l])
        wv1, pev, wv2 = _cmp_weights(pe_cv[l], w_cv1[l], w_cv2[l])
        ch = jnp.stack([zck, zcv]).reshape(2, batch, n_chunk16, CMP_STRIDE * KV_SLAB)
        kvc = _compress(ch, jnp.stack([wk1, wv1]), jnp.stack([pek, pev]), jnp.stack([wk2, wv2]))

        oa = _attention(zq, kvc, zkv, zg, batch, seq)

        mats = _s5_matrices(s5_a_re[l], s5_a_im[l], s5_b_re[l], s5_b_im[l], s5_c_re[l], s5_c_im[l],
                            s5_d[l], s5_log_dt[l])
        ys = _s5(zs, mats, batch, seq)

        x1 = _merge(oa, ys, zm, xf, mod3, _pad_nsa_out(w_nsa_out[l]), w_s5_glu[l].astype(BF16),
                    w_o[l].astype(BF16), ln1_g[l], ln1_b[l], seq, alpha)

        xf = _ffn(x1, mod3, w_up[l].astype(BF16), conv_w[l], conv_b[l], w_down[l].astype(BF16),
                  ln2_g[l], ln2_b[l], seq, alpha)
    return xf.reshape(batch, seq, d)
```

```python
import functools
import math

import jax
import jax.numpy as jnp
import numpy as np
from jax import lax
from jax.experimental import pallas as pl
from jax.experimental.pallas import tpu as pltpu

F32 = jnp.float32
BF16 = jnp.bfloat16

D_MODEL = 1024
NSA_HEADS = 8
KV_GROUPS = 2
HPG = NSA_HEADS // KV_GROUPS
HEAD_DIM = 64
CMP_BLOCK = 32
CMP_STRIDE = 16
CMP_HIDDEN = 128
SEL_BLOCK = 64
SEL_TOPK = 16
WINDOW = 512
Q_BLOCK = 128
S5_GROUP = 16
S5_WIDTH = 512
S5_GROUPS = S5_WIDTH // S5_GROUP
S5_STATE = 64
D_FF = 2816
CONV_WIDTH = 3
LN_EPS = 1e-5
NEG_INF = -1e30
SEL_FORCE = 1e9

LANES = 128
SUBLANES = 8
VMEM_LIMIT = 56 * 1024 * 1024

TM_IN = 512
TM_MERGE = 512
TM_FFN = 512
FF_CHUNK = 256
SEL_TK = 512
WIN_KEYS = WINDOW + Q_BLOCK
WIN_BACK = WINDOW // Q_BLOCK
S5_CHUNK = 8
S5_LB = LANES // S5_GROUP
S5_NLB = S5_WIDTH // LANES

Q_COLS = NSA_HEADS * LANES
KV_SLAB = KV_GROUPS * HEAD_DIM
GATE_COLS = LANES
MERGE_COLS = 2 * D_MODEL
KV_OUT_COLS = 8 * KV_SLAB
ONE_LANE = 2


def _cparams(sem):
    return pltpu.CompilerParams(dimension_semantics=sem, vmem_limit_bytes=VMEM_LIMIT)


def _const_spec(shape):
    n = len(shape)
    return pl.BlockSpec(shape, lambda *_: (0,) * n)


def _layer_norm(x):
    mu = jnp.mean(x, axis=-1, keepdims=True)
    xc = x - mu
    var = jnp.mean(xc * xc, axis=-1, keepdims=True)
    return xc * lax.rsqrt(var + LN_EPS)


def _nt_dot(a, b):
    return lax.dot_general(a, b, (((1,), (1,)), ((), ())), preferred_element_type=F32)


def _group_lane0(g):
    return g * HEAD_DIM


def _ada_kernel(c_ref, w_ref, b_ref, o_ref):
    c = c_ref[...]
    a = c * jax.nn.sigmoid(c)
    o_ref[...] = jnp.dot(a, w_ref[...], preferred_element_type=F32,
                         precision=lax.Precision.HIGHEST) + b_ref[...]


def _ada(c, w_ada, b_ada):
    b, d = c.shape
    n = w_ada.shape[1]
    blk = 1024
    return pl.pallas_call(
        _ada_kernel,
        grid=(n // blk,),
        in_specs=[pl.BlockSpec((b, d), lambda j: (0, 0)),
                  pl.BlockSpec((d, blk), lambda j: (0, j)),
                  pl.BlockSpec((1, blk), lambda j: (0, j))],
        out_specs=pl.BlockSpec((b, blk), lambda j: (0, j)),
        out_shape=jax.ShapeDtypeStruct((b, n), F32),
        compiler_params=_cparams(("arbitrary",)),
        name="ada",
    )(c, w_ada, b_ada.reshape(1, n))


def _inproj_kernel(x_ref, mod_ref, w_ref, pf_ref, q_ref, ck_ref, cv_ref, kv_ref, g_ref, s_ref, m_ref):
    hn = _layer_norm(x_ref[...])
    shift = mod_ref[0, 0:1, :]
    scale = mod_ref[0, 1:2, :]
    h = (hn * (1.0 + scale) + shift).astype(BF16)

    def proj(col, width):
        return jnp.dot(h, w_ref[:, col:col + width], preferred_element_type=F32)

    col = 0
    q_ref[...] = proj(col, Q_COLS).astype(q_ref.dtype)
    col += Q_COLS
    ck_ref[...] = proj(col, KV_SLAB)
    col += KV_SLAB
    cv_ref[...] = proj(col, KV_SLAB)
    col += KV_SLAB
    lane = lax.broadcasted_iota(jnp.int32, (x_ref.shape[0], LANES), 1)
    pf = pf_ref[...]
    for part in range(4):
        z = proj(col + part * KV_SLAB, KV_SLAB)
        for g in range(KV_GROUPS):
            own = (lane < HEAD_DIM) if g == 0 else (lane >= HEAD_DIM)
            slot = (part * KV_GROUPS + g) * KV_SLAB
            kv_ref[:, slot:slot + KV_SLAB] = jnp.where(
                own, z, pf[:, g * LANES:(g + 1) * LANES]).astype(kv_ref.dtype)
    col += 4 * KV_SLAB
    g_ref[...] = proj(col, GATE_COLS)
    col += GATE_COLS
    s_ref[...] = proj(col, S5_WIDTH)
    col += S5_WIDTH
    m_ref[...] = proj(col, MERGE_COLS)


def _key_position_features(seq):
    p = np.arange(seq)
    out = np.zeros((seq, KV_GROUPS, LANES), np.float32)
    for g in range(KV_GROUPS):
        l0 = _group_lane0(1 - g)
        out[:, g, l0] = p // SEL_BLOCK
        out[:, g, l0 + 1] = p % SEL_BLOCK
        out[:, g, l0 + 2] = 1.0
    return jnp.asarray(out.reshape(seq, KV_GROUPS * LANES))


def _inproj(x2, mod3, w_all, seq):
    n, d = x2.shape
    tiles_per_batch = seq // TM_IN
    widths = (Q_COLS, KV_SLAB, KV_SLAB, KV_OUT_COLS, GATE_COLS, S5_WIDTH, MERGE_COLS)
    dtypes = (BF16, F32, F32, BF16, F32, F32, F32)
    pf = _key_position_features(seq)
    return pl.pallas_call(
        _inproj_kernel,
        grid=(n // TM_IN,),
        in_specs=[pl.BlockSpec((TM_IN, d), lambda i: (i, 0)),
                  pl.BlockSpec((1, 6, d), lambda i: (i // tiles_per_batch, 0, 0)),
                  _const_spec(w_all.shape),
                  pl.BlockSpec((TM_IN, KV_GROUPS * LANES), lambda i: (i % tiles_per_batch, 0))],
        out_specs=[pl.BlockSpec((TM_IN, w), lambda i: (i, 0)) for w in widths],
        out_shape=[jax.ShapeDtypeStruct((n, w), dt) for w, dt in zip(widths, dtypes)],
        compiler_params=_cparams(("arbitrary",)),
        name="inproj",
    )(x2, mod3, w_all, pf)


def _build_w_all(w_in):
    d = w_in.shape[0]
    cq = NSA_HEADS * HEAD_DIM
    ckv = 6 * KV_SLAB
    cg = 3 * NSA_HEADS
    zeros = jnp.zeros((d, HEAD_DIM), w_in.dtype)
    pieces = []
    for hd in range(NSA_HEADS):
        wq = w_in[:, hd * HEAD_DIM:(hd + 1) * HEAD_DIM] * (HEAD_DIM ** -0.5)
        pieces += [wq, zeros] if hd < HPG else [zeros, wq]
    wq_pad = jnp.concatenate(pieces, axis=1)
    wkv = w_in[:, cq:cq + ckv]
    wg = jnp.pad(w_in[:, cq + ckv:cq + ckv + cg], ((0, 0), (0, GATE_COLS - cg)))
    rest = w_in[:, cq + ckv + cg:]
    return jnp.concatenate([wq_pad, wkv, wg, rest], axis=1).astype(BF16)


def _cmp_kernel(ch_ref, w1_ref, pe_ref, w2_ref, o_ref):
    ch = ch_ref[0, 0]
    a = jnp.dot((ch + pe_ref[0, 0]).astype(BF16), w1_ref[0, 0], preferred_element_type=F32)
    b = jnp.dot((ch + pe_ref[0, 1]).astype(BF16), w1_ref[0, 1], preferred_element_type=F32)
    n = a.shape[0]
    hsum = a + pltpu.roll(b, n - 1, axis=0)
    hact = hsum * jax.nn.sigmoid(hsum)
    o_ref[0, 0] = jnp.dot(hact.astype(BF16), w2_ref[0], preferred_element_type=F32).astype(o_ref.dtype)


def _cmp_weights(pe, w1, w2):
    half = CMP_BLOCK // 2
    halves = []
    pes = []
    for lo in (0, half):
        w = w1[lo:lo + half]
        z = jnp.zeros_like(w)
        w_g0 = jnp.concatenate([w, z], axis=-1)
        w_g1 = jnp.concatenate([z, w], axis=-1)
        halves.append(jnp.stack([w_g0, w_g1], axis=1).reshape(half * KV_SLAB, KV_GROUPS * CMP_HIDDEN))
        p = pe[lo:lo + half]
        pes.append(jnp.broadcast_to(p[:, None, :], (half, KV_GROUPS, HEAD_DIM)).reshape(1, half * KV_SLAB))
    z2 = jnp.zeros_like(w2)
    w2_blk = jnp.concatenate([jnp.concatenate([w2, z2], axis=1), jnp.concatenate([z2, w2], axis=1)], axis=0)
    return jnp.stack(halves).astype(BF16), jnp.stack(pes), w2_blk.astype(BF16)


def _compress(ch, w1s, pes, w2s):
    kinds, b, nchunk, width = ch.shape
    hid = w1s.shape[-1]
    return pl.pallas_call(
        _cmp_kernel,
        grid=(kinds, b),
        in_specs=[pl.BlockSpec((1, 1, nchunk, width), lambda k, i: (k, i, 0, 0)),
                  pl.BlockSpec((1, 2, width, hid), lambda k, i: (k, 0, 0, 0)),
                  pl.BlockSpec((1, 2, 1, width), lambda k, i: (k, 0, 0, 0)),
                  pl.BlockSpec((1, hid, KV_SLAB), lambda k, i: (k, 0, 0))],
        out_specs=pl.BlockSpec((1, 1, nchunk, KV_SLAB), lambda k, i: (k, i, 0, 0)),
        out_shape=jax.ShapeDtypeStruct((kinds, b, nchunk, KV_SLAB), BF16),
        compiler_params=_cparams(("arbitrary", "arbitrary")),
        name="compress",
    )(ch, w1s, pes, w2s)


def _softmax_parts(s):
    m = jnp.max(s, axis=-1, keepdims=True)
    e = jnp.exp(s - m)
    return m, e, jnp.sum(e, axis=-1, keepdims=True)


def _attn_kernel(q_ref, kc_ref, vc_ref, ks0_ref, ks1_ref, vs0_ref, vs1_ref, kw0_ref, kw1_ref,
                 vw0_ref, vw1_ref, zg_ref,
                 ov_ref, cf_ref, hot_ref, tri_ref, wm_ref, o_ref, v_scr, lhs_scr, m_scr, acc_scr, part_scr,
                 *, n_cmp, n_sel):
    qb = pl.program_id(1)
    t0 = qb * Q_BLOCK
    t0f = t0.astype(F32)
    ncp = kc_ref.shape[2]
    gates = jax.nn.sigmoid(zg_ref[...])
    r_col = lax.broadcasted_iota(jnp.int32, (Q_BLOCK, 1), 0)
    lane = lax.broadcasted_iota(jnp.int32, (Q_BLOCK, LANES), 1)
    tri = tri_ref[...]
    wmask = wm_ref[jnp.minimum(qb, WIN_BACK)]
    hrows = [slice(h * Q_BLOCK, (h + 1) * Q_BLOCK) for h in range(HPG)]

    for g in range(KV_GROUPS):
        slopes = [2.0 ** -(g * HPG + h + 1) for h in range(HPG)]
        ks_ref = (ks0_ref, ks1_ref)[g]
        vs_ref = (vs0_ref, vs1_ref)[g]
        kw_ref = (kw0_ref, kw1_ref)[g]
        vw_ref = (vw0_ref, vw1_ref)[g]
        f0 = _group_lane0(1 - g)
        own_c = lax.broadcasted_iota(jnp.int32, (ncp, LANES), 1)
        own_c = (own_c < HEAD_DIM) if g == 0 else (own_c >= HEAD_DIM)

        def feature_rows(values_of_slope):
            blocks = []
            for h in range(HPG):
                f = jnp.zeros((Q_BLOCK, LANES), F32)
                for k, v in enumerate(values_of_slope(slopes[h])):
                    f = jnp.where(lane == f0 + k, v, f)
                blocks.append(f)
            return jnp.concatenate(blocks, axis=0)

        q4 = jnp.concatenate(
            [q_ref[:, (g * HPG + h) * LANES:(g * HPG + h + 1) * LANES] for h in range(HPG)],
            axis=0).astype(F32)
        lhs_pos = (q4 + feature_rows(lambda s: (64.0 * s, s, -s * t0f))).astype(BF16)
        lhs_cmp = (q4 + feature_rows(
            lambda s: (256.0 * s, 16.0 * s, (CMP_BLOCK - 1.0) * s, -s * t0f))).astype(BF16)

        kc_aug = jnp.where(own_c, kc_ref[0, 0], cf_ref[g])
        sc = _nt_dot(lhs_cmp, kc_aug)
        n_i = lax.broadcasted_iota(jnp.int32, (Q_BLOCK, ncp), 1)
        r_i = lax.broadcasted_iota(jnp.int32, (Q_BLOCK, ncp), 0)
        valid_c = jnp.logical_and((n_i * CMP_STRIDE + (CMP_BLOCK - 1)) <= (t0 + r_i), n_i < n_cmp)
        row_any = ((t0 + r_col) >= (CMP_BLOCK - 1)).astype(F32)
        sc = jnp.where(valid_c[None], sc.reshape(HPG, Q_BLOCK, ncp), NEG_INF)
        _, e, l = _softmax_parts(sc)
        p_c = (e * (row_any[None] / l)).astype(BF16)
        o_c = jnp.dot(p_c.reshape(HPG * Q_BLOCK, ncp), vc_ref[0, 0], preferred_element_type=F32)
        imp = jnp.dot(jnp.concatenate([p_c[h] for h in range(HPG)], axis=1), ov_ref[...],
                      preferred_element_type=F32)

        imp_t = imp.T[:n_sel]
        j_i = lax.broadcasted_iota(jnp.int32, (n_sel, Q_BLOCK), 0)
        cur = lax.shift_right_logical(t0 + lax.broadcasted_iota(jnp.int32, (n_sel, Q_BLOCK), 1),
                                      int(math.log2(SEL_BLOCK)))
        forced = jnp.logical_or(j_i == 0, jnp.logical_or(j_i == cur, j_i == cur - 1))
        valid = j_i <= cur
        v = jnp.where(forced, SEL_FORCE, jnp.where(valid, imp_t, -SEL_FORCE))
        v_scr[...] = v
        n_chunk = n_sel // SUBLANES
        chunks = [v[k * SUBLANES:(k + 1) * SUBLANES] for k in range(n_chunk)]
        ranks = [jnp.zeros((SUBLANES, Q_BLOCK), jnp.int32) for _ in range(n_chunk)]
        sub_i = lax.broadcasted_iota(jnp.int32, (SUBLANES, Q_BLOCK), 0)
        for i in range(n_sel):
            vi = v_scr[i:i + 1, :]
            ki = i // SUBLANES
            for k in range(n_chunk):
                if k > ki:
                    beats = jnp.where(vi >= chunks[k], 1, 0)
                elif k < ki:
                    beats = jnp.where(vi > chunks[k], 1, 0)
                else:
                    beats = jnp.where(sub_i > (i - ki * SUBLANES),
                                      jnp.where(vi >= chunks[k], 1, 0),
                                      jnp.where(vi > chunks[k], 1, 0))
                ranks[k] = ranks[k] + beats
        rank = jnp.concatenate(ranks, axis=0)
        chosen = jnp.logical_and(jnp.logical_and(rank < SEL_TOPK, valid), j_i < 2 * qb)
        bias_t = jnp.where(chosen, 0.0, NEG_INF)
        bias_t = jnp.concatenate([bias_t, jnp.zeros((LANES - n_sel, Q_BLOCK), F32)], axis=0)
        bias = bias_t.T.astype(BF16)
        lhs_scr[g] = jnp.concatenate([lhs_pos, jnp.concatenate([bias] * HPG, axis=0)], axis=1)

        kd = ks_ref[pl.ds(pl.multiple_of(t0, Q_BLOCK), Q_BLOCK), :]
        vd = vs_ref[pl.ds(pl.multiple_of(t0, Q_BLOCK), Q_BLOCK), :]
        sd = _nt_dot(lhs_pos, kd)
        sd = (sd.reshape(HPG, Q_BLOCK, Q_BLOCK) + tri[None]).reshape(HPG * Q_BLOCK, Q_BLOCK)
        m = jnp.max(sd, axis=-1, keepdims=True)
        m_scr[g] = jnp.broadcast_to(m, m_scr.shape[1:])
        acc_scr[g] = jnp.dot(jnp.exp(sd - m).astype(BF16), vd, preferred_element_type=F32)

        w0 = pl.multiple_of(jnp.maximum(qb - WIN_BACK, 0) * Q_BLOCK, Q_BLOCK)
        k_w = kw_ref[pl.ds(w0, WIN_KEYS), :]
        v_w = vw_ref[pl.ds(w0, WIN_KEYS), :]
        sw = _nt_dot(lhs_pos, k_w)
        sw = (sw.reshape(HPG, Q_BLOCK, WIN_KEYS) + wmask[None]).reshape(HPG * Q_BLOCK, WIN_KEYS)
        e = jnp.exp(sw - jnp.max(sw, axis=-1, keepdims=True))
        o_w = jnp.dot(e.astype(BF16), v_w, preferred_element_type=F32)
        o_w = o_w * (1.0 / o_w[:, f0 + ONE_LANE:f0 + ONE_LANE + 1])

        for h in range(HPG):
            c = (g * HPG + h) * 3
            part_scr[g, hrows[h]] = gates[:, c:c + 1] * o_c[hrows[h]] + gates[:, c + 2:c + 3] * o_w[hrows[h]]

    n_tiles = (t0 + SEL_TK - 1) // SEL_TK

    def sel_body(j, carry):
        off = pl.multiple_of(j * SEL_TK, SEL_TK)
        hot = hot_ref[pl.ds(off, SEL_TK), :]
        for g in range(KV_GROUPS):
            ks_ref = (ks0_ref, ks1_ref)[g]
            vs_ref = (vs0_ref, vs1_ref)[g]
            rhs = jnp.concatenate([ks_ref[pl.ds(off, SEL_TK), :], hot], axis=1)
            s = _nt_dot(lhs_scr[g], rhs)
            m_old = m_scr[g]
            m_new = jnp.maximum(m_old, jnp.max(s, axis=-1, keepdims=True))
            p = jnp.exp(s - jnp.tile(m_new, (1, SEL_TK // LANES)))
            alpha = jnp.exp(m_old - m_new)
            m_scr[g] = m_new
            pv = jnp.dot(p.astype(BF16), vs_ref[pl.ds(off, SEL_TK), :], preferred_element_type=F32)
            acc_scr[g] = alpha * acc_scr[g] + pv
        return carry

    lax.fori_loop(0, n_tiles, sel_body, 0)

    for g in range(KV_GROUPS):
        one = _group_lane0(1 - g) + ONE_LANE
        acc = acc_scr[g]
        o_s = acc * (1.0 / acc[:, one:one + 1])
        keep = (lane < HEAD_DIM) if g == 0 else (lane >= HEAD_DIM)
        for h in range(HPG):
            c = (g * HPG + h) * 3
            out = part_scr[g, hrows[h]] + gates[:, c + 1:c + 2] * o_s[hrows[h]]
            out = jnp.where(keep, out, 0.0)
            slot = (g * HPG + h) * LANES
            o_ref[:, slot:slot + LANES] = out.astype(o_ref.dtype)


def _attention_tables(seq, ncp, n_cmp, n_sel):
    cstart = np.arange(ncp) * CMP_STRIDE
    sstart = np.arange(n_sel) * SEL_BLOCK
    overlap = ((cstart[:, None] < sstart[None, :] + SEL_BLOCK)
               & (cstart[:, None] + CMP_BLOCK > sstart[None, :])
               & (np.arange(ncp)[:, None] < n_cmp)).astype(np.float32)
    overlap = np.pad(overlap, ((0, 0), (0, LANES - n_sel)))
    ov4 = np.tile(overlap, (HPG, 1))
    n = np.arange(ncp)
    cfeat = np.zeros((KV_GROUPS, ncp, LANES), np.float32)
    for g in range(KV_GROUPS):
        l0 = _group_lane0(1 - g)
        cfeat[g, :, l0] = n // 16
        cfeat[g, :, l0 + 1] = n % 16
        cfeat[g, :, l0 + 2] = 1.0
        cfeat[g, :, l0 + 3] = 1.0
    hot = (np.arange(seq)[:, None] // SEL_BLOCK == np.arange(LANES)[None, :]).astype(np.float32)
    r = np.arange(Q_BLOCK)[:, None]
    tri = np.where(np.arange(Q_BLOCK)[None, :] <= r, 0.0, NEG_INF).astype(np.float32)
    c = np.arange(WIN_KEYS)[None, :]
    wm = []
    for qb in range(WIN_BACK + 1):
        dist = (qb * Q_BLOCK + r) - c if qb < WIN_BACK else (WINDOW + r) - c
        wm.append(np.where((dist >= 0) & (dist < WINDOW), 0.0, NEG_INF))
    wm = np.stack(wm).astype(np.float32)
    return (jnp.asarray(ov4, BF16), jnp.asarray(cfeat, BF16), jnp.asarray(hot, BF16),
            jnp.asarray(tri), jnp.asarray(wm))


def _attention(zq, kvc, zkv, zg, batch, seq):
    n = zq.shape[0]
    nqb = seq // Q_BLOCK
    ncp = kvc.shape[2]
    n_cmp = seq // CMP_STRIDE - 1
    n_sel = seq // SEL_BLOCK
    assert n_sel <= HEAD_DIM and n_sel % SUBLANES == 0 and seq % SEL_TK == 0 and nqb > WIN_BACK
    assert ncp // 16 <= 256 and seq // SEL_BLOCK <= 256
    ov4, cfeat, hot, tri, wm = _attention_tables(seq, ncp, n_cmp, n_sel)

    kernel = functools.partial(_attn_kernel, n_cmp=n_cmp, n_sel=n_sel)
    kv_spec = lambda idx: pl.BlockSpec((seq, KV_SLAB), lambda b, i, idx=idx: (b, idx))
    return pl.pallas_call(
        kernel,
        grid=(batch, nqb),
        in_specs=[pl.BlockSpec((Q_BLOCK, Q_COLS), lambda b, i: (b * nqb + i, 0)),
                  pl.BlockSpec((1, 1, ncp, KV_SLAB), lambda b, i: (0, b, 0, 0)),
                  pl.BlockSpec((1, 1, ncp, KV_SLAB), lambda b, i: (1, b, 0, 0)),
                  *[kv_spec(k) for k in range(8)],
                  pl.BlockSpec((Q_BLOCK, GATE_COLS), lambda b, i: (b * nqb + i, 0)),
                  _const_spec(ov4.shape), _const_spec(cfeat.shape), _const_spec(hot.shape),
                  _const_spec(tri.shape), _const_spec(wm.shape)],
        out_specs=pl.BlockSpec((Q_BLOCK, Q_COLS), lambda b, i: (b * nqb + i, 0)),
        out_shape=jax.ShapeDtypeStruct((n, Q_COLS), BF16),
        scratch_shapes=[pltpu.VMEM((n_sel, Q_BLOCK), F32),
                        pltpu.VMEM((KV_GROUPS, HPG * Q_BLOCK, 2 * LANES), BF16),
                        pltpu.VMEM((KV_GROUPS, HPG * Q_BLOCK, LANES), F32),
                        pltpu.VMEM((KV_GROUPS, HPG * Q_BLOCK, LANES), F32),
                        pltpu.VMEM((KV_GROUPS, HPG * Q_BLOCK, LANES), F32)],
        compiler_params=_cparams(("arbitrary", "arbitrary")),
        name="nsa_attn",
    )(zq, kvc, kvc, *([zkv] * 8), zg, ov4, cfeat, hot, tri, wm)


def _s5_kernel(u_ref, tp_ref, mre_ref, mim_ref, cre_ref, cim_ref, lre_ref, lim_ref, d_ref, y_ref,
               ere, eim, xre, xim, *, n_chunks):
    def tok(t):
        return pl.ds(t, n_chunks, stride=S5_CHUNK)

    u = jnp.concatenate([u_ref[tok(t), :] for t in range(S5_CHUNK)], axis=1).astype(BF16)
    y = jnp.dot(u, tp_ref[0], preferred_element_type=F32)
    ere[...] = jnp.dot(u, mre_ref[0], preferred_element_type=F32)
    eim[...] = jnp.dot(u, mim_ref[0], preferred_element_type=F32)
    lr = lre_ref[0]
    li = lim_ref[0]

    def body(c, carry):
        xr, xi = carry
        row = pl.ds(c, 1)
        xre[row, :] = xr
        xim[row, :] = xi
        er = ere[row, :]
        ei = eim[row, :]
        return lr * xr - li * xi + er, lr * xi + li * xr + ei

    zero = jnp.zeros((1, ere.shape[1]), F32)
    lax.fori_loop(0, n_chunks, body, (zero, zero), unroll=8)
    y = y + jnp.dot(xre[...].astype(BF16), cre_ref[0], preferred_element_type=F32)
    y = y + jnp.dot(xim[...].astype(BF16), cim_ref[0], preferred_element_type=F32)
    d = d_ref[0]
    for t in range(S5_CHUNK):
        y_ref[tok(t), :] = y[:, t * LANES:(t + 1) * LANES] + d * u_ref[tok(t), :]


def _s5_matrices(a_re, a_im, b_re, b_im, c_re, c_im, d_skip, log_dt):
    t = S5_CHUNK
    hp = lax.Precision.HIGHEST
    dt = jnp.exp(log_dt)[:, None]
    lam_re = jnp.minimum(a_re, -1e-4)
    lam_im = a_im
    mag = jnp.exp(lam_re * dt)
    ang = lam_im * dt
    lb_re = mag * jnp.cos(ang)
    lb_im = mag * jnp.sin(ang)
    den = lam_re * lam_re + lam_im * lam_im
    nr = lb_re - 1.0
    coef_re = (nr * lam_re + lb_im * lam_im) / den
    coef_im = (lb_im * lam_re - nr * lam_im) / den
    bb_re = coef_re[..., None] * b_re - coef_im[..., None] * b_im
    bb_im = coef_re[..., None] * b_im + coef_im[..., None] * b_re
    j = jnp.arange(t + 1, dtype=F32)[:, None, None]
    pmag = jnp.exp(j * (lam_re * dt)[None])
    pw_re = pmag * jnp.cos(j * ang[None])
    pw_im = pmag * jnp.sin(j * ang[None])
    cl_re = c_re[None] * pw_re[:, :, None, :] - c_im[None] * pw_im[:, :, None, :]
    cl_im = c_re[None] * pw_im[:, :, None, :] + c_im[None] * pw_re[:, :, None, :]
    kern = (jnp.einsum('jgop,gpi->gjoi', cl_re[:t], bb_re, precision=hp)
            - jnp.einsum('jgop,gpi->gjoi', cl_im[:t], bb_im, precision=hp))
    lag = np.arange(t)[None, :] - np.arange(t)[:, None]
    lag_hot = jnp.asarray((lag[None] == np.arange(t)[:, None, None]).astype(np.float32))
    eye = jnp.eye(S5_LB, dtype=F32)
    nlb = S5_GROUPS // S5_LB
    toep = jnp.einsum('jst,gjoi->gstoi', lag_hot, kern, precision=hp)
    toep = toep.reshape(nlb, S5_LB, t, t, S5_GROUP, S5_GROUP)
    toep = jnp.einsum('qgstoi,gh->qsgitho', toep, eye, precision=hp).reshape(nlb, t * LANES, t * LANES)
    rv_re = pw_re[t - 1 - np.arange(t)]
    rv_im = pw_im[t - 1 - np.arange(t)]
    m_re = rv_re[..., None] * bb_re[None] - rv_im[..., None] * bb_im[None]
    m_im = rv_re[..., None] * bb_im[None] + rv_im[..., None] * bb_re[None]

    def lay_m(m):
        m = m.reshape(t, nlb, S5_LB, S5_STATE, S5_GROUP)
        return jnp.einsum('sqgpi,gh->qsgihp', m, eye, precision=hp).reshape(
            nlb, t * LANES, S5_LB * S5_STATE)

    def lay_c(c):
        c = c.reshape(t, nlb, S5_LB, S5_GROUP, S5_STATE)
        return jnp.einsum('tqgop,gh->qgptho', c, eye, precision=hp).reshape(
            nlb, S5_LB * S5_STATE, t * LANES)

    lam_t_re = pw_re[t].reshape(nlb, 1, S5_LB * S5_STATE)
    lam_t_im = pw_im[t].reshape(nlb, 1, S5_LB * S5_STATE)
    d_row = d_skip.reshape(nlb, 1, LANES)
    return (toep.astype(BF16), lay_m(m_re).astype(BF16), lay_m(m_im).astype(BF16),
            lay_c(cl_re[1:]).astype(BF16), lay_c(-cl_im[1:]).astype(BF16), lam_t_re, lam_t_im, d_row)


def _s5(zs, mats, batch, seq):
    n_chunks = seq // S5_CHUNK
    toep, m_re, m_im, cp_re, cp_im, l_re, l_im, d_row = mats
    nlb = toep.shape[0]
    wcol = S5_CHUNK * LANES
    wst = S5_LB * S5_STATE
    kernel = functools.partial(_s5_kernel, n_chunks=n_chunks)
    p3 = lambda r, c: pl.BlockSpec((1, r, c), lambda q, b: (q, 0, 0))
    return pl.pallas_call(
        kernel,
        grid=(nlb, batch),
        in_specs=[pl.BlockSpec((seq, LANES), lambda q, b: (b, q)),
                  p3(wcol, wcol), p3(wcol, wst), p3(wcol, wst), p3(wst, wcol), p3(wst, wcol),
                  p3(1, wst), p3(1, wst), p3(1, LANES)],
        out_specs=pl.BlockSpec((seq, LANES), lambda q, b: (b, q)),
        out_shape=jax.ShapeDtypeStruct(zs.shape, F32),
        scratch_shapes=[pltpu.VMEM((n_chunks, wst), F32) for _ in range(4)],
        compiler_params=_cparams(("arbitrary", "arbitrary")),
        name="s5",
    )(zs, toep, m_re, m_im, cp_re, cp_im, l_re, l_im, d_row)


def _merge_kernel(oa_ref, ys_ref, zm_ref, x_ref, mod_ref, wn_ref, wg_ref, wo_ref, g_ref, b_ref, o_ref,
                  *, alpha):
    d = x_ref.shape[-1]
    y_a = jnp.dot(oa_ref[...], wn_ref[...], preferred_element_type=F32)
    gl = jax.nn.gelu(ys_ref[...], approximate=True)
    zz = jnp.dot(gl.astype(BF16), wg_ref[...], preferred_element_type=F32)
    y_b = zz[:, :d] * jax.nn.sigmoid(zz[:, d:])
    zm = zm_ref[...]
    mix_in = jax.nn.sigmoid(zm[:, :d]) * y_a + jax.nn.sigmoid(zm[:, d:]) * y_b
    mix = jnp.dot(mix_in.astype(BF16), wo_ref[...], preferred_element_type=F32)
    gate = mod_ref[0, 2:3, :]
    r = alpha * x_ref[...] + gate * mix
    o_ref[...] = _layer_norm(r) * g_ref[...] + b_ref[...]


def _merge(oa, ys, zm, x2, mod3, wn_pad, wg, wo, ln_g, ln_b, seq, alpha):
    n, d = x2.shape
    tiles_per_batch = seq // TM_MERGE
    row = lambda w: pl.BlockSpec((TM_MERGE, w), lambda i: (i, 0))
    return pl.pallas_call(
        functools.partial(_merge_kernel, alpha=alpha),
        grid=(n // TM_MERGE,),
        in_specs=[row(oa.shape[1]), row(ys.shape[1]), row(zm.shape[1]), row(d),
                  pl.BlockSpec((1, 6, d), lambda i: (i // tiles_per_batch, 0, 0)),
                  _const_spec(wn_pad.shape), _const_spec(wg.shape), _const_spec(wo.shape),
                  _const_spec((1, d)), _const_spec((1, d))],
        out_specs=row(d),
        out_shape=jax.ShapeDtypeStruct((n, d), F32),
        compiler_params=_cparams(("arbitrary",)),
        name="merge",
    )(oa, ys, zm, x2, mod3, wn_pad, wg, wo, ln_g.reshape(1, d), ln_b.reshape(1, d))


def _pad_nsa_out(w):
    d = w.shape[1]
    w = w.reshape(NSA_HEADS, HEAD_DIM, d)
    z = jnp.zeros_like(w)
    is_g0 = (np.arange(NSA_HEADS) < HPG)[:, None, None]
    out = jnp.concatenate([jnp.where(is_g0, w, z), jnp.where(is_g0, z, w)], axis=1)
    return out.reshape(NSA_HEADS * LANES, d).astype(BF16)


def _ffn_kernel(x_ref, mod_ref, wup_ref, cw_ref, cb_ref, wdn_ref, g_ref, b_ref, o_ref, tail_ref,
                *, alpha, tiles_per_batch):
    i = pl.program_id(0)
    tm = x_ref.shape[0]

    @pl.when(i % tiles_per_batch == 0)
    def _():
        tail_ref[...] = jnp.zeros(tail_ref.shape, F32)

    x = x_ref[...]
    shift = mod_ref[0, 3:4, :]
    scale = mod_ref[0, 4:5, :]
    gate = mod_ref[0, 5:6, :]
    h2 = (_layer_norm(x) * (1.0 + scale) + shift).astype(BF16)
    row = lax.broadcasted_iota(jnp.int32, (tm, FF_CHUNK), 0)

    def conv_cols(c0):
        cols = slice(c0, c0 + FF_CHUNK)
        up = jnp.dot(h2, wup_ref[:, cols], preferred_element_type=F32)
        tail = tail_ref[:, cols]
        prev1 = tail[SUBLANES - 1:SUBLANES]
        prev2 = tail[SUBLANES - 2:SUBLANES - 1]
        up1 = jnp.where(row == 0, prev1, pltpu.roll(up, 1, axis=0))
        up2 = jnp.where(row == 0, prev2, jnp.where(row == 1, prev1, pltpu.roll(up, 2, axis=0)))
        tail_ref[:, cols] = up[tm - SUBLANES:tm]
        w = cw_ref[:, cols]
        return w[0:1] * up2 + w[1:2] * up1 + w[2:3] * up + cb_ref[:, cols]

    acc = jnp.zeros((tm, x_ref.shape[1]), F32)
    for k in range(D_FF // FF_CHUNK):
        val = conv_cols(k * FF_CHUNK)
        gte = conv_cols(D_FF + k * FF_CHUNK)
        act = (gte * jax.nn.sigmoid(gte) * val).astype(BF16)
        acc = acc + jnp.dot(act, wdn_ref[k * FF_CHUNK:(k + 1) * FF_CHUNK, :], preferred_element_type=F32)
    r = alpha * x + gate * acc
    o_ref[...] = _layer_norm(r) * g_ref[...] + b_ref[...]


def _ffn(x1, mod3, wup, conv_w, conv_b, wdn, ln_g, ln_b, seq, alpha):
    n, d = x1.shape
    tiles_per_batch = seq // TM_FFN
    ff2 = wup.shape[1]
    return pl.pallas_call(
        functools.partial(_ffn_kernel, alpha=alpha, tiles_per_batch=tiles_per_batch),
        grid=(n // TM_FFN,),
        in_specs=[pl.BlockSpec((TM_FFN, d), lambda i: (i, 0)),
                  pl.BlockSpec((1, 6, d), lambda i: (i // tiles_per_batch, 0, 0)),
                  _const_spec(wup.shape), _const_spec(conv_w.shape), _const_spec((1, ff2)),
                  _const_spec(wdn.shape), _const_spec((1, d)), _const_spec((1, d))],
        out_specs=pl.BlockSpec((TM_FFN, d), lambda i: (i, 0)),
        out_shape=jax.ShapeDtypeStruct((n, d), F32),
        scratch_shapes=[pltpu.VMEM((SUBLANES, ff2), F32)],
        compiler_params=_cparams(("arbitrary",)),
        name="ffn",
    )(x1, mod3, wup, conv_w, conv_b.reshape(1, ff2), wdn, ln_g.reshape(1, d), ln_b.reshape(1, d))


def kernel(x, c, w_ada, b_ada, w_in, pe_ck, w_ck1, w_ck2, pe_cv, w_cv1, w_cv2, w_nsa_out,
           s5_a_re, s5_a_im, s5_b_re, s5_b_im, s5_c_re, s5_c_im, s5_d, s5_log_dt, w_s5_glu,
           w_o, ln1_g, ln1_b, w_up, conv_w, conv_b, w_down, ln2_g, ln2_b):
    batch, seq, d = x.shape
    depth = w_ada.shape[0]
    alpha = (2.0 * depth) ** 0.25
    n = batch * seq
    n_chunk16 = seq // CMP_STRIDE
    xf = x.reshape(n, d)
    for l in range(depth):
        mod3 = _ada(c, w_ada[l], b_ada[l]).reshape(batch, 6, d)

        zq, zck, zcv, zkv, zg, zs, zm = _inproj(xf, mod3, _build_w_all(w_in[l]), seq)

        wk1, pek, wk2 = _cmp_weights(pe_ck[l], w_ck1[l], w_ck2[l])
        wv1, pev, wv2 = _cmp_weights(pe_cv[l], w_cv1[l], w_cv2[l])
        ch = jnp.stack([zck, zcv]).reshape(2, batch, n_chunk16, CMP_STRIDE * KV_SLAB)
        kvc = _compress(ch, jnp.stack([wk1, wv1]), jnp.stack([pek, pev]), jnp.stack([wk2, wv2]))

        oa = _attention(zq, kvc, zkv, zg, batch, seq)

        mats = _s5_matrices(s5_a_re[l], s5_a_im[l], s5_b_re[l], s5_b_im[l], s5_c_re[l], s5_c_im[l],
                            s5_d[l], s5_log_dt[l])
        ys = _s5(zs, mats, batch, seq)

        x1 = _merge(oa, ys, zm, xf, mod3, _pad_nsa_out(w_nsa_out[l]), w_s5_glu[l].astype(BF16),
                    w_o[l].astype(BF16), ln1_g[l], ln1_b[l], seq, alpha)

        xf = _ffn(x1, mod3, w_up[l].astype(BF16), conv_w[l], conv_b[l], w_down[l].astype(BF16),
                  ln2_g[l], ln2_b[l], seq, alpha)
    return xf.reshape(batch, seq, d)
```

```python
import functools
import math

import jax
import jax.numpy as jnp
import numpy as np
from jax import lax
from jax.experimental import pallas as pl
from jax.experimental.pallas import tpu as pltpu

F32 = jnp.float32
BF16 = jnp.bfloat16

D_MODEL = 1024
NSA_HEADS = 8
KV_GROUPS = 2
HPG = NSA_HEADS // KV_GROUPS
HEAD_DIM = 64
CMP_BLOCK = 32
CMP_STRIDE = 16
CMP_HIDDEN = 128
SEL_BLOCK = 64
SEL_TOPK = 16
WINDOW = 512
Q_BLOCK = 128
S5_GROUP = 16
S5_WIDTH = 512
S5_GROUPS = S5_WIDTH // S5_GROUP
S5_STATE = 64
D_FF = 2816
CONV_WIDTH = 3
LN_EPS = 1e-5
NEG_INF = -1e30
SEL_FORCE = 1e9

LANES = 128
SUBLANES = 8
VMEM_LIMIT = 56 * 1024 * 1024

TM_IN = 512
TM_MERGE = 512
TM_FFN = 512
FF_CHUNK = 256
SEL_TK = 512
WIN_KEYS = WINDOW + Q_BLOCK
WIN_BACK = WINDOW // Q_BLOCK
S5_CHUNK = 8
S5_LB = LANES // S5_GROUP
S5_NLB = S5_WIDTH // LANES

Q_COLS = NSA_HEADS * LANES
KV_SLAB = KV_GROUPS * HEAD_DIM
GATE_COLS = LANES
MERGE_COLS = 2 * D_MODEL
KV_OUT_COLS = 8 * KV_SLAB
ONE_LANE = 2


def _cparams(sem):
    return pltpu.CompilerParams(dimension_semantics=sem, vmem_limit_bytes=VMEM_LIMIT)


def _const_spec(shape):
    n = len(shape)
    return pl.BlockSpec(shape, lambda *_: (0,) * n)


def _layer_norm(x):
    mu = jnp.mean(x, axis=-1, keepdims=True)
    xc = x - mu
    var = jnp.mean(xc * xc, axis=-1, keepdims=True)
    return xc * lax.rsqrt(var + LN_EPS)


def _nt_dot(a, b):
    return lax.dot_general(a, b, (((1,), (1,)), ((), ())), preferred_element_type=F32)


def _group_lane0(g):
    return g * HEAD_DIM


def _ada_kernel(c_ref, w_ref, b_ref, o_ref):
    c = c_ref[...]
    a = c * jax.nn.sigmoid(c)
    o_ref[...] = jnp.dot(a, w_ref[...], preferred_element_type=F32,
                         precision=lax.Precision.HIGHEST) + b_ref[...]


def _ada(c, w_ada, b_ada):
    b, d = c.shape
    n = w_ada.shape[1]
    blk = 1024
    return pl.pallas_call(
        _ada_kernel,
        grid=(n // blk,),
        in_specs=[pl.BlockSpec((b, d), lambda j: (0, 0)),
                  pl.BlockSpec((d, blk), lambda j: (0, j)),
                  pl.BlockSpec((1, blk), lambda j: (0, j))],
        out_specs=pl.BlockSpec((b, blk), lambda j: (0, j)),
        out_shape=jax.ShapeDtypeStruct((b, n), F32),
        compiler_params=_cparams(("arbitrary",)),
        name="ada",
    )(c, w_ada, b_ada.reshape(1, n))


def _inproj_kernel(x_ref, mod_ref, w_ref, pf_ref, q_ref, ck_ref, cv_ref, kv_ref, g_ref, s_ref, m_ref):
    hn = _layer_norm(x_ref[...])
    shift = mod_ref[0, 0:1, :]
    scale = mod_ref[0, 1:2, :]
    h = (hn * (1.0 + scale) + shift).astype(BF16)

    def proj(col, width):
        return jnp.dot(h, w_ref[:, col:col + width], preferred_element_type=F32)

    col = 0
    q_ref[...] = proj(col, Q_COLS).astype(q_ref.dtype)
    col += Q_COLS
    ck_ref[...] = proj(col, KV_SLAB)
    col += KV_SLAB
    cv_ref[...] = proj(col, KV_SLAB)
    col += KV_SLAB
    lane = lax.broadcasted_iota(jnp.int32, (x_ref.shape[0], LANES), 1)
    pf = pf_ref[...]
    for part in range(4):
        z = proj(col + part * KV_SLAB, KV_SLAB)
        for g in range(KV_GROUPS):
            own = (lane < HEAD_DIM) if g == 0 else (lane >= HEAD_DIM)
            slot = (part * KV_GROUPS + g) * KV_SLAB
            kv_ref[:, slot:slot + KV_SLAB] = jnp.where(
                own, z, pf[:, g * LANES:(g + 1) * LANES]).astype(kv_ref.dtype)
    col += 4 * KV_SLAB
    g_ref[...] = proj(col, GATE_COLS)
    col += GATE_COLS
    s_ref[...] = proj(col, S5_WIDTH)
    col += S5_WIDTH
    m_ref[...] = proj(col, MERGE_COLS)


def _key_position_features(seq):
    p = np.arange(seq)
    out = np.zeros((seq, KV_GROUPS, LANES), np.float32)
    for g in range(KV_GROUPS):
        l0 = _group_lane0(1 - g)
        out[:, g, l0] = p // SEL_BLOCK
        out[:, g, l0 + 1] = p % SEL_BLOCK
        out[:, g, l0 + 2] = 1.0
    return jnp.asarray(out.reshape(seq, KV_GROUPS * LANES))


def _inproj(x2, mod3, w_all, seq):
    n, d = x2.shape
    tiles_per_batch = seq // TM_IN
    widths = (Q_COLS, KV_SLAB, KV_SLAB, KV_OUT_COLS, GATE_COLS, S5_WIDTH, MERGE_COLS)
    dtypes = (BF16, F32, F32, BF16, F32, F32, F32)
    pf = _key_position_features(seq)
    return pl.pallas_call(
        _inproj_kernel,
        grid=(n // TM_IN,),
        in_specs=[pl.BlockSpec((TM_IN, d), lambda i: (i, 0)),
                  pl.BlockSpec((1, 6, d), lambda i: (i // tiles_per_batch, 0, 0)),
                  _const_spec(w_all.shape),
                  pl.BlockSpec((TM_IN, KV_GROUPS * LANES), lambda i: (i % tiles_per_batch, 0))],
        out_specs=[pl.BlockSpec((TM_IN, w), lambda i: (i, 0)) for w in widths],
        out_shape=[jax.ShapeDtypeStruct((n, w), dt) for w, dt in zip(widths, dtypes)],
        compiler_params=_cparams(("arbitrary",)),
        name="inproj",
    )(x2, mod3, w_all, pf)


def _build_w_all(w_in):
    d = w_in.shape[0]
    cq = NSA_HEADS * HEAD_DIM
    ckv = 6 * KV_SLAB
    cg = 3 * NSA_HEADS
    zeros = jnp.zeros((d, HEAD_DIM), w_in.dtype)
    pieces = []
    for hd in range(NSA_HEADS):
        wq = w_in[:, hd * HEAD_DIM:(hd + 1) * HEAD_DIM] * (HEAD_DIM ** -0.5)
        pieces += [wq, zeros] if hd < HPG else [zeros, wq]
    wq_pad = jnp.concatenate(pieces, axis=1)
    wkv = w_in[:, cq:cq + ckv]
    wg = jnp.pad(w_in[:, cq + ckv:cq + ckv + cg], ((0, 0), (0, GATE_COLS - cg)))
    rest = w_in[:, cq + ckv + cg:]
    return jnp.concatenate([wq_pad, wkv, wg, rest], axis=1).astype(BF16)


def _cmp_kernel(ch_ref, w1_ref, pe_ref, w2_ref, o_ref):
    ch = ch_ref[0, 0]
    a = jnp.dot((ch + pe_ref[0, 0]).astype(BF16), w1_ref[0, 0], preferred_element_type=F32)
    b = jnp.dot((ch + pe_ref[0, 1]).astype(BF16), w1_ref[0, 1], preferred_element_type=F32)
    n = a.shape[0]
    hsum = a + pltpu.roll(b, n - 1, axis=0)
    hact = hsum * jax.nn.sigmoid(hsum)
    o_ref[0, 0] = jnp.dot(hact.astype(BF16), w2_ref[0], preferred_element_type=F32).astype(o_ref.dtype)


def _cmp_weights(pe, w1, w2):
    half = CMP_BLOCK // 2
    halves = []
    pes = []
    for lo in (0, half):
        w = w1[lo:lo + half]
        z = jnp.zeros_like(w)
        w_g0 = jnp.concatenate([w, z], axis=-1)
        w_g1 = jnp.concatenate([z, w], axis=-1)
        halves.append(jnp.stack([w_g0, w_g1], axis=1).reshape(half * KV_SLAB, KV_GROUPS * CMP_HIDDEN))
        p = pe[lo:lo + half]
        pes.append(jnp.broadcast_to(p[:, None, :], (half, KV_GROUPS, HEAD_DIM)).reshape(1, half * KV_SLAB))
    z2 = jnp.zeros_like(w2)
    w2_blk = jnp.concatenate([jnp.concatenate([w2, z2], axis=1), jnp.concatenate([z2, w2], axis=1)], axis=0)
    return jnp.stack(halves).astype(BF16), jnp.stack(pes), w2_blk.astype(BF16)


def _compress(ch, w1s, pes, w2s):
    kinds, b, nchunk, width = ch.shape
    hid = w1s.shape[-1]
    return pl.pallas_call(
        _cmp_kernel,
        grid=(kinds, b),
        in_specs=[pl.BlockSpec((1, 1, nchunk, width), lambda k, i: (k, i, 0, 0)),
                  pl.BlockSpec((1, 2, width, hid), lambda k, i: (k, 0, 0, 0)),
                  pl.BlockSpec((1, 2, 1, width), lambda k, i: (k, 0, 0, 0)),
                  pl.BlockSpec((1, hid, KV_SLAB), lambda k, i: (k, 0, 0))],
        out_specs=pl.BlockSpec((1, 1, nchunk, KV_SLAB), lambda k, i: (k, i, 0, 0)),
        out_shape=jax.ShapeDtypeStruct((kinds, b, nchunk, KV_SLAB), BF16),
        compiler_params=_cparams(("arbitrary", "arbitrary")),
        name="compress",
    )(ch, w1s, pes, w2s)


def _softmax_parts(s):
    m = jnp.max(s, axis=-1, keepdims=True)
    e = jnp.exp(s - m)
    return m, e, jnp.sum(e, axis=-1, keepdims=True)


def _attn_kernel(q_ref, kc_ref, vc_ref, ks0_ref, ks1_ref, vs0_ref, vs1_ref, kw0_ref, kw1_ref,
                 vw0_ref, vw1_ref, zg_ref,
                 ov_ref, cf_ref, hot_ref, tri_ref, wm_ref, o_ref, v_scr, lhs_scr, m_scr, acc_scr, part_scr,
                 *, n_cmp, n_sel):
    qb = pl.program_id(1)
    t0 = qb * Q_BLOCK
    t0f = t0.astype(F32)
    ncp = kc_ref.shape[2]
    gates = jax.nn.sigmoid(zg_ref[...])
    r_col = lax.broadcasted_iota(jnp.int32, (Q_BLOCK, 1), 0)
    lane = lax.broadcasted_iota(jnp.int32, (Q_BLOCK, LANES), 1)
    tri = tri_ref[...]
    wmask = wm_ref[jnp.minimum(qb, WIN_BACK)]
    hrows = [slice(h * Q_BLOCK, (h + 1) * Q_BLOCK) for h in range(HPG)]

    t0a = pl.multiple_of(t0, Q_BLOCK)
    w0 = pl.multiple_of(jnp.maximum(qb - WIN_BACK, 0) * Q_BLOCK, Q_BLOCK)

    scores = []
    for g in range(KV_GROUPS):
        slopes = [2.0 ** -(g * HPG + h + 1) for h in range(HPG)]
        ks_ref = (ks0_ref, ks1_ref)[g]
        kw_ref = (kw0_ref, kw1_ref)[g]
        f0 = _group_lane0(1 - g)
        own_c = lax.broadcasted_iota(jnp.int32, (ncp, LANES), 1)
        own_c = (own_c < HEAD_DIM) if g == 0 else (own_c >= HEAD_DIM)

        def feature_rows(values_of_slope):
            blocks = []
            for h in range(HPG):
                f = jnp.zeros((Q_BLOCK, LANES), F32)
                for k, v in enumerate(values_of_slope(slopes[h])):
                    f = jnp.where(lane == f0 + k, v, f)
                blocks.append(f)
            return jnp.concatenate(blocks, axis=0)

        q4 = jnp.concatenate(
            [q_ref[:, (g * HPG + h) * LANES:(g * HPG + h + 1) * LANES] for h in range(HPG)],
            axis=0).astype(F32)
        lhs_pos = (q4 + feature_rows(lambda s: (64.0 * s, s, -s * t0f))).astype(BF16)
        lhs_cmp = (q4 + feature_rows(
            lambda s: (256.0 * s, 16.0 * s, (CMP_BLOCK - 1.0) * s, -s * t0f))).astype(BF16)

        kc_aug = jnp.where(own_c, kc_ref[0, 0], cf_ref[g])
        sc = _nt_dot(lhs_cmp, kc_aug)
        sd = _nt_dot(lhs_pos, ks_ref[pl.ds(t0a, Q_BLOCK), :])
        sw = _nt_dot(lhs_pos, kw_ref[pl.ds(w0, WIN_KEYS), :])
        scores.append((lhs_pos, sc, sd, sw))

    for g in range(KV_GROUPS):
        lhs_pos, sc, sd, sw = scores[g]
        vs_ref = (vs0_ref, vs1_ref)[g]
        vw_ref = (vw0_ref, vw1_ref)[g]
        f0 = _group_lane0(1 - g)

        n_i = lax.broadcasted_iota(jnp.int32, (Q_BLOCK, ncp), 1)
        r_i = lax.broadcasted_iota(jnp.int32, (Q_BLOCK, ncp), 0)
        valid_c = jnp.logical_and((n_i * CMP_STRIDE + (CMP_BLOCK - 1)) <= (t0 + r_i), n_i < n_cmp)
        row_any = ((t0 + r_col) >= (CMP_BLOCK - 1)).astype(F32)
        sc = jnp.where(valid_c[None], sc.reshape(HPG, Q_BLOCK, ncp), NEG_INF)
        _, e, l = _softmax_parts(sc)
        p_c = (e * (row_any[None] / l)).astype(BF16)
        o_c = jnp.dot(p_c.reshape(HPG * Q_BLOCK, ncp), vc_ref[0, 0], preferred_element_type=F32)
        imp = jnp.dot(jnp.concatenate([p_c[h] for h in range(HPG)], axis=1), ov_ref[...],
                      preferred_element_type=F32)

        imp_t = imp.T[:n_sel]
        j_i = lax.broadcasted_iota(jnp.int32, (n_sel, Q_BLOCK), 0)
        cur = lax.shift_right_logical(t0 + lax.broadcasted_iota(jnp.int32, (n_sel, Q_BLOCK), 1),
                                      int(math.log2(SEL_BLOCK)))
        forced = jnp.logical_or(j_i == 0, jnp.logical_or(j_i == cur, j_i == cur - 1))
        valid = j_i <= cur
        v = jnp.where(forced, SEL_FORCE, jnp.where(valid, imp_t, -SEL_FORCE))
        v_scr[...] = v
        n_chunk = n_sel // SUBLANES
        chunks = [v[k * SUBLANES:(k + 1) * SUBLANES] for k in range(n_chunk)]
        ranks = [jnp.zeros((SUBLANES, Q_BLOCK), jnp.int32) for _ in range(n_chunk)]
        sub_i = lax.broadcasted_iota(jnp.int32, (SUBLANES, Q_BLOCK), 0)
        for i in range(n_sel):
            vi = v_scr[i:i + 1, :]
            ki = i // SUBLANES
            for k in range(n_chunk):
                if k > ki:
                    beats = jnp.where(vi >= chunks[k], 1, 0)
                elif k < ki:
                    beats = jnp.where(vi > chunks[k], 1, 0)
                else:
                    beats = jnp.where(sub_i > (i - ki * SUBLANES),
                                      jnp.where(vi >= chunks[k], 1, 0),
                                      jnp.where(vi > chunks[k], 1, 0))
                ranks[k] = ranks[k] + beats
        rank = jnp.concatenate(ranks, axis=0)
        chosen = jnp.logical_and(jnp.logical_and(rank < SEL_TOPK, valid), j_i < 2 * qb)
        bias_t = jnp.where(chosen, 0.0, NEG_INF)
        bias_t = jnp.concatenate([bias_t, jnp.zeros((LANES - n_sel, Q_BLOCK), F32)], axis=0)
        bias = bias_t.T.astype(BF16)
        lhs_scr[g] = jnp.concatenate([lhs_pos, jnp.concatenate([bias] * HPG, axis=0)], axis=1)

        vd = vs_ref[pl.ds(t0a, Q_BLOCK), :]
        sd = (sd.reshape(HPG, Q_BLOCK, Q_BLOCK) + tri[None]).reshape(HPG * Q_BLOCK, Q_BLOCK)
        m = jnp.max(sd, axis=-1, keepdims=True)
        m_scr[g] = jnp.broadcast_to(m, m_scr.shape[1:])
        acc_scr[g] = jnp.dot(jnp.exp(sd - m).astype(BF16), vd, preferred_element_type=F32)

        v_w = vw_ref[pl.ds(w0, WIN_KEYS), :]
        sw = (sw.reshape(HPG, Q_BLOCK, WIN_KEYS) + wmask[None]).reshape(HPG * Q_BLOCK, WIN_KEYS)
        e = jnp.exp(sw - jnp.max(sw, axis=-1, keepdims=True))
        o_w = jnp.dot(e.astype(BF16), v_w, preferred_element_type=F32)
        o_w = o_w * (1.0 / o_w[:, f0 + ONE_LANE:f0 + ONE_LANE + 1])

        for h in range(HPG):
            c = (g * HPG + h) * 3
            part_scr[g, hrows[h]] = gates[:, c:c + 1] * o_c[hrows[h]] + gates[:, c + 2:c + 3] * o_w[hrows[h]]

    n_tiles = (t0 + SEL_TK - 1) // SEL_TK

    def tile_scores(j):
        off = pl.multiple_of(j * SEL_TK, SEL_TK)
        hot = hot_ref[pl.ds(off, SEL_TK), :]
        return [_nt_dot(lhs_scr[g], jnp.concatenate([(ks0_ref, ks1_ref)[g][pl.ds(off, SEL_TK), :], hot], axis=1))
                for g in range(KV_GROUPS)]

    def tile_update(j, s_g):
        off = pl.multiple_of(j * SEL_TK, SEL_TK)
        for g in range(KV_GROUPS):
            vs_ref = (vs0_ref, vs1_ref)[g]
            s = s_g[g]
            m_old = m_scr[g]
            m_new = jnp.maximum(m_old, jnp.max(s, axis=-1, keepdims=True))
            p = jnp.exp(s - jnp.tile(m_new, (1, SEL_TK // LANES)))
            alpha = jnp.exp(m_old - m_new)
            m_scr[g] = m_new
            pv = jnp.dot(p.astype(BF16), vs_ref[pl.ds(off, SEL_TK), :], preferred_element_type=F32)
            acc_scr[g] = alpha * acc_scr[g] + pv

    def sel_body(j, carry):
        tile_update(j, tile_scores(j))
        return carry

    lax.fori_loop(0, n_tiles, sel_body, 0)

    for g in range(KV_GROUPS):
        one = _group_lane0(1 - g) + ONE_LANE
        acc = acc_scr[g]
        o_s = acc * (1.0 / acc[:, one:one + 1])
        keep = (lane < HEAD_DIM) if g == 0 else (lane >= HEAD_DIM)
        for h in range(HPG):
            c = (g * HPG + h) * 3
            out = part_scr[g, hrows[h]] + gates[:, c + 1:c + 2] * o_s[hrows[h]]
            out = jnp.where(keep, out, 0.0)
            slot = (g * HPG + h) * LANES
            o_ref[:, slot:slot + LANES] = out.astype(o_ref.dtype)


def _attention_tables(seq, ncp, n_cmp, n_sel):
    cstart = np.arange(ncp) * CMP_STRIDE
    sstart = np.arange(n_sel) * SEL_BLOCK
    overlap = ((cstart[:, None] < sstart[None, :] + SEL_BLOCK)
               & (cstart[:, None] + CMP_BLOCK > sstart[None, :])
               & (np.arange(ncp)[:, None] < n_cmp)).astype(np.float32)
    overlap = np.pad(overlap, ((0, 0), (0, LANES - n_sel)))
    ov4 = np.tile(overlap, (HPG, 1))
    n = np.arange(ncp)
    cfeat = np.zeros((KV_GROUPS, ncp, LANES), np.float32)
    for g in range(KV_GROUPS):
        l0 = _group_lane0(1 - g)
        cfeat[g, :, l0] = n // 16
        cfeat[g, :, l0 + 1] = n % 16
        cfeat[g, :, l0 + 2] = 1.0
        cfeat[g, :, l0 + 3] = 1.0
    hot = (np.arange(seq)[:, None] // SEL_BLOCK == np.arange(LANES)[None, :]).astype(np.float32)
    r = np.arange(Q_BLOCK)[:, None]
    tri = np.where(np.arange(Q_BLOCK)[None, :] <= r, 0.0, NEG_INF).astype(np.float32)
    c = np.arange(WIN_KEYS)[None, :]
    wm = []
    for qb in range(WIN_BACK + 1):
        dist = (qb * Q_BLOCK + r) - c if qb < WIN_BACK else (WINDOW + r) - c
        wm.append(np.where((dist >= 0) & (dist < WINDOW), 0.0, NEG_INF))
    wm = np.stack(wm).astype(np.float32)
    return (jnp.asarray(ov4, BF16), jnp.asarray(cfeat, BF16), jnp.asarray(hot, BF16),
            jnp.asarray(tri), jnp.asarray(wm))


def _attention(zq, kvc, zkv, zg, batch, seq):
    n = zq.shape[0]
    nqb = seq // Q_BLOCK
    ncp = kvc.shape[2]
    n_cmp = seq // CMP_STRIDE - 1
    n_sel = seq // SEL_BLOCK
    assert n_sel <= HEAD_DIM and n_sel % SUBLANES == 0 and seq % SEL_TK == 0 and nqb > WIN_BACK
    assert ncp // 16 <= 256 and seq // SEL_BLOCK <= 256
    ov4, cfeat, hot, tri, wm = _attention_tables(seq, ncp, n_cmp, n_sel)

    kernel = functools.partial(_attn_kernel, n_cmp=n_cmp, n_sel=n_sel)
    kv_spec = lambda idx: pl.BlockSpec((seq, KV_SLAB), lambda b, i, idx=idx: (b, idx))
    return pl.pallas_call(
        kernel,
        grid=(batch, nqb),
        in_specs=[pl.BlockSpec((Q_BLOCK, Q_COLS), lambda b, i: (b * nqb + i, 0)),
                  pl.BlockSpec((1, 1, ncp, KV_SLAB), lambda b, i: (0, b, 0, 0)),
                  pl.BlockSpec((1, 1, ncp, KV_SLAB), lambda b, i: (1, b, 0, 0)),
                  *[kv_spec(k) for k in range(8)],
                  pl.BlockSpec((Q_BLOCK, GATE_COLS), lambda b, i: (b * nqb + i, 0)),
                  _const_spec(ov4.shape), _const_spec(cfeat.shape), _const_spec(hot.shape),
                  _const_spec(tri.shape), _const_spec(wm.shape)],
        out_specs=pl.BlockSpec((Q_BLOCK, Q_COLS), lambda b, i: (b * nqb + i, 0)),
        out_shape=jax.ShapeDtypeStruct((n, Q_COLS), BF16),
        scratch_shapes=[pltpu.VMEM((n_sel, Q_BLOCK), F32),
                        pltpu.VMEM((KV_GROUPS, HPG * Q_BLOCK, 2 * LANES), BF16),
                        pltpu.VMEM((KV_GROUPS, HPG * Q_BLOCK, LANES), F32),
                        pltpu.VMEM((KV_GROUPS, HPG * Q_BLOCK, LANES), F32),
                        pltpu.VMEM((KV_GROUPS, HPG * Q_BLOCK, LANES), F32)],
        compiler_params=_cparams(("arbitrary", "arbitrary")),
        name="nsa_attn",
    )(zq, kvc, kvc, *([zkv] * 8), zg, ov4, cfeat, hot, tri, wm)


def _s5_kernel(u_ref, lagk_ref, mre_ref, mim_ref, cre_ref, cim_ref, lre_ref, lim_ref, d_ref, y_ref,
               ere, eim, xre, xim, tp_scr, *, n_chunks):
    def tok(t):
        return pl.ds(t, n_chunks, stride=S5_CHUNK)

    @pl.when(pl.program_id(1) == 0)
    def _():
        tp_scr[...] = jnp.zeros(tp_scr.shape, tp_scr.dtype)
        for s in range(S5_CHUNK):
            for t in range(s, S5_CHUNK):
                tp_scr[s * LANES:(s + 1) * LANES, t * LANES:(t + 1) * LANES] = lagk_ref[0, t - s]

    u = jnp.concatenate([u_ref[tok(t), :] for t in range(S5_CHUNK)], axis=1).astype(BF16)
    y = jnp.dot(u, tp_scr[...], preferred_element_type=F32)
    ere[...] = jnp.dot(u, mre_ref[0], preferred_element_type=F32)
    eim[...] = jnp.dot(u, mim_ref[0], preferred_element_type=F32)
    lr = lre_ref[0]
    li = lim_ref[0]

    def body(c, carry):
        xr, xi = carry
        row = pl.ds(c, 1)
        xre[row, :] = xr
        xim[row, :] = xi
        er = ere[row, :]
        ei = eim[row, :]
        return lr * xr - li * xi + er, lr * xi + li * xr + ei

    zero = jnp.zeros((1, ere.shape[1]), F32)
    lax.fori_loop(0, n_chunks, body, (zero, zero), unroll=8)
    y = y + jnp.dot(xre[...].astype(BF16), cre_ref[0], preferred_element_type=F32)
    y = y + jnp.dot(xim[...].astype(BF16), cim_ref[0], preferred_element_type=F32)
    d = d_ref[0]
    for t in range(S5_CHUNK):
        y_ref[tok(t), :] = y[:, t * LANES:(t + 1) * LANES] + d * u_ref[tok(t), :]


def _s5_matrices(a_re, a_im, b_re, b_im, c_re, c_im, d_skip, log_dt):
    t = S5_CHUNK
    hp = lax.Precision.HIGHEST
    dt = jnp.exp(log_dt)[:, None]
    lam_re = jnp.minimum(a_re, -1e-4)
    lam_im = a_im
    mag = jnp.exp(lam_re * dt)
    ang = lam_im * dt
    lb_re = mag * jnp.cos(ang)
    lb_im = mag * jnp.sin(ang)
    den = lam_re * lam_re + lam_im * lam_im
    nr = lb_re - 1.0
    coef_re = (nr * lam_re + lb_im * lam_im) / den
    coef_im = (lb_im * lam_re - nr * lam_im) / den
    bb_re = coef_re[..., None] * b_re - coef_im[..., None] * b_im
    bb_im = coef_re[..., None] * b_im + coef_im[..., None] * b_re
    j = jnp.arange(t + 1, dtype=F32)[:, None, None]
    pmag = jnp.exp(j * (lam_re * dt)[None])
    pw_re = pmag * jnp.cos(j * ang[None])
    pw_im = pmag * jnp.sin(j * ang[None])
    cl_re = c_re[None] * pw_re[:, :, None, :] - c_im[None] * pw_im[:, :, None, :]
    cl_im = c_re[None] * pw_im[:, :, None, :] + c_im[None] * pw_re[:, :, None, :]
    kern = (jnp.einsum('jgop,gpi->gjoi', cl_re[:t], bb_re, precision=hp)
            - jnp.einsum('jgop,gpi->gjoi', cl_im[:t], bb_im, precision=hp))
    eye = jnp.eye(S5_LB, dtype=F32)
    nlb = S5_GROUPS // S5_LB
    lagk = jnp.einsum('qgjoi,gh->qjgiho', kern.reshape(nlb, S5_LB, t, S5_GROUP, S5_GROUP), eye,
                      precision=hp).reshape(nlb, t, LANES, LANES)
    rv_re = pw_re[t - 1 - np.arange(t)]
    rv_im = pw_im[t - 1 - np.arange(t)]
    m_re = rv_re[..., None] * bb_re[None] - rv_im[..., None] * bb_im[None]
    m_im = rv_re[..., None] * bb_im[None] + rv_im[..., None] * bb_re[None]

    def lay_m(m):
        m = m.reshape(t, nlb, S5_LB, S5_STATE, S5_GROUP)
        return jnp.einsum('sqgpi,gh->qsgihp', m, eye, precision=hp).reshape(
            nlb, t * LANES, S5_LB * S5_STATE)

    def lay_c(c):
        c = c.reshape(t, nlb, S5_LB, S5_GROUP, S5_STATE)
        return jnp.einsum('tqgop,gh->qgptho', c, eye, precision=hp).reshape(
            nlb, S5_LB * S5_STATE, t * LANES)

    lam_t_re = pw_re[t].reshape(nlb, 1, S5_LB * S5_STATE)
    lam_t_im = pw_im[t].reshape(nlb, 1, S5_LB * S5_STATE)
    d_row = d_skip.reshape(nlb, 1, LANES)
    return (lagk.astype(BF16), lay_m(m_re).astype(BF16), lay_m(m_im).astype(BF16),
            lay_c(cl_re[1:]).astype(BF16), lay_c(-cl_im[1:]).astype(BF16), lam_t_re, lam_t_im, d_row)


def _s5(zs, mats, batch, seq):
    n_chunks = seq // S5_CHUNK
    lagk, m_re, m_im, cp_re, cp_im, l_re, l_im, d_row = mats
    nlb = lagk.shape[0]
    wcol = S5_CHUNK * LANES
    wst = S5_LB * S5_STATE
    kernel = functools.partial(_s5_kernel, n_chunks=n_chunks)
    p3 = lambda r, c: pl.BlockSpec((1, r, c), lambda q, b: (q, 0, 0))
    return pl.pallas_call(
        kernel,
        grid=(nlb, batch),
        in_specs=[pl.BlockSpec((seq, LANES), lambda q, b: (b, q)),
                  pl.BlockSpec((1, S5_CHUNK, LANES, LANES), lambda q, b: (q, 0, 0, 0)),
                  p3(wcol, wst), p3(wcol, wst), p3(wst, wcol), p3(wst, wcol),
                  p3(1, wst), p3(1, wst), p3(1, LANES)],
        out_specs=pl.BlockSpec((seq, LANES), lambda q, b: (b, q)),
        out_shape=jax.ShapeDtypeStruct(zs.shape, F32),
        scratch_shapes=[pltpu.VMEM((n_chunks, wst), F32) for _ in range(4)]
        + [pltpu.VMEM((wcol, wcol), BF16)],
        compiler_params=_cparams(("arbitrary", "arbitrary")),
        name="s5",
    )(zs, lagk, m_re, m_im, cp_re, cp_im, l_re, l_im, d_row)


def _merge_kernel(oa_ref, ys_ref, zm_ref, x_ref, mod_ref, wn_ref, wg_ref, wo_ref, g_ref, b_ref, o_ref,
                  *, alpha):
    d = x_ref.shape[-1]
    y_a = jnp.dot(oa_ref[...], wn_ref[...], preferred_element_type=F32)
    gl = jax.nn.gelu(ys_ref[...], approximate=True)
    zz = jnp.dot(gl.astype(BF16), wg_ref[...], preferred_element_type=F32)
    y_b = zz[:, :d] * jax.nn.sigmoid(zz[:, d:])
    zm = zm_ref[...]
    mix_in = jax.nn.sigmoid(zm[:, :d]) * y_a + jax.nn.sigmoid(zm[:, d:]) * y_b
    mix = jnp.dot(mix_in.astype(BF16), wo_ref[...], preferred_element_type=F32)
    gate = mod_ref[0, 2:3, :]
    r = alpha * x_ref[...] + gate * mix
    o_ref[...] = _layer_norm(r) * g_ref[...] + b_ref[...]


def _merge(oa, ys, zm, x2, mod3, wn_pad, wg, wo, ln_g, ln_b, seq, alpha):
    n, d = x2.shape
    tiles_per_batch = seq // TM_MERGE
    row = lambda w: pl.BlockSpec((TM_MERGE, w), lambda i: (i, 0))
    return pl.pallas_call(
        functools.partial(_merge_kernel, alpha=alpha),
        grid=(n // TM_MERGE,),
        in_specs=[row(oa.shape[1]), row(ys.shape[1]), row(zm.shape[1]), row(d),
                  pl.BlockSpec((1, 6, d), lambda i: (i // tiles_per_batch, 0, 0)),
                  _const_spec(wn_pad.shape), _const_spec(wg.shape), _const_spec(wo.shape),
                  _const_spec((1, d)), _const_spec((1, d))],
        out_specs=row(d),
        out_shape=jax.ShapeDtypeStruct((n, d), F32),
        compiler_params=_cparams(("arbitrary",)),
        name="merge",
    )(oa, ys, zm, x2, mod3, wn_pad, wg, wo, ln_g.reshape(1, d), ln_b.reshape(1, d))


def _pad_nsa_out(w):
    d = w.shape[1]
    w = w.reshape(NSA_HEADS, HEAD_DIM, d)
    z = jnp.zeros_like(w)
    is_g0 = (np.arange(NSA_HEADS) < HPG)[:, None, None]
    out = jnp.concatenate([jnp.where(is_g0, w, z), jnp.where(is_g0, z, w)], axis=1)
    return out.reshape(NSA_HEADS * LANES, d).astype(BF16)


def _ffn_kernel(x_ref, mod_ref, wup_ref, cw_ref, cb_ref, wdn_ref, g_ref, b_ref, o_ref, tail_ref, act_scr,
                *, alpha, tiles_per_batch):
    i = pl.program_id(0)
    tm = x_ref.shape[0]

    @pl.when(i % tiles_per_batch == 0)
    def _():
        tail_ref[...] = jnp.zeros(tail_ref.shape, F32)

    x = x_ref[...]
    shift = mod_ref[0, 3:4, :]
    scale = mod_ref[0, 4:5, :]
    gate = mod_ref[0, 5:6, :]
    h2 = (_layer_norm(x) * (1.0 + scale) + shift).astype(BF16)
    row = lax.broadcasted_iota(jnp.int32, (SUBLANES, FF_CHUNK), 0)

    def shift_rows(a, prev_rows):
        rolled = pltpu.roll(a, 1, axis=0)
        head = jnp.where(row == 0, prev_rows[SUBLANES - 1:SUBLANES], rolled[:SUBLANES])
        return jnp.concatenate([head, rolled[SUBLANES:]], axis=0)

    def conv_cols(c0):
        cols = slice(c0, c0 + FF_CHUNK)
        up = jnp.dot(h2, wup_ref[:, cols], preferred_element_type=F32)
        tail = tail_ref[:, cols]
        tail_ref[:, cols] = up[tm - SUBLANES:tm]
        w = cw_ref[:, cols]
        inner = shift_rows(w[0:1] * up, w[0:1] * tail) + w[1:2] * up
        prev_inner = pltpu.roll(w[0:1] * tail, 1, axis=0) + w[1:2] * tail
        return shift_rows(inner, prev_inner) + w[2:3] * up + cb_ref[:, cols]

    for k in range(D_FF // FF_CHUNK):
        val = conv_cols(k * FF_CHUNK)
        gte = conv_cols(D_FF + k * FF_CHUNK)
        act_scr[:, k * FF_CHUNK:(k + 1) * FF_CHUNK] = (gte * jax.nn.sigmoid(gte) * val).astype(BF16)
    ff = jnp.dot(act_scr[...], wdn_ref[...], preferred_element_type=F32)
    r = alpha * x + gate * ff
    o_ref[...] = _layer_norm(r) * g_ref[...] + b_ref[...]


def _ffn(x1, mod3, wup, conv_w, conv_b, wdn, ln_g, ln_b, seq, alpha):
    n, d = x1.shape
    tiles_per_batch = seq // TM_FFN
    ff2 = wup.shape[1]
    return pl.pallas_call(
        functools.partial(_ffn_kernel, alpha=alpha, tiles_per_batch=tiles_per_batch),
        grid=(n // TM_FFN,),
        in_specs=[pl.BlockSpec((TM_FFN, d), lambda i: (i, 0)),
                  pl.BlockSpec((1, 6, d), lambda i: (i // tiles_per_batch, 0, 0)),
                  _const_spec(wup.shape), _const_spec(conv_w.shape), _const_spec((1, ff2)),
                  _const_spec(wdn.shape), _const_spec((1, d)), _const_spec((1, d))],
        out_specs=pl.BlockSpec((TM_FFN, d), lambda i: (i, 0)),
        out_shape=jax.ShapeDtypeStruct((n, d), F32),
        scratch_shapes=[pltpu.VMEM((SUBLANES, ff2), F32), pltpu.VMEM((TM_FFN, ff2 // 2), BF16)],
        compiler_params=_cparams(("arbitrary",)),
        name="ffn",
    )(x1, mod3, wup, conv_w, conv_b.reshape(1, ff2), wdn, ln_g.reshape(1, d), ln_b.reshape(1, d))


def kernel(x, c, w_ada, b_ada, w_in, pe_ck, w_ck1, w_ck2, pe_cv, w_cv1, w_cv2, w_nsa_out,
           s5_a_re, s5_a_im, s5_b_re, s5_b_im, s5_c_re, s5_c_im, s5_d, s5_log_dt, w_s5_glu,
           w_o, ln1_g, ln1_b, w_up, conv_w, conv_b, w_down, ln2_g, ln2_b):
    batch, seq, d = x.shape
    depth = w_ada.shape[0]
    alpha = (2.0 * depth) ** 0.25
    n = batch * seq
    n_chunk16 = seq // CMP_STRIDE
    xf = x.reshape(n, d)
    for l in range(depth):
        mod3 = _ada(c, w_ada[l], b_ada[l]).reshape(batch, 6, d)

        zq, zck, zcv, zkv, zg, zs, zm = _inproj(xf, mod3, _build_w_all(w_in[l]), seq)

        wk1, pek, wk2 = _cmp_weights(pe_ck[l], w_ck1[l], w_ck2[l])
        wv1, pev, wv2 = _cmp_weights(pe_cv[l], w_cv1[l], w_cv2[l])
        ch = jnp.stack([zck, zcv]).reshape(2, batch, n_chunk16, CMP_STRIDE * KV_SLAB)
        kvc = _compress(ch, jnp.stack([wk1, wv1]), jnp.stack([pek, pev]), jnp.stack([wk2, wv2]))

        oa = _attention(zq, kvc, zkv, zg, batch, seq)

        mats = _s5_matrices(s5_a_re[l], s5_a_im[l], s5_b_re[l], s5_b_im[l], s5_c_re[l], s5_c_im[l],
                            s5_d[l], s5_log_dt[l])
        ys = _s5(zs, mats, batch, seq)

        x1 = _merge(oa, ys, zm, xf, mod3, _pad_nsa_out(w_nsa_out[l]), w_s5_glu[l].astype(BF16),
                    w_o[l].astype(BF16), ln1_g[l], ln1_b[l], seq, alpha)

        xf = _ffn(x1, mod3, w_up[l].astype(BF16), conv_w[l], conv_b[l], w_down[l].astype(BF16),
                  ln2_g[l], ln2_b[l], seq, alpha)
    return xf.reshape(batch, seq, d)
```

```python
import functools
import math

import jax
import jax.numpy as jnp
import ml_dtypes
import numpy as np
from jax import lax
from jax.experimental import pallas as pl
from jax.experimental.pallas import tpu as pltpu

F32 = jnp.float32
BF16 = jnp.bfloat16

D_MODEL = 1024
NSA_HEADS = 8
KV_GROUPS = 2
HPG = NSA_HEADS // KV_GROUPS
HEAD_DIM = 64
CMP_BLOCK = 32
CMP_STRIDE = 16
CMP_HIDDEN = 128
SEL_BLOCK = 64
SEL_TOPK = 16
WINDOW = 512
Q_BLOCK = 128
S5_GROUP = 16
S5_WIDTH = 512
S5_GROUPS = S5_WIDTH // S5_GROUP
S5_STATE = 64
D_FF = 2816
CONV_WIDTH = 3
LN_EPS = 1e-5
NEG_INF = -1e30
SEL_FORCE = 1e9

LANES = 128
SUBLANES = 8
VMEM_LIMIT = 56 * 1024 * 1024

TM_IN = 512
TM_MERGE = 512
TM_FFN = 512
FF_CHUNK = 256
SEL_TK = 512
WIN_KEYS = WINDOW + Q_BLOCK
WIN_BACK = WINDOW // Q_BLOCK
S5_CHUNK = 8
S5_LB = LANES // S5_GROUP
S5_NLB = S5_WIDTH // LANES

Q_COLS = NSA_HEADS * LANES
KV_SLAB = KV_GROUPS * HEAD_DIM
GATE_COLS = LANES
MERGE_COLS = 2 * D_MODEL
KV_OUT_COLS = 8 * KV_SLAB
F_HI, F_LO = 0, 3
ONE_LANE = 6
F_CHI, F_CLO, F_COFF = 7, 10, 13
LOG2E = 1.4426950408889634


def _cparams(sem):
    return pltpu.CompilerParams(dimension_semantics=sem, vmem_limit_bytes=VMEM_LIMIT)


def _const_spec(shape):
    n = len(shape)
    return pl.BlockSpec(shape, lambda *_: (0,) * n)


def _layer_norm(x):
    mu = jnp.mean(x, axis=-1, keepdims=True)
    xc = x - mu
    var = jnp.mean(xc * xc, axis=-1, keepdims=True)
    return xc * lax.rsqrt(var + LN_EPS)


def _nt_dot(a, b):
    return lax.dot_general(a, b, (((1,), (1,)), ((), ())), preferred_element_type=F32)


def _group_lane0(g):
    return g * HEAD_DIM


def _ada_kernel(c_ref, w_ref, b_ref, o_ref):
    c = c_ref[...]
    a = c * jax.nn.sigmoid(c)
    o_ref[...] = jnp.dot(a, w_ref[...], preferred_element_type=F32,
                         precision=lax.Precision.HIGHEST) + b_ref[...]


def _ada(c, w_ada, b_ada):
    b, d = c.shape
    n = w_ada.shape[1]
    blk = 1024
    return pl.pallas_call(
        _ada_kernel,
        grid=(n // blk,),
        in_specs=[pl.BlockSpec((b, d), lambda j: (0, 0)),
                  pl.BlockSpec((d, blk), lambda j: (0, j)),
                  pl.BlockSpec((1, blk), lambda j: (0, j))],
        out_specs=pl.BlockSpec((b, blk), lambda j: (0, j)),
        out_shape=jax.ShapeDtypeStruct((b, n), F32),
        compiler_params=_cparams(("arbitrary",)),
        name="ada",
    )(c, w_ada, b_ada.reshape(1, n))


def _inproj_kernel(x_ref, mod_ref, w_ref, pf_ref, qf_ref, q_ref, c_ref, kv_ref, g_ref, s_ref, m_ref, zc_scr):
    hn = _layer_norm(x_ref[...])
    shift = mod_ref[0, 0:1, :]
    scale = mod_ref[0, 1:2, :]
    h = (hn * (1.0 + scale) + shift).astype(BF16)
    tm = x_ref.shape[0]

    def proj(col, width):
        return jnp.dot(h, w_ref[:, col:col + width], preferred_element_type=F32)

    col = 0
    q_ref[...] = (proj(col, Q_COLS) + qf_ref[...]).astype(q_ref.dtype)
    col += Q_COLS
    for kind in range(2):
        zc_scr[...] = proj(col, KV_SLAB)
        for tok in range(CMP_STRIDE):
            c_ref[kind, :, tok * KV_SLAB:(tok + 1) * KV_SLAB] = zc_scr[
                pl.ds(tok, tm // CMP_STRIDE, stride=CMP_STRIDE), :]
        col += KV_SLAB
    lane = lax.broadcasted_iota(jnp.int32, (x_ref.shape[0], LANES), 1)
    pf = pf_ref[...]
    for part in range(4):
        z = proj(col + part * KV_SLAB, KV_SLAB)
        for g in range(KV_GROUPS):
            own = (lane < HEAD_DIM) if g == 0 else (lane >= HEAD_DIM)
            slot = (part * KV_GROUPS + g) * KV_SLAB
            kv_ref[:, slot:slot + KV_SLAB] = jnp.where(
                own, z, pf[:, g * LANES:(g + 1) * LANES]).astype(kv_ref.dtype)
    col += 4 * KV_SLAB
    g_ref[...] = proj(col, GATE_COLS)
    col += GATE_COLS
    s_ref[...] = proj(col, S5_WIDTH)
    col += S5_WIDTH
    m_ref[...] = proj(col, MERGE_COLS)


def _log2e_terms():
    terms, rest = [], np.float64(LOG2E)
    for _ in range(3):
        t = np.float64(np.float32(rest).astype(ml_dtypes.bfloat16))
        terms.append(float(t))
        rest -= t
    return terms


def _key_position_features(seq):
    p = np.arange(seq)
    out = np.zeros((seq, KV_GROUPS, LANES), np.float32)
    for g in range(KV_GROUPS):
        l0 = _group_lane0(1 - g)
        for k in range(3):
            out[:, g, l0 + F_HI + k] = p // SEL_BLOCK
            out[:, g, l0 + F_LO + k] = p % SEL_BLOCK
        out[:, g, l0 + ONE_LANE] = 1.0
    return jnp.asarray(out.reshape(seq, KV_GROUPS * LANES))


def _query_feature_row():
    c = _log2e_terms()
    row = np.zeros((NSA_HEADS, LANES), np.float32)
    for hd in range(NSA_HEADS):
        g = hd // HPG
        slope = 2.0 ** -(hd + 1)
        l0 = _group_lane0(1 - g)
        for k in range(3):
            row[hd, l0 + F_HI + k] = SEL_BLOCK * slope * c[k]
            row[hd, l0 + F_LO + k] = slope * c[k]
            row[hd, l0 + F_CHI + k] = 16 * CMP_STRIDE * slope * c[k]
            row[hd, l0 + F_CLO + k] = CMP_STRIDE * slope * c[k]
            row[hd, l0 + F_COFF + k] = slope * c[k]
    return jnp.asarray(row.reshape(1, Q_COLS))


def _inproj(x2, mod3, w_all, seq):
    n, d = x2.shape
    tiles_per_batch = seq // TM_IN
    rows16 = TM_IN // CMP_STRIDE
    widths = (Q_COLS, KV_OUT_COLS, GATE_COLS, S5_WIDTH, MERGE_COLS)
    dtypes = (BF16, BF16, F32, F32, F32)
    pf = _key_position_features(seq)
    qf = _query_feature_row()
    row_spec = lambda w: pl.BlockSpec((TM_IN, w), lambda i: (i, 0))
    out_specs = [row_spec(Q_COLS),
                 pl.BlockSpec((2, rows16, CMP_STRIDE * KV_SLAB), lambda i: (0, i, 0))]
    out_specs += [row_spec(w) for w in widths[1:]]
    out_shape = [jax.ShapeDtypeStruct((n, Q_COLS), BF16),
                 jax.ShapeDtypeStruct((2, n // CMP_STRIDE, CMP_STRIDE * KV_SLAB), F32)]
    out_shape += [jax.ShapeDtypeStruct((n, w), dt) for w, dt in zip(widths[1:], dtypes[1:])]
    return pl.pallas_call(
        _inproj_kernel,
        grid=(n // TM_IN,),
        in_specs=[pl.BlockSpec((TM_IN, d), lambda i: (i, 0)),
                  pl.BlockSpec((1, 6, d), lambda i: (i // tiles_per_batch, 0, 0)),
                  _const_spec(w_all.shape),
                  pl.BlockSpec((TM_IN, KV_GROUPS * LANES), lambda i: (i % tiles_per_batch, 0)),
                  _const_spec(qf.shape)],
        out_specs=out_specs,
        out_shape=out_shape,
        scratch_shapes=[pltpu.VMEM((TM_IN, KV_SLAB), F32)],
        compiler_params=_cparams(("arbitrary",)),
        name="inproj",
    )(x2, mod3, w_all, pf, qf)


def _build_w_all(w_in):
    d = w_in.shape[0]
    cq = NSA_HEADS * HEAD_DIM
    ckv = 6 * KV_SLAB
    cg = 3 * NSA_HEADS
    zeros = jnp.zeros((d, HEAD_DIM), w_in.dtype)
    pieces = []
    for hd in range(NSA_HEADS):
        wq = w_in[:, hd * HEAD_DIM:(hd + 1) * HEAD_DIM] * (HEAD_DIM ** -0.5 * LOG2E)
        pieces += [wq, zeros] if hd < HPG else [zeros, wq]
    wq_pad = jnp.concatenate(pieces, axis=1)
    wkv = w_in[:, cq:cq + ckv]
    wg = jnp.pad(w_in[:, cq + ckv:cq + ckv + cg], ((0, 0), (0, GATE_COLS - cg)))
    rest = w_in[:, cq + ckv + cg:]
    return jnp.concatenate([wq_pad, wkv, wg, rest], axis=1).astype(BF16)


def _cmp_kernel(ch_ref, w1_ref, pe_ref, w2_ref, o_ref):
    ch = ch_ref[0, 0]
    a = jnp.dot((ch + pe_ref[0, 0]).astype(BF16), w1_ref[0, 0], preferred_element_type=F32)
    b = jnp.dot((ch + pe_ref[0, 1]).astype(BF16), w1_ref[0, 1], preferred_element_type=F32)
    n = a.shape[0]
    hsum = a + pltpu.roll(b, n - 1, axis=0)
    hact = hsum * jax.nn.sigmoid(hsum)
    o_ref[0, 0] = jnp.dot(hact.astype(BF16), w2_ref[0], preferred_element_type=F32).astype(o_ref.dtype)


def _cmp_weights(pe, w1, w2):
    half = CMP_BLOCK // 2
    halves = []
    pes = []
    for lo in (0, half):
        w = w1[lo:lo + half]
        z = jnp.zeros_like(w)
        w_g0 = jnp.concatenate([w, z], axis=-1)
        w_g1 = jnp.concatenate([z, w], axis=-1)
        halves.append(jnp.stack([w_g0, w_g1], axis=1).reshape(half * KV_SLAB, KV_GROUPS * CMP_HIDDEN))
        p = pe[lo:lo + half]
        pes.append(jnp.broadcast_to(p[:, None, :], (half, KV_GROUPS, HEAD_DIM)).reshape(1, half * KV_SLAB))
    z2 = jnp.zeros_like(w2)
    w2_blk = jnp.concatenate([jnp.concatenate([w2, z2], axis=1), jnp.concatenate([z2, w2], axis=1)], axis=0)
    return jnp.stack(halves).astype(BF16), jnp.stack(pes), w2_blk.astype(BF16)


def _compress(ch, w1s, pes, w2s):
    kinds, b, nchunk, width = ch.shape
    hid = w1s.shape[-1]
    return pl.pallas_call(
        _cmp_kernel,
        grid=(kinds, b),
        in_specs=[pl.BlockSpec((1, 1, nchunk, width), lambda k, i: (k, i, 0, 0)),
                  pl.BlockSpec((1, 2, width, hid), lambda k, i: (k, 0, 0, 0)),
                  pl.BlockSpec((1, 2, 1, width), lambda k, i: (k, 0, 0, 0)),
                  pl.BlockSpec((1, hid, KV_SLAB), lambda k, i: (k, 0, 0))],
        out_specs=pl.BlockSpec((1, 1, nchunk, KV_SLAB), lambda k, i: (k, i, 0, 0)),
        out_shape=jax.ShapeDtypeStruct((kinds, b, nchunk, KV_SLAB), BF16),
        compiler_params=_cparams(("arbitrary", "arbitrary")),
        name="compress",
    )(ch, w1s, pes, w2s)


def _softmax_parts(s):
    m = jnp.max(s, axis=-1, keepdims=True)
    e = jnp.exp2(s - m)
    return m, e, jnp.sum(e, axis=-1, keepdims=True)


def _attn_kernel(q_ref, kc_ref, vc_ref, ks0_ref, ks1_ref, vs0_ref, vs1_ref, kw0_ref, kw1_ref,
                 vw0_ref, vw1_ref, zg_ref,
                 ov_ref, cf_ref, hot_ref, tri_ref, wm_ref, o_ref, v_scr, lhs_scr, m_scr, acc_scr, part_scr,
                 tile_smem,
                 *, n_cmp, n_sel):
    qb = pl.program_id(1)
    t0 = qb * Q_BLOCK
    ncp = kc_ref.shape[2]
    gates = jax.nn.sigmoid(zg_ref[...])
    r_col = lax.broadcasted_iota(jnp.int32, (Q_BLOCK, 1), 0)
    lane = lax.broadcasted_iota(jnp.int32, (Q_BLOCK, LANES), 1)
    tri = tri_ref[...]
    wmask = wm_ref[jnp.minimum(qb, WIN_BACK)]
    hrows = [slice(h * Q_BLOCK, (h + 1) * Q_BLOCK) for h in range(HPG)]

    t0a = pl.multiple_of(t0, Q_BLOCK)
    w0 = pl.multiple_of(jnp.maximum(qb - WIN_BACK, 0) * Q_BLOCK, Q_BLOCK)

    scores = []
    for g in range(KV_GROUPS):
        ks_ref = (ks0_ref, ks1_ref)[g]
        kw_ref = (kw0_ref, kw1_ref)[g]
        own_c = lax.broadcasted_iota(jnp.int32, (ncp, LANES), 1)
        own_c = (own_c < HEAD_DIM) if g == 0 else (own_c >= HEAD_DIM)
        lhs_pos = jnp.concatenate(
            [q_ref[:, (g * HPG + h) * LANES:(g * HPG + h + 1) * LANES] for h in range(HPG)], axis=0)
        kc_aug = jnp.where(own_c, kc_ref[0, 0], cf_ref[g])
        sc = _nt_dot(lhs_pos, kc_aug)
        sd = _nt_dot(lhs_pos, ks_ref[pl.ds(t0a, Q_BLOCK), :])
        sw = _nt_dot(lhs_pos, kw_ref[pl.ds(w0, WIN_KEYS), :])
        scores.append((lhs_pos, sc, sd, sw))

    block_used = []
    for g in range(KV_GROUPS):
        lhs_pos, sc, sd, sw = scores[g]
        vs_ref = (vs0_ref, vs1_ref)[g]
        vw_ref = (vw0_ref, vw1_ref)[g]
        f0 = _group_lane0(1 - g)

        n_i = lax.broadcasted_iota(jnp.int32, (Q_BLOCK, ncp), 1)
        r_i = lax.broadcasted_iota(jnp.int32, (Q_BLOCK, ncp), 0)
        valid_c = jnp.logical_and((n_i * CMP_STRIDE + (CMP_BLOCK - 1)) <= (t0 + r_i), n_i < n_cmp)
        row_any = ((t0 + r_col) >= (CMP_BLOCK - 1)).astype(F32)
        sc = jnp.where(valid_c[None], sc.reshape(HPG, Q_BLOCK, ncp), NEG_INF)
        _, e, l = _softmax_parts(sc)
        p_c = (e * (row_any[None] / l)).astype(BF16)
        o_c = jnp.dot(p_c.reshape(HPG * Q_BLOCK, ncp), vc_ref[0, 0], preferred_element_type=F32)
        imp = jnp.dot(jnp.concatenate([p_c[h] for h in range(HPG)], axis=1), ov_ref[...],
                      preferred_element_type=F32)

        imp_t = imp.T[:n_sel]
        j_i = lax.broadcasted_iota(jnp.int32, (n_sel, Q_BLOCK), 0)
        cur = lax.shift_right_logical(t0 + lax.broadcasted_iota(jnp.int32, (n_sel, Q_BLOCK), 1),
                                      int(math.log2(SEL_BLOCK)))
        forced = jnp.logical_or(j_i == 0, jnp.logical_or(j_i == cur, j_i == cur - 1))
        valid = j_i <= cur
        v = jnp.where(forced, SEL_FORCE, jnp.where(valid, imp_t, -SEL_FORCE))
        v_scr[...] = v
        n_chunk = n_sel // SUBLANES
        chunks = [v[k * SUBLANES:(k + 1) * SUBLANES] for k in range(n_chunk)]
        ranks = [jnp.zeros((SUBLANES, Q_BLOCK), jnp.int32) for _ in range(n_chunk)]
        sub_i = lax.broadcasted_iota(jnp.int32, (SUBLANES, Q_BLOCK), 0)
        for i in range(n_sel):
            vi = v_scr[i:i + 1, :]
            ki = i // SUBLANES
            for k in range(n_chunk):
                if k > ki:
                    beats = jnp.where(vi >= chunks[k], 1, 0)
                elif k < ki:
                    beats = jnp.where(vi > chunks[k], 1, 0)
                else:
                    beats = jnp.where(sub_i > (i - ki * SUBLANES),
                                      jnp.where(vi >= chunks[k], 1, 0),
                                      jnp.where(vi > chunks[k], 1, 0))
                ranks[k] = ranks[k] + beats
        rank = jnp.concatenate(ranks, axis=0)
        chosen = jnp.logical_and(jnp.logical_and(rank < SEL_TOPK, valid), j_i < 2 * qb)
        bias_t = jnp.where(chosen, 0.0, NEG_INF)
        bias_t = jnp.concatenate([bias_t, jnp.full((LANES - n_sel, Q_BLOCK), NEG_INF, F32)], axis=0)
        bias_f = bias_t.T
        block_used.append(jnp.max(bias_f, axis=0, keepdims=True))
        bias = bias_f.astype(BF16)
        lhs_scr[g] = jnp.concatenate([lhs_pos, jnp.concatenate([bias] * HPG, axis=0)], axis=1)

        vd = vs_ref[pl.ds(t0a, Q_BLOCK), :]
        sd = (sd.reshape(HPG, Q_BLOCK, Q_BLOCK) + tri[None]).reshape(HPG * Q_BLOCK, Q_BLOCK)
        m = jnp.max(sd, axis=-1, keepdims=True)
        m_scr[g] = jnp.broadcast_to(m, m_scr.shape[1:])
        acc_scr[g] = jnp.dot(jnp.exp2(sd - m).astype(BF16), vd, preferred_element_type=F32)

        v_w = vw_ref[pl.ds(w0, WIN_KEYS), :]
        sw = (sw.reshape(HPG, Q_BLOCK, WIN_KEYS) + wmask[None]).reshape(HPG * Q_BLOCK, WIN_KEYS)
        e = jnp.exp2(sw - jnp.max(sw, axis=-1, keepdims=True))
        o_w = jnp.dot(e.astype(BF16), v_w, preferred_element_type=F32)
        o_w = o_w * (1.0 / o_w[:, f0 + ONE_LANE:f0 + ONE_LANE + 1])

        for h in range(HPG):
            c = (g * HPG + h) * 3
            part_scr[g, hrows[h]] = gates[:, c:c + 1] * o_c[hrows[h]] + gates[:, c + 2:c + 3] * o_w[hrows[h]]

    n_key_tiles = hot_ref.shape[0] // Q_BLOCK - 1
    per_step = SEL_TK // Q_BLOCK
    used = jnp.maximum(block_used[0], block_used[1])
    used = jnp.maximum(used, pltpu.roll(used, LANES - 1, axis=1))
    count = jnp.int32(0)
    for j in range(n_key_tiles):
        tile_smem[count] = j
        count = count + (used[0, 2 * j] > -1.0).astype(jnp.int32)
    for k in range(per_step - 1):
        tile_smem[count + k] = n_key_tiles

    def tile_offsets(i, k):
        j = tile_smem[i * per_step + k]
        hot_off = pl.multiple_of(j * Q_BLOCK, Q_BLOCK)
        kv_off = pl.multiple_of(jnp.minimum(j, n_key_tiles - 1) * Q_BLOCK, Q_BLOCK)
        return kv_off, hot_off

    def sel_body(i, carry):
        offs = [tile_offsets(i, k) for k in range(per_step)]
        hot = jnp.concatenate([hot_ref[pl.ds(ho, Q_BLOCK), :] for _, ho in offs], axis=0)
        s_g = []
        for g in range(KV_GROUPS):
            ks_ref = (ks0_ref, ks1_ref)[g]
            k_t = jnp.concatenate([ks_ref[pl.ds(ko, Q_BLOCK), :] for ko, _ in offs], axis=0)
            s_g.append(_nt_dot(lhs_scr[g], jnp.concatenate([k_t, hot], axis=1)))
        for g in range(KV_GROUPS):
            vs_ref = (vs0_ref, vs1_ref)[g]
            v_t = jnp.concatenate([vs_ref[pl.ds(ko, Q_BLOCK), :] for ko, _ in offs], axis=0)
            s = s_g[g]
            m_old = m_scr[g]
            m_new = jnp.maximum(m_old, jnp.max(s, axis=-1, keepdims=True))
            p = jnp.exp2(s - jnp.tile(m_new, (1, SEL_TK // LANES)))
            alpha = jnp.exp2(m_old - m_new)
            m_scr[g] = m_new
            pv = jnp.dot(p.astype(BF16), v_t, preferred_element_type=F32)
            acc_scr[g] = alpha * acc_scr[g] + pv
        return carry

    lax.fori_loop(0, (count + per_step - 1) // per_step, sel_body, 0)

    for g in range(KV_GROUPS):
        one = _group_lane0(1 - g) + ONE_LANE
        acc = acc_scr[g]
        o_s = acc * (1.0 / acc[:, one:one + 1])
        keep = (lane < HEAD_DIM) if g == 0 else (lane >= HEAD_DIM)
        for h in range(HPG):
            c = (g * HPG + h) * 3
            out = part_scr[g, hrows[h]] + gates[:, c + 1:c + 2] * o_s[hrows[h]]
            out = jnp.where(keep, out, 0.0)
            slot = (g * HPG + h) * LANES
            o_ref[:, slot:slot + LANES] = out.astype(o_ref.dtype)


def _attention_tables(seq, ncp, n_cmp, n_sel):
    cstart = np.arange(ncp) * CMP_STRIDE
    sstart = np.arange(n_sel) * SEL_BLOCK
    overlap = ((cstart[:, None] < sstart[None, :] + SEL_BLOCK)
               & (cstart[:, None] + CMP_BLOCK > sstart[None, :])
               & (np.arange(ncp)[:, None] < n_cmp)).astype(np.float32)
    overlap = np.pad(overlap, ((0, 0), (0, LANES - n_sel)))
    ov4 = np.tile(overlap, (HPG, 1))
    n = np.arange(ncp)
    cfeat = np.zeros((KV_GROUPS, ncp, LANES), np.float32)
    for g in range(KV_GROUPS):
        l0 = _group_lane0(1 - g)
        for k in range(3):
            cfeat[g, :, l0 + F_CHI + k] = n // 16
            cfeat[g, :, l0 + F_CLO + k] = n % 16
            cfeat[g, :, l0 + F_COFF + k] = CMP_BLOCK - 1
    blk = np.concatenate([np.arange(seq) // SEL_BLOCK, np.full(Q_BLOCK, n_sel)])
    hot = (blk[:, None] == np.arange(LANES)[None, :]).astype(np.float32)
    r = np.arange(Q_BLOCK)[:, None]
    tri = np.where(np.arange(Q_BLOCK)[None, :] <= r, 0.0, NEG_INF).astype(np.float32)
    c = np.arange(WIN_KEYS)[None, :]
    wm = []
    for qb in range(WIN_BACK + 1):
        dist = (qb * Q_BLOCK + r) - c if qb < WIN_BACK else (WINDOW + r) - c
        wm.append(np.where((dist >= 0) & (dist < WINDOW), 0.0, NEG_INF))
    wm = np.stack(wm).astype(np.float32)
    return (jnp.asarray(ov4, BF16), jnp.asarray(cfeat, BF16), jnp.asarray(hot, BF16),
            jnp.asarray(tri), jnp.asarray(wm))


def _attention(zq, kvc, zkv, zg, batch, seq):
    n = zq.shape[0]
    nqb = seq // Q_BLOCK
    ncp = kvc.shape[2]
    n_cmp = seq // CMP_STRIDE - 1
    n_sel = seq // SEL_BLOCK
    assert n_sel <= HEAD_DIM and n_sel % SUBLANES == 0 and seq % SEL_TK == 0 and nqb > WIN_BACK
    assert ncp // 16 <= 256 and seq // SEL_BLOCK <= 256
    ov4, cfeat, hot, tri, wm = _attention_tables(seq, ncp, n_cmp, n_sel)

    kernel = functools.partial(_attn_kernel, n_cmp=n_cmp, n_sel=n_sel)
    kv_spec = lambda idx: pl.BlockSpec((seq, KV_SLAB), lambda b, i, idx=idx: (b, idx))
    return pl.pallas_call(
        kernel,
        grid=(batch, nqb),
        in_specs=[pl.BlockSpec((Q_BLOCK, Q_COLS), lambda b, i: (b * nqb + i, 0)),
                  pl.BlockSpec((1, 1, ncp, KV_SLAB), lambda b, i: (0, b, 0, 0)),
                  pl.BlockSpec((1, 1, ncp, KV_SLAB), lambda b, i: (1, b, 0, 0)),
                  *[kv_spec(k) for k in range(8)],
                  pl.BlockSpec((Q_BLOCK, GATE_COLS), lambda b, i: (b * nqb + i, 0)),
                  _const_spec(ov4.shape), _const_spec(cfeat.shape), _const_spec(hot.shape),
                  _const_spec(tri.shape), _const_spec(wm.shape)],
        out_specs=pl.BlockSpec((Q_BLOCK, Q_COLS), lambda b, i: (b * nqb + i, 0)),
        out_shape=jax.ShapeDtypeStruct((n, Q_COLS), BF16),
        scratch_shapes=[pltpu.VMEM((n_sel, Q_BLOCK), F32),
                        pltpu.VMEM((KV_GROUPS, HPG * Q_BLOCK, 2 * LANES), BF16),
                        pltpu.VMEM((KV_GROUPS, HPG * Q_BLOCK, LANES), F32),
                        pltpu.VMEM((KV_GROUPS, HPG * Q_BLOCK, LANES), F32),
                        pltpu.VMEM((KV_GROUPS, HPG * Q_BLOCK, LANES), F32),
                        pltpu.SMEM((seq // Q_BLOCK + SEL_TK // Q_BLOCK,), jnp.int32)],
        compiler_params=_cparams(("arbitrary", "arbitrary")),
        name="nsa_attn",
    )(zq, kvc, kvc, *([zkv] * 8), zg, ov4, cfeat, hot, tri, wm)


def _s5_kernel(u_ref, lagk_ref, mre_ref, mim_ref, cre_ref, cim_ref, lre_ref, lim_ref, d_ref, y_ref,
               ere, eim, xre, xim, tp_scr, *, n_chunks):
    def tok(t):
        return pl.ds(t, n_chunks, stride=S5_CHUNK)

    @pl.when(pl.program_id(1) == 0)
    def _():
        tp_scr[...] = jnp.zeros(tp_scr.shape, tp_scr.dtype)
        for s in range(S5_CHUNK):
            for t in range(s, S5_CHUNK):
                tp_scr[s * LANES:(s + 1) * LANES, t * LANES:(t + 1) * LANES] = lagk_ref[0, t - s]

    u = jnp.concatenate([u_ref[tok(t), :] for t in range(S5_CHUNK)], axis=1).astype(BF16)
    y = jnp.dot(u, tp_scr[...], preferred_element_type=F32)
    ere[...] = jnp.dot(u, mre_ref[0], preferred_element_type=F32)
    eim[...] = jnp.dot(u, mim_ref[0], preferred_element_type=F32)
    lr = lre_ref[0]
    li = lim_ref[0]

    def body(c, carry):
        xr, xi = carry
        row = pl.ds(c, 1)
        xre[row, :] = xr
        xim[row, :] = xi
        er = ere[row, :]
        ei = eim[row, :]
        return lr * xr - li * xi + er, lr * xi + li * xr + ei

    zero = jnp.zeros((1, ere.shape[1]), F32)
    lax.fori_loop(0, n_chunks, body, (zero, zero), unroll=8)
    y = y + jnp.dot(xre[...].astype(BF16), cre_ref[0], preferred_element_type=F32)
    y = y + jnp.dot(xim[...].astype(BF16), cim_ref[0], preferred_element_type=F32)
    d = d_ref[0]
    for t in range(S5_CHUNK):
        y_ref[tok(t), :] = y[:, t * LANES:(t + 1) * LANES] + d * u_ref[tok(t), :]


def _s5_matrices(a_re, a_im, b_re, b_im, c_re, c_im, d_skip, log_dt):
    t = S5_CHUNK
    hp = lax.Precision.HIGHEST
    dt = jnp.exp(log_dt)[:, None]
    lam_re = jnp.minimum(a_re, -1e-4)
    lam_im = a_im
    mag = jnp.exp(lam_re * dt)
    ang = lam_im * dt
    lb_re = mag * jnp.cos(ang)
    lb_im = mag * jnp.sin(ang)
    den = lam_re * lam_re + lam_im * lam_im
    nr = lb_re - 1.0
    coef_re = (nr * lam_re + lb_im * lam_im) / den
    coef_im = (lb_im * lam_re - nr * lam_im) / den
    bb_re = coef_re[..., None] * b_re - coef_im[..., None] * b_im
    bb_im = coef_re[..., None] * b_im + coef_im[..., None] * b_re
    j = jnp.arange(t + 1, dtype=F32)[:, None, None]
    pmag = jnp.exp(j * (lam_re * dt)[None])
    pw_re = pmag * jnp.cos(j * ang[None])
    pw_im = pmag * jnp.sin(j * ang[None])
    cl_re = c_re[None] * pw_re[:, :, None, :] - c_im[None] * pw_im[:, :, None, :]
    cl_im = c_re[None] * pw_im[:, :, None, :] + c_im[None] * pw_re[:, :, None, :]
    kern = (jnp.einsum('jgop,gpi->gjoi', cl_re[:t], bb_re, precision=hp)
            - jnp.einsum('jgop,gpi->gjoi', cl_im[:t], bb_im, precision=hp))
    eye = jnp.eye(S5_LB, dtype=F32)
    nlb = S5_GROUPS // S5_LB
    lagk = jnp.einsum('qgjoi,gh->qjgiho', kern.reshape(nlb, S5_LB, t, S5_GROUP, S5_GROUP), eye,
                      precision=hp).reshape(nlb, t, LANES, LANES)
    rv_re = pw_re[t - 1 - np.arange(t)]
    rv_im = pw_im[t - 1 - np.arange(t)]
    m_re = rv_re[..., None] * bb_re[None] - rv_im[..., None] * bb_im[None]
    m_im = rv_re[..., None] * bb_im[None] + rv_im[..., None] * bb_re[None]

    def lay_m(m):
        m = m.reshape(t, nlb, S5_LB, S5_STATE, S5_GROUP)
        return jnp.einsum('sqgpi,gh->qsgihp', m, eye, precision=hp).reshape(
            nlb, t * LANES, S5_LB * S5_STATE)

    def lay_c(c):
        c = c.reshape(t, nlb, S5_LB, S5_GROUP, S5_STATE)
        return jnp.einsum('tqgop,gh->qgptho', c, eye, precision=hp).reshape(
            nlb, S5_LB * S5_STATE, t * LANES)

    lam_t_re = pw_re[t].reshape(nlb, 1, S5_LB * S5_STATE)
    lam_t_im = pw_im[t].reshape(nlb, 1, S5_LB * S5_STATE)
    d_row = d_skip.reshape(nlb, 1, LANES)
    return (lagk.astype(BF16), lay_m(m_re).astype(BF16), lay_m(m_im).astype(BF16),
            lay_c(cl_re[1:]).astype(BF16), lay_c(-cl_im[1:]).astype(BF16), lam_t_re, lam_t_im, d_row)


def _s5(zs, mats, batch, seq):
    n_chunks = seq // S5_CHUNK
    lagk, m_re, m_im, cp_re, cp_im, l_re, l_im, d_row = mats
    nlb = lagk.shape[0]
    wcol = S5_CHUNK * LANES
    wst = S5_LB * S5_STATE
    kernel = functools.partial(_s5_kernel, n_chunks=n_chunks)
    p3 = lambda r, c: pl.BlockSpec((1, r, c), lambda q, b: (q, 0, 0))
    return pl.pallas_call(
        kernel,
        grid=(nlb, batch),
        in_specs=[pl.BlockSpec((seq, LANES), lambda q, b: (b, q)),
                  pl.BlockSpec((1, S5_CHUNK, LANES, LANES), lambda q, b: (q, 0, 0, 0)),
                  p3(wcol, wst), p3(wcol, wst), p3(wst, wcol), p3(wst, wcol),
                  p3(1, wst), p3(1, wst), p3(1, LANES)],
        out_specs=pl.BlockSpec((seq, LANES), lambda q, b: (b, q)),
        out_shape=jax.ShapeDtypeStruct(zs.shape, F32),
        scratch_shapes=[pltpu.VMEM((n_chunks, wst), F32) for _ in range(4)]
        + [pltpu.VMEM((wcol, wcol), BF16)],
        compiler_params=_cparams(("arbitrary", "arbitrary")),
        name="s5",
    )(zs, lagk, m_re, m_im, cp_re, cp_im, l_re, l_im, d_row)


def _merge_kernel(oa_ref, ys_ref, zm_ref, x_ref, mod_ref, wn_ref, wg_ref, wo_ref, g_ref, b_ref, o_ref,
                  *, alpha):
    d = x_ref.shape[-1]
    y_a = jnp.dot(oa_ref[...], wn_ref[...], preferred_element_type=F32)
    gl = jax.nn.gelu(ys_ref[...], approximate=True)
    zz = jnp.dot(gl.astype(BF16), wg_ref[...], preferred_element_type=F32)
    y_b = zz[:, :d] * jax.nn.sigmoid(zz[:, d:])
    zm = zm_ref[...]
    mix_in = jax.nn.sigmoid(zm[:, :d]) * y_a + jax.nn.sigmoid(zm[:, d:]) * y_b
    mix = jnp.dot(mix_in.astype(BF16), wo_ref[...], preferred_element_type=F32)
    gate = mod_ref[0, 2:3, :]
    r = alpha * x_ref[...] + gate * mix
    o_ref[...] = _layer_norm(r) * g_ref[...] + b_ref[...]


def _merge(oa, ys, zm, x2, mod3, wn_pad, wg, wo, ln_g, ln_b, seq, alpha):
    n, d = x2.shape
    tiles_per_batch = seq // TM_MERGE
    row = lambda w: pl.BlockSpec((TM_MERGE, w), lambda i: (i, 0))
    return pl.pallas_call(
        functools.partial(_merge_kernel, alpha=alpha),
        grid=(n // TM_MERGE,),
        in_specs=[row(oa.shape[1]), row(ys.shape[1]), row(zm.shape[1]), row(d),
                  pl.BlockSpec((1, 6, d), lambda i: (i // tiles_per_batch, 0, 0)),
                  _const_spec(wn_pad.shape), _const_spec(wg.shape), _const_spec(wo.shape),
                  _const_spec((1, d)), _const_spec((1, d))],
        out_specs=row(d),
        out_shape=jax.ShapeDtypeStruct((n, d), F32),
        compiler_params=_cparams(("arbitrary",)),
        name="merge",
    )(oa, ys, zm, x2, mod3, wn_pad, wg, wo, ln_g.reshape(1, d), ln_b.reshape(1, d))


def _pad_nsa_out(w):
    d = w.shape[1]
    w = w.reshape(NSA_HEADS, HEAD_DIM, d)
    z = jnp.zeros_like(w)
    is_g0 = (np.arange(NSA_HEADS) < HPG)[:, None, None]
    out = jnp.concatenate([jnp.where(is_g0, w, z), jnp.where(is_g0, z, w)], axis=1)
    return out.reshape(NSA_HEADS * LANES, d).astype(BF16)


def _ffn_kernel(x_ref, mod_ref, wup_ref, cw_ref, cb_ref, wdn_ref, g_ref, b_ref, o_ref, tail_ref, act_scr,
                *, alpha, tiles_per_batch):
    i = pl.program_id(0)
    tm = x_ref.shape[0]

    @pl.when(i % tiles_per_batch == 0)
    def _():
        tail_ref[...] = jnp.zeros(tail_ref.shape, F32)

    x = x_ref[...]
    shift = mod_ref[0, 3:4, :]
    scale = mod_ref[0, 4:5, :]
    gate = mod_ref[0, 5:6, :]
    h2 = (_layer_norm(x) * (1.0 + scale) + shift).astype(BF16)
    row = lax.broadcasted_iota(jnp.int32, (SUBLANES, FF_CHUNK), 0)

    def shift_rows(a, prev_rows):
        rolled = pltpu.roll(a, 1, axis=0)
        head = jnp.where(row == 0, prev_rows[SUBLANES - 1:SUBLANES], rolled[:SUBLANES])
        return jnp.concatenate([head, rolled[SUBLANES:]], axis=0)

    def conv_cols(c0):
        cols = slice(c0, c0 + FF_CHUNK)
        up = jnp.dot(h2, wup_ref[:, cols], preferred_element_type=F32)
        tail = tail_ref[:, cols]
        tail_ref[:, cols] = up[tm - SUBLANES:tm]
        w = cw_ref[:, cols]
        inner = shift_rows(w[0:1] * up, w[0:1] * tail) + w[1:2] * up
        prev_inner = pltpu.roll(w[0:1] * tail, 1, axis=0) + w[1:2] * tail
        return shift_rows(inner, prev_inner) + w[2:3] * up + cb_ref[:, cols]

    for k in range(D_FF // FF_CHUNK):
        val = conv_cols(k * FF_CHUNK)
        gte = conv_cols(D_FF + k * FF_CHUNK)
        act_scr[:, k * FF_CHUNK:(k + 1) * FF_CHUNK] = (gte * jax.nn.sigmoid(gte) * val).astype(BF16)
    ff = jnp.dot(act_scr[...], wdn_ref[...], preferred_element_type=F32)
    r = alpha * x + gate * ff
    o_ref[...] = _layer_norm(r) * g_ref[...] + b_ref[...]


def _ffn(x1, mod3, wup, conv_w, conv_b, wdn, ln_g, ln_b, seq, alpha):
    n, d = x1.shape
    tiles_per_batch = seq // TM_FFN
    ff2 = wup.shape[1]
    return pl.pallas_call(
        functools.partial(_ffn_kernel, alpha=alpha, tiles_per_batch=tiles_per_batch),
        grid=(n // TM_FFN,),
        in_specs=[pl.BlockSpec((TM_FFN, d), lambda i: (i, 0)),
                  pl.BlockSpec((1, 6, d), lambda i: (i // tiles_per_batch, 0, 0)),
                  _const_spec(wup.shape), _const_spec(conv_w.shape), _const_spec((1, ff2)),
                  _const_spec(wdn.shape), _const_spec((1, d)), _const_spec((1, d))],
        out_specs=pl.BlockSpec((TM_FFN, d), lambda i: (i, 0)),
        out_shape=jax.ShapeDtypeStruct((n, d), F32),
        scratch_shapes=[pltpu.VMEM((SUBLANES, ff2), F32), pltpu.VMEM((TM_FFN, ff2 // 2), BF16)],
        compiler_params=_cparams(("arbitrary",)),
        name="ffn",
    )(x1, mod3, wup, conv_w, conv_b.reshape(1, ff2), wdn, ln_g.reshape(1, d), ln_b.reshape(1, d))


def kernel(x, c, w_ada, b_ada, w_in, pe_ck, w_ck1, w_ck2, pe_cv, w_cv1, w_cv2, w_nsa_out,
           s5_a_re, s5_a_im, s5_b_re, s5_b_im, s5_c_re, s5_c_im, s5_d, s5_log_dt, w_s5_glu,
           w_o, ln1_g, ln1_b, w_up, conv_w, conv_b, w_down, ln2_g, ln2_b):
    batch, seq, d = x.shape
    depth = w_ada.shape[0]
    alpha = (2.0 * depth) ** 0.25
    n = batch * seq
    n_chunk16 = seq // CMP_STRIDE
    xf = x.reshape(n, d)
    for l in range(depth):
        mod3 = _ada(c, w_ada[l], b_ada[l]).reshape(batch, 6, d)

        zq, zc, zkv, zg, zs, zm = _inproj(xf, mod3, _build_w_all(w_in[l]), seq)

        wk1, pek, wk2 = _cmp_weights(pe_ck[l], w_ck1[l], w_ck2[l])
        wv1, pev, wv2 = _cmp_weights(pe_cv[l], w_cv1[l], w_cv2[l])
        ch = zc.reshape(2, batch, n_chunk16, CMP_STRIDE * KV_SLAB)
        kvc = _compress(ch, jnp.stack([wk1, wv1]), jnp.stack([pek, pev]), jnp.stack([wk2, wv2]))

        oa = _attention(zq, kvc, zkv, zg, batch, seq)

        mats = _s5_matrices(s5_a_re[l], s5_a_im[l], s5_b_re[l], s5_b_im[l], s5_c_re[l], s5_c_im[l],
                            s5_d[l], s5_log_dt[l])
        ys = _s5(zs, mats, batch, seq)

        x1 = _merge(oa, ys, zm, xf, mod3, _pad_nsa_out(w_nsa_out[l]), w_s5_glu[l].astype(BF16),
                    w_o[l].astype(BF16), ln1_g[l], ln1_b[l], seq, alpha)

        xf = _ffn(x1, mod3, w_up[l].astype(BF16), conv_w[l], conv_b[l], w_down[l].astype(BF16),
                  ln2_g[l], ln2_b[l], seq, alpha)
    return xf.reshape(batch, seq, d)
```

```python
import functools
import math

import jax
import jax.numpy as jnp
import ml_dtypes
import numpy as np
from jax import lax
from jax.experimental import pallas as pl
from jax.experimental.pallas import tpu as pltpu

F32 = jnp.float32
BF16 = jnp.bfloat16

D_MODEL = 1024
NSA_HEADS = 8
KV_GROUPS = 2
HPG = NSA_HEADS // KV_GROUPS
HEAD_DIM = 64
CMP_BLOCK = 32
CMP_STRIDE = 16
CMP_HIDDEN = 128
SEL_BLOCK = 64
SEL_TOPK = 16
WINDOW = 512
Q_BLOCK = 128
S5_GROUP = 16
S5_WIDTH = 512
S5_GROUPS = S5_WIDTH // S5_GROUP
S5_STATE = 64
D_FF = 2816
CONV_WIDTH = 3
LN_EPS = 1e-5
NEG_INF = -1e30
SEL_FORCE = 1e9

LANES = 128
SUBLANES = 8
VMEM_LIMIT = 56 * 1024 * 1024

TM_IN = 512
TM_MERGE = 512
TM_FFN = 512
FF_CHUNK = 256
SEL_TK = 512
WIN_KEYS = WINDOW + Q_BLOCK
WIN_BACK = WINDOW // Q_BLOCK
S5_CHUNK = 8
S5_LB = LANES // S5_GROUP
S5_NLB = S5_WIDTH // LANES

Q_COLS = NSA_HEADS * LANES
KV_SLAB = KV_GROUPS * HEAD_DIM
GATE_COLS = LANES
MERGE_COLS = 2 * D_MODEL
KV_OUT_COLS = 8 * KV_SLAB
F_HI, F_LO = 0, 3
ONE_LANE = 6
F_CHI, F_CLO, F_COFF = 7, 10, 13
LOG2E = 1.4426950408889634
MASK_BITS = 16


def _cparams(sem):
    return pltpu.CompilerParams(dimension_semantics=sem, vmem_limit_bytes=VMEM_LIMIT)


def _const_spec(shape):
    n = len(shape)
    return pl.BlockSpec(shape, lambda *_: (0,) * n)


def _layer_norm(x):
    mu = jnp.mean(x, axis=-1, keepdims=True)
    xc = x - mu
    var = jnp.mean(xc * xc, axis=-1, keepdims=True)
    return xc * lax.rsqrt(var + LN_EPS)


def _nt_dot(a, b):
    return lax.dot_general(a, b, (((1,), (1,)), ((), ())), preferred_element_type=F32)


def _group_lane0(g):
    return g * HEAD_DIM


def _ada_kernel(c_ref, w_ref, b_ref, o_ref):
    c = c_ref[...]
    a = c * jax.nn.sigmoid(c)
    o_ref[...] = jnp.dot(a, w_ref[...], preferred_element_type=F32,
                         precision=lax.Precision.HIGHEST) + b_ref[...]


def _ada(c, w_ada, b_ada):
    b, d = c.shape
    n = w_ada.shape[1]
    blk = 1024
    return pl.pallas_call(
        _ada_kernel,
        grid=(n // blk,),
        in_specs=[pl.BlockSpec((b, d), lambda j: (0, 0)),
                  pl.BlockSpec((d, blk), lambda j: (0, j)),
                  pl.BlockSpec((1, blk), lambda j: (0, j))],
        out_specs=pl.BlockSpec((b, blk), lambda j: (0, j)),
        out_shape=jax.ShapeDtypeStruct((b, n), F32),
        compiler_params=_cparams(("arbitrary",)),
        name="ada",
    )(c, w_ada, b_ada.reshape(1, n))


def _inproj_kernel(x_ref, mod_ref, w_ref, pf_ref, qf_ref, q_ref, c_ref, kv_ref, g_ref, s_ref, m_ref, zc_scr):
    hn = _layer_norm(x_ref[...])
    shift = mod_ref[0, 0:1, :]
    scale = mod_ref[0, 1:2, :]
    h = (hn * (1.0 + scale) + shift).astype(BF16)
    tm = x_ref.shape[0]

    def proj(col, width):
        return jnp.dot(h, w_ref[:, col:col + width], preferred_element_type=F32)

    col = 0
    q_ref[...] = (proj(col, Q_COLS) + qf_ref[...]).astype(q_ref.dtype)
    col += Q_COLS
    for kind in range(2):
        zc_scr[...] = proj(col, KV_SLAB)
        for tok in range(CMP_STRIDE):
            c_ref[kind, :, tok * KV_SLAB:(tok + 1) * KV_SLAB] = zc_scr[
                pl.ds(tok, tm // CMP_STRIDE, stride=CMP_STRIDE), :]
        col += KV_SLAB
    lane = lax.broadcasted_iota(jnp.int32, (x_ref.shape[0], LANES), 1)
    pf = pf_ref[...]
    for part in range(4):
        z = proj(col + part * KV_SLAB, KV_SLAB)
        for g in range(KV_GROUPS):
            own = (lane < HEAD_DIM) if g == 0 else (lane >= HEAD_DIM)
            slot = (part * KV_GROUPS + g) * KV_SLAB
            kv_ref[:, slot:slot + KV_SLAB] = jnp.where(
                own, z, pf[:, g * LANES:(g + 1) * LANES]).astype(kv_ref.dtype)
    col += 4 * KV_SLAB
    g_ref[...] = proj(col, GATE_COLS)
    col += GATE_COLS
    s_ref[...] = proj(col, S5_WIDTH)
    col += S5_WIDTH
    m_ref[...] = proj(col, MERGE_COLS)


def _log2e_terms():
    terms, rest = [], np.float64(LOG2E)
    for _ in range(3):
        t = np.float64(np.float32(rest).astype(ml_dtypes.bfloat16))
        terms.append(float(t))
        rest -= t
    return terms


def _key_position_features(seq):
    p = np.arange(seq)
    out = np.zeros((seq, KV_GROUPS, LANES), np.float32)
    for g in range(KV_GROUPS):
        l0 = _group_lane0(1 - g)
        for k in range(3):
            out[:, g, l0 + F_HI + k] = p // SEL_BLOCK
            out[:, g, l0 + F_LO + k] = p % SEL_BLOCK
        out[:, g, l0 + ONE_LANE] = 1.0
    return jnp.asarray(out.reshape(seq, KV_GROUPS * LANES))


def _query_feature_row():
    c = _log2e_terms()
    row = np.zeros((NSA_HEADS, LANES), np.float32)
    for hd in range(NSA_HEADS):
        g = hd // HPG
        slope = 2.0 ** -(hd + 1)
        l0 = _group_lane0(1 - g)
        for k in range(3):
            row[hd, l0 + F_HI + k] = SEL_BLOCK * slope * c[k]
            row[hd, l0 + F_LO + k] = slope * c[k]
            row[hd, l0 + F_CHI + k] = 16 * CMP_STRIDE * slope * c[k]
            row[hd, l0 + F_CLO + k] = CMP_STRIDE * slope * c[k]
            row[hd, l0 + F_COFF + k] = slope * c[k]
    return jnp.asarray(row.reshape(1, Q_COLS))


def _inproj(x2, mod3, w_all, seq):
    n, d = x2.shape
    tiles_per_batch = seq // TM_IN
    rows16 = TM_IN // CMP_STRIDE
    widths = (Q_COLS, KV_OUT_COLS, GATE_COLS, S5_WIDTH, MERGE_COLS)
    dtypes = (BF16, BF16, F32, F32, F32)
    pf = _key_position_features(seq)
    qf = _query_feature_row()
    row_spec = lambda w: pl.BlockSpec((TM_IN, w), lambda i: (i, 0))
    out_specs = [row_spec(Q_COLS),
                 pl.BlockSpec((2, rows16, CMP_STRIDE * KV_SLAB), lambda i: (0, i, 0))]
    out_specs += [row_spec(w) for w in widths[1:]]
    out_shape = [jax.ShapeDtypeStruct((n, Q_COLS), BF16),
                 jax.ShapeDtypeStruct((2, n // CMP_STRIDE, CMP_STRIDE * KV_SLAB), F32)]
    out_shape += [jax.ShapeDtypeStruct((n, w), dt) for w, dt in zip(widths[1:], dtypes[1:])]
    return pl.pallas_call(
        _inproj_kernel,
        grid=(n // TM_IN,),
        in_specs=[pl.BlockSpec((TM_IN, d), lambda i: (i, 0)),
                  pl.BlockSpec((1, 6, d), lambda i: (i // tiles_per_batch, 0, 0)),
                  _const_spec(w_all.shape),
                  pl.BlockSpec((TM_IN, KV_GROUPS * LANES), lambda i: (i % tiles_per_batch, 0)),
                  _const_spec(qf.shape)],
        out_specs=out_specs,
        out_shape=out_shape,
        scratch_shapes=[pltpu.VMEM((TM_IN, KV_SLAB), F32)],
        compiler_params=_cparams(("arbitrary",)),
        name="inproj",
    )(x2, mod3, w_all, pf, qf)


def _build_w_all(w_in):
    d = w_in.shape[0]
    cq = NSA_HEADS * HEAD_DIM
    ckv = 6 * KV_SLAB
    cg = 3 * NSA_HEADS
    zeros = jnp.zeros((d, HEAD_DIM), w_in.dtype)
    pieces = []
    for hd in range(NSA_HEADS):
        wq = w_in[:, hd * HEAD_DIM:(hd + 1) * HEAD_DIM] * (HEAD_DIM ** -0.5 * LOG2E)
        pieces += [wq, zeros] if hd < HPG else [zeros, wq]
    wq_pad = jnp.concatenate(pieces, axis=1)
    wkv = w_in[:, cq:cq + ckv]
    wg = jnp.pad(w_in[:, cq + ckv:cq + ckv + cg], ((0, 0), (0, GATE_COLS - cg)))
    rest = w_in[:, cq + ckv + cg:]
    return jnp.concatenate([wq_pad, wkv, wg, rest], axis=1).astype(BF16)


def _cmp_kernel(ch_ref, w1_ref, pe_ref, w2_ref, o_ref):
    ch = ch_ref[0, 0]
    a = jnp.dot((ch + pe_ref[0, 0]).astype(BF16), w1_ref[0, 0], preferred_element_type=F32)
    b = jnp.dot((ch + pe_ref[0, 1]).astype(BF16), w1_ref[0, 1], preferred_element_type=F32)
    n = a.shape[0]
    hsum = a + pltpu.roll(b, n - 1, axis=0)
    hact = hsum * jax.nn.sigmoid(hsum)
    o_ref[0, 0] = jnp.dot(hact.astype(BF16), w2_ref[0], preferred_element_type=F32).astype(o_ref.dtype)


def _cmp_weights(pe, w1, w2):
    half = CMP_BLOCK // 2
    halves = []
    pes = []
    for lo in (0, half):
        w = w1[lo:lo + half]
        z = jnp.zeros_like(w)
        w_g0 = jnp.concatenate([w, z], axis=-1)
        w_g1 = jnp.concatenate([z, w], axis=-1)
        halves.append(jnp.stack([w_g0, w_g1], axis=1).reshape(half * KV_SLAB, KV_GROUPS * CMP_HIDDEN))
        p = pe[lo:lo + half]
        pes.append(jnp.broadcast_to(p[:, None, :], (half, KV_GROUPS, HEAD_DIM)).reshape(1, half * KV_SLAB))
    z2 = jnp.zeros_like(w2)
    w2_blk = jnp.concatenate([jnp.concatenate([w2, z2], axis=1), jnp.concatenate([z2, w2], axis=1)], axis=0)
    return jnp.stack(halves).astype(BF16), jnp.stack(pes), w2_blk.astype(BF16)


def _compress(ch, w1s, pes, w2s):
    kinds, b, nchunk, width = ch.shape
    hid = w1s.shape[-1]
    return pl.pallas_call(
        _cmp_kernel,
        grid=(kinds, b),
        in_specs=[pl.BlockSpec((1, 1, nchunk, width), lambda k, i: (k, i, 0, 0)),
                  pl.BlockSpec((1, 2, width, hid), lambda k, i: (k, 0, 0, 0)),
                  pl.BlockSpec((1, 2, 1, width), lambda k, i: (k, 0, 0, 0)),
                  pl.BlockSpec((1, hid, KV_SLAB), lambda k, i: (k, 0, 0))],
        out_specs=pl.BlockSpec((1, 1, nchunk, KV_SLAB), lambda k, i: (k, i, 0, 0)),
        out_shape=jax.ShapeDtypeStruct((kinds, b, nchunk, KV_SLAB), BF16),
        compiler_params=_cparams(("arbitrary", "arbitrary")),
        name="compress",
    )(ch, w1s, pes, w2s)


def _softmax_parts(s):
    m = jnp.max(s, axis=-1, keepdims=True)
    e = jnp.exp2(s - m)
    return m, e, jnp.sum(e, axis=-1, keepdims=True)


def _attn_kernel(q_ref, kc_ref, vc_ref, ks0_ref, ks1_ref, vs0_ref, vs1_ref, kw0_ref, kw1_ref,
                 vw0_ref, vw1_ref, zg_ref,
                 ov_ref, cf_ref, hot_ref, tri_ref, wm_ref, gsel_ref, o_ref, v_scr, lhs_scr, m_scr, acc_scr,
                 part_scr, tile_smem,
                 *, n_cmp, n_sel):
    qb = pl.program_id(1)
    t0 = qb * Q_BLOCK
    ncp = kc_ref.shape[2]
    gates = jax.nn.sigmoid(zg_ref[...])
    g_hi = gates.astype(BF16)
    g_lo = (gates - g_hi.astype(F32)).astype(BF16)
    gates_b = jnp.dot(jnp.concatenate([g_hi, g_lo], axis=1), gsel_ref[...], preferred_element_type=F32)

    def gate(g, h, branch):
        c = (g * HPG + h) * 3 + branch
        return gates_b[:, c * LANES:(c + 1) * LANES]
    r_col = lax.broadcasted_iota(jnp.int32, (Q_BLOCK, 1), 0)
    lane = lax.broadcasted_iota(jnp.int32, (Q_BLOCK, LANES), 1)
    tri = tri_ref[...]
    wmask = wm_ref[jnp.minimum(qb, WIN_BACK)]
    hrows = [slice(h * Q_BLOCK, (h + 1) * Q_BLOCK) for h in range(HPG)]

    t0a = pl.multiple_of(t0, Q_BLOCK)
    w0 = pl.multiple_of(jnp.maximum(qb - WIN_BACK, 0) * Q_BLOCK, Q_BLOCK)

    scores = []
    for g in range(KV_GROUPS):
        ks_ref = (ks0_ref, ks1_ref)[g]
        kw_ref = (kw0_ref, kw1_ref)[g]
        own_c = lax.broadcasted_iota(jnp.int32, (ncp, LANES), 1)
        own_c = (own_c < HEAD_DIM) if g == 0 else (own_c >= HEAD_DIM)
        lhs_pos = jnp.concatenate(
            [q_ref[:, (g * HPG + h) * LANES:(g * HPG + h + 1) * LANES] for h in range(HPG)], axis=0)
        kc_aug = jnp.where(own_c, kc_ref[0, 0], cf_ref[g])
        sc = _nt_dot(lhs_pos, kc_aug)
        sd = _nt_dot(lhs_pos, ks_ref[pl.ds(t0a, Q_BLOCK), :])
        sw = _nt_dot(lhs_pos, kw_ref[pl.ds(w0, WIN_KEYS), :])
        scores.append((lhs_pos, sc, sd, sw))

    block_used = []
    for g in range(KV_GROUPS):
        lhs_pos, sc, sd, sw = scores[g]
        vs_ref = (vs0_ref, vs1_ref)[g]
        vw_ref = (vw0_ref, vw1_ref)[g]
        f0 = _group_lane0(1 - g)

        n_i = lax.broadcasted_iota(jnp.int32, (Q_BLOCK, ncp), 1)
        r_i = lax.broadcasted_iota(jnp.int32, (Q_BLOCK, ncp), 0)
        valid_c = jnp.logical_and((n_i * CMP_STRIDE + (CMP_BLOCK - 1)) <= (t0 + r_i), n_i < n_cmp)
        row_any = ((t0 + r_col) >= (CMP_BLOCK - 1)).astype(F32)
        sc = jnp.where(valid_c[None], sc.reshape(HPG, Q_BLOCK, ncp), NEG_INF)
        _, e, l = _softmax_parts(sc)
        p_c = (e * (row_any[None] / l)).astype(BF16)
        o_c = jnp.dot(p_c.reshape(HPG * Q_BLOCK, ncp), vc_ref[0, 0], preferred_element_type=F32)
        imp = jnp.dot(jnp.concatenate([p_c[h] for h in range(HPG)], axis=1), ov_ref[...],
                      preferred_element_type=F32)

        imp_t = imp.T[:n_sel]
        j_i = lax.broadcasted_iota(jnp.int32, (n_sel, Q_BLOCK), 0)
        cur = lax.shift_right_logical(t0 + lax.broadcasted_iota(jnp.int32, (n_sel, Q_BLOCK), 1),
                                      int(math.log2(SEL_BLOCK)))
        forced = jnp.logical_or(j_i == 0, jnp.logical_or(j_i == cur, j_i == cur - 1))
        valid = j_i <= cur
        v = jnp.where(forced, SEL_FORCE, jnp.where(valid, imp_t, -SEL_FORCE))
        v_scr[...] = v
        n_chunk = n_sel // SUBLANES
        chunks = [v[k * SUBLANES:(k + 1) * SUBLANES] for k in range(n_chunk)]
        ranks = [jnp.zeros((SUBLANES, Q_BLOCK), jnp.int32) for _ in range(n_chunk)]
        sub_i = lax.broadcasted_iota(jnp.int32, (SUBLANES, Q_BLOCK), 0)
        for i in range(n_sel):
            vi = v_scr[i:i + 1, :]
            ki = i // SUBLANES
            for k in range(n_chunk):
                if k > ki:
                    beats = jnp.where(vi >= chunks[k], 1, 0)
                elif k < ki:
                    beats = jnp.where(vi > chunks[k], 1, 0)
                else:
                    beats = jnp.where(sub_i > (i - ki * SUBLANES),
                                      jnp.where(vi >= chunks[k], 1, 0),
                                      jnp.where(vi > chunks[k], 1, 0))
                ranks[k] = ranks[k] + beats
        rank = jnp.concatenate(ranks, axis=0)
        chosen = jnp.logical_and(jnp.logical_and(rank < SEL_TOPK, valid), j_i < 2 * qb)
        bias_t = jnp.where(chosen, 0.0, NEG_INF)
        bias_t = jnp.concatenate([bias_t, jnp.full((LANES - n_sel, Q_BLOCK), NEG_INF, F32)], axis=0)
        bias_f = bias_t.T
        block_used.append(jnp.max(bias_f, axis=0, keepdims=True))
        bias = bias_f.astype(BF16)
        lhs_scr[g] = jnp.concatenate([lhs_pos, jnp.concatenate([bias] * HPG, axis=0)], axis=1)

        vd = vs_ref[pl.ds(t0a, Q_BLOCK), :]
        sd = (sd.reshape(HPG, Q_BLOCK, Q_BLOCK) + tri[None]).reshape(HPG * Q_BLOCK, Q_BLOCK)
        m = jnp.max(sd, axis=-1, keepdims=True)
        m_scr[g] = jnp.broadcast_to(m, m_scr.shape[1:])
        acc_scr[g] = jnp.dot(jnp.exp2(sd - m).astype(BF16), vd, preferred_element_type=F32)

        v_w = vw_ref[pl.ds(w0, WIN_KEYS), :]
        sw = (sw.reshape(HPG, Q_BLOCK, WIN_KEYS) + wmask[None]).reshape(HPG * Q_BLOCK, WIN_KEYS)
        e = jnp.exp2(sw - jnp.max(sw, axis=-1, keepdims=True))
        o_w = jnp.dot(e.astype(BF16), v_w, preferred_element_type=F32)
        o_w = o_w * (1.0 / o_w[:, f0 + ONE_LANE:f0 + ONE_LANE + 1])

        for h in range(HPG):
            part_scr[g, hrows[h]] = gate(g, h, 0) * o_c[hrows[h]] + gate(g, h, 2) * o_w[hrows[h]]

    n_key_tiles = hot_ref.shape[0] // Q_BLOCK - 1
    per_step = SEL_TK // Q_BLOCK
    used = jnp.maximum(block_used[0], block_used[1])
    used = jnp.maximum(used, pltpu.roll(used, LANES - 1, axis=1))
    lane1 = lax.broadcasted_iota(jnp.int32, (1, LANES), 1)
    tile_of_lane = lax.shift_right_logical(lane1, 1)
    bit = jnp.where(jnp.logical_and(used > -1.0, (lane1 & 1) == 0),
                    jnp.left_shift(1, tile_of_lane & (MASK_BITS - 1)), 0).astype(F32)
    word = lax.shift_right_logical(tile_of_lane, int(math.log2(MASK_BITS)))
    masks = [jnp.sum(jnp.where(word == w, bit, 0.0), axis=1, keepdims=True)[0, 0].astype(jnp.int32)
             for w in range((n_key_tiles + MASK_BITS - 1) // MASK_BITS)]
    count = jnp.int32(0)
    for j in range(n_key_tiles):
        tile_smem[count] = j
        count = count + (lax.shift_right_logical(masks[j // MASK_BITS], j % MASK_BITS) & 1)
    for k in range(per_step - 1):
        tile_smem[count + k] = n_key_tiles

    def tile_offsets(i, k):
        j = tile_smem[i * per_step + k]
        hot_off = pl.multiple_of(j * Q_BLOCK, Q_BLOCK)
        kv_off = pl.multiple_of(jnp.minimum(j, n_key_tiles - 1) * Q_BLOCK, Q_BLOCK)
        return kv_off, hot_off

    def sel_body(i, carry):
        offs = [tile_offsets(i, k) for k in range(per_step)]
        hot = jnp.concatenate([hot_ref[pl.ds(ho, Q_BLOCK), :] for _, ho in offs], axis=0)
        s_g = []
        for g in range(KV_GROUPS):
            ks_ref = (ks0_ref, ks1_ref)[g]
            k_t = jnp.concatenate([ks_ref[pl.ds(ko, Q_BLOCK), :] for ko, _ in offs], axis=0)
            s_g.append(_nt_dot(lhs_scr[g], jnp.concatenate([k_t, hot], axis=1)))
        for g in range(KV_GROUPS):
            vs_ref = (vs0_ref, vs1_ref)[g]
            v_t = jnp.concatenate([vs_ref[pl.ds(ko, Q_BLOCK), :] for ko, _ in offs], axis=0)
            s = s_g[g]
            m_old = m_scr[g]
            m_new = jnp.maximum(m_old, jnp.max(s, axis=-1, keepdims=True))
            p = jnp.exp2(s - jnp.tile(m_new, (1, SEL_TK // LANES)))
            alpha = jnp.exp2(m_old - m_new)
            m_scr[g] = m_new
            pv = jnp.dot(p.astype(BF16), v_t, preferred_element_type=F32)
            acc_scr[g] = alpha * acc_scr[g] + pv
        return carry

    lax.fori_loop(0, (count + per_step - 1) // per_step, sel_body, 0)

    o_s = []
    for g in range(KV_GROUPS):
        one = _group_lane0(1 - g) + ONE_LANE
        acc = acc_scr[g]
        o_s.append(acc * (1.0 / acc[:, one:one + 1]))
    for h in range(HPG):
        outs = [part_scr[g, hrows[h]] + gate(g, h, 1) * o_s[g][hrows[h]] for g in range(KV_GROUPS)]
        o_ref[:, h * LANES:(h + 1) * LANES] = jnp.where(lane < HEAD_DIM, outs[0], outs[1]).astype(o_ref.dtype)


def _attention_tables(seq, ncp, n_cmp, n_sel):
    cstart = np.arange(ncp) * CMP_STRIDE
    sstart = np.arange(n_sel) * SEL_BLOCK
    overlap = ((cstart[:, None] < sstart[None, :] + SEL_BLOCK)
               & (cstart[:, None] + CMP_BLOCK > sstart[None, :])
               & (np.arange(ncp)[:, None] < n_cmp)).astype(np.float32)
    overlap = np.pad(overlap, ((0, 0), (0, LANES - n_sel)))
    ov4 = np.tile(overlap, (HPG, 1))
    n = np.arange(ncp)
    cfeat = np.zeros((KV_GROUPS, ncp, LANES), np.float32)
    for g in range(KV_GROUPS):
        l0 = _group_lane0(1 - g)
        for k in range(3):
            cfeat[g, :, l0 + F_CHI + k] = n // 16
            cfeat[g, :, l0 + F_CLO + k] = n % 16
            cfeat[g, :, l0 + F_COFF + k] = CMP_BLOCK - 1
    blk = np.concatenate([np.arange(seq) // SEL_BLOCK, np.full(Q_BLOCK, n_sel)])
    hot = (blk[:, None] == np.arange(LANES)[None, :]).astype(np.float32)
    r = np.arange(Q_BLOCK)[:, None]
    tri = np.where(np.arange(Q_BLOCK)[None, :] <= r, 0.0, NEG_INF).astype(np.float32)
    c = np.arange(WIN_KEYS)[None, :]
    wm = []
    for qb in range(WIN_BACK + 1):
        dist = (qb * Q_BLOCK + r) - c if qb < WIN_BACK else (WINDOW + r) - c
        wm.append(np.where((dist >= 0) & (dist < WINDOW), 0.0, NEG_INF))
    wm = np.stack(wm).astype(np.float32)
    n_gate = 3 * NSA_HEADS
    gsel = np.zeros((2 * LANES, n_gate, LANES), np.float32)
    for c in range(n_gate):
        gsel[c, c, :] = 1.0
        gsel[LANES + c, c, :] = 1.0
    gsel = jnp.asarray(gsel.reshape(2 * LANES, n_gate * LANES), BF16)
    return (jnp.asarray(ov4, BF16), jnp.asarray(cfeat, BF16), jnp.asarray(hot, BF16), gsel,
            jnp.asarray(tri), jnp.asarray(wm))


def _attention(zq, kvc, zkv, zg, batch, seq):
    n = zq.shape[0]
    nqb = seq // Q_BLOCK
    ncp = kvc.shape[2]
    n_cmp = seq // CMP_STRIDE - 1
    n_sel = seq // SEL_BLOCK
    assert n_sel <= HEAD_DIM and n_sel % SUBLANES == 0 and seq % SEL_TK == 0 and nqb > WIN_BACK
    assert ncp // 16 <= 256 and seq // SEL_BLOCK <= 256
    ov4, cfeat, hot, gsel, tri, wm = _attention_tables(seq, ncp, n_cmp, n_sel)
    out_cols = HPG * LANES

    kernel = functools.partial(_attn_kernel, n_cmp=n_cmp, n_sel=n_sel)
    kv_spec = lambda idx: pl.BlockSpec((seq, KV_SLAB), lambda b, i, idx=idx: (b, idx))
    return pl.pallas_call(
        kernel,
        grid=(batch, nqb),
        in_specs=[pl.BlockSpec((Q_BLOCK, Q_COLS), lambda b, i: (b * nqb + i, 0)),
                  pl.BlockSpec((1, 1, ncp, KV_SLAB), lambda b, i: (0, b, 0, 0)),
                  pl.BlockSpec((1, 1, ncp, KV_SLAB), lambda b, i: (1, b, 0, 0)),
                  *[kv_spec(k) for k in range(8)],
                  pl.BlockSpec((Q_BLOCK, GATE_COLS), lambda b, i: (b * nqb + i, 0)),
                  _const_spec(ov4.shape), _const_spec(cfeat.shape), _const_spec(hot.shape),
                  _const_spec(tri.shape), _const_spec(wm.shape), _const_spec(gsel.shape)],
        out_specs=pl.BlockSpec((Q_BLOCK, out_cols), lambda b, i: (b * nqb + i, 0)),
        out_shape=jax.ShapeDtypeStruct((n, out_cols), BF16),
        scratch_shapes=[pltpu.VMEM((n_sel, Q_BLOCK), F32),
                        pltpu.VMEM((KV_GROUPS, HPG * Q_BLOCK, 2 * LANES), BF16),
                        pltpu.VMEM((KV_GROUPS, HPG * Q_BLOCK, LANES), F32),
                        pltpu.VMEM((KV_GROUPS, HPG * Q_BLOCK, LANES), F32),
                        pltpu.VMEM((KV_GROUPS, HPG * Q_BLOCK, LANES), F32),
                        pltpu.SMEM((seq // Q_BLOCK + SEL_TK // Q_BLOCK,), jnp.int32)],
        compiler_params=_cparams(("arbitrary", "arbitrary")),
        name="nsa_attn",
    )(zq, kvc, kvc, *([zkv] * 8), zg, ov4, cfeat, hot, tri, wm, gsel)


def _s5_kernel(u_ref, lagk_ref, mre_ref, mim_ref, cre_ref, cim_ref, lre_ref, lim_ref, d_ref, y_ref,
               ere, eim, xre, xim, tp_scr, *, n_chunks):
    def tok(t):
        return pl.ds(t, n_chunks, stride=S5_CHUNK)

    @pl.when(pl.program_id(1) == 0)
    def _():
        tp_scr[...] = jnp.zeros(tp_scr.shape, tp_scr.dtype)
        for s in range(S5_CHUNK):
            for t in range(s, S5_CHUNK):
                tp_scr[s * LANES:(s + 1) * LANES, t * LANES:(t + 1) * LANES] = lagk_ref[0, t - s]

    u = jnp.concatenate([u_ref[tok(t), :] for t in range(S5_CHUNK)], axis=1).astype(BF16)
    y = jnp.dot(u, tp_scr[...], preferred_element_type=F32)
    ere[...] = jnp.dot(u, mre_ref[0], preferred_element_type=F32)
    eim[...] = jnp.dot(u, mim_ref[0], preferred_element_type=F32)
    lr = lre_ref[0]
    li = lim_ref[0]

    def body(c, carry):
        xr, xi = carry
        row = pl.ds(c, 1)
        xre[row, :] = xr
        xim[row, :] = xi
        er = ere[row, :]
        ei = eim[row, :]
        return lr * xr - li * xi + er, lr * xi + li * xr + ei

    zero = jnp.zeros((1, ere.shape[1]), F32)
    lax.fori_loop(0, n_chunks, body, (zero, zero), unroll=8)
    y = y + jnp.dot(xre[...].astype(BF16), cre_ref[0], preferred_element_type=F32)
    y = y + jnp.dot(xim[...].astype(BF16), cim_ref[0], preferred_element_type=F32)
    d = d_ref[0]
    for t in range(S5_CHUNK):
        y_ref[tok(t), :] = y[:, t * LANES:(t + 1) * LANES] + d * u_ref[tok(t), :]


def _s5_matrices(a_re, a_im, b_re, b_im, c_re, c_im, d_skip, log_dt):
    t = S5_CHUNK
    hp = lax.Precision.HIGHEST
    dt = jnp.exp(log_dt)[:, None]
    lam_re = jnp.minimum(a_re, -1e-4)
    lam_im = a_im
    mag = jnp.exp(lam_re * dt)
    ang = lam_im * dt
    lb_re = mag * jnp.cos(ang)
    lb_im = mag * jnp.sin(ang)
    den = lam_re * lam_re + lam_im * lam_im
    nr = lb_re - 1.0
    coef_re = (nr * lam_re + lb_im * lam_im) / den
    coef_im = (lb_im * lam_re - nr * lam_im) / den
    bb_re = coef_re[..., None] * b_re - coef_im[..., None] * b_im
    bb_im = coef_re[..., None] * b_im + coef_im[..., None] * b_re
    j = jnp.arange(t + 1, dtype=F32)[:, None, None]
    pmag = jnp.exp(j * (lam_re * dt)[None])
    pw_re = pmag * jnp.cos(j * ang[None])
    pw_im = pmag * jnp.sin(j * ang[None])
    cl_re = c_re[None] * pw_re[:, :, None, :] - c_im[None] * pw_im[:, :, None, :]
    cl_im = c_re[None] * pw_im[:, :, None, :] + c_im[None] * pw_re[:, :, None, :]
    kern = (jnp.einsum('jgop,gpi->gjoi', cl_re[:t], bb_re, precision=hp)
            - jnp.einsum('jgop,gpi->gjoi', cl_im[:t], bb_im, precision=hp))
    eye = jnp.eye(S5_LB, dtype=F32)
    nlb = S5_GROUPS // S5_LB
    lagk = jnp.einsum('qgjoi,gh->qjgiho', kern.reshape(nlb, S5_LB, t, S5_GROUP, S5_GROUP), eye,
                      precision=hp).reshape(nlb, t, LANES, LANES)
    rv_re = pw_re[t - 1 - np.arange(t)]
    rv_im = pw_im[t - 1 - np.arange(t)]
    m_re = rv_re[..., None] * bb_re[None] - rv_im[..., None] * bb_im[None]
    m_im = rv_re[..., None] * bb_im[None] + rv_im[..., None] * bb_re[None]

    g_in_lb = np.arange(S5_GROUPS) % S5_LB

    def lay_m(m):
        a = jnp.tile(m.transpose(0, 1, 3, 2), (1, 1, 1, S5_LB))
        keep = (np.arange(S5_LB * S5_STATE)[None, :] // S5_STATE == g_in_lb[:, None])
        a = jnp.where(keep[None, :, None, :], a, 0.0)
        a = a.reshape(t, nlb, S5_LB * S5_GROUP, S5_LB * S5_STATE).transpose(1, 0, 2, 3)
        return a.reshape(nlb, t * LANES, S5_LB * S5_STATE)

    def lay_c(c):
        a = jnp.tile(c.transpose(0, 1, 3, 2), (1, 1, 1, S5_LB))
        keep = (np.arange(LANES)[None, :] // S5_GROUP == g_in_lb[:, None])
        a = jnp.where(keep[None, :, None, :], a, 0.0)
        a = a.reshape(t, nlb, S5_LB * S5_STATE, LANES).transpose(1, 2, 0, 3)
        return a.reshape(nlb, S5_LB * S5_STATE, t * LANES)

    lam_t_re = pw_re[t].reshape(nlb, 1, S5_LB * S5_STATE)
    lam_t_im = pw_im[t].reshape(nlb, 1, S5_LB * S5_STATE)
    d_row = d_skip.reshape(nlb, 1, LANES)
    return (lagk.astype(BF16), lay_m(m_re).astype(BF16), lay_m(m_im).astype(BF16),
            lay_c(cl_re[1:]).astype(BF16), lay_c(-cl_im[1:]).astype(BF16), lam_t_re, lam_t_im, d_row)


def _s5(zs, mats, batch, seq):
    n_chunks = seq // S5_CHUNK
    lagk, m_re, m_im, cp_re, cp_im, l_re, l_im, d_row = mats
    nlb = lagk.shape[0]
    wcol = S5_CHUNK * LANES
    wst = S5_LB * S5_STATE
    kernel = functools.partial(_s5_kernel, n_chunks=n_chunks)
    p3 = lambda r, c: pl.BlockSpec((1, r, c), lambda q, b: (q, 0, 0))
    return pl.pallas_call(
        kernel,
        grid=(nlb, batch),
        in_specs=[pl.BlockSpec((seq, LANES), lambda q, b: (b, q)),
                  pl.BlockSpec((1, S5_CHUNK, LANES, LANES), lambda q, b: (q, 0, 0, 0)),
                  p3(wcol, wst), p3(wcol, wst), p3(wst, wcol), p3(wst, wcol),
                  p3(1, wst), p3(1, wst), p3(1, LANES)],
        out_specs=pl.BlockSpec((seq, LANES), lambda q, b: (b, q)),
        out_shape=jax.ShapeDtypeStruct(zs.shape, F32),
        scratch_shapes=[pltpu.VMEM((n_chunks, wst), F32) for _ in range(4)]
        + [pltpu.VMEM((wcol, wcol), BF16)],
        compiler_params=_cparams(("arbitrary", "arbitrary")),
        name="s5",
    )(zs, lagk, m_re, m_im, cp_re, cp_im, l_re, l_im, d_row)


def _merge_kernel(oa_ref, ys_ref, zm_ref, x_ref, mod_ref, wn_ref, wg_ref, wo_ref, g_ref, b_ref, o_ref,
                  *, alpha):
    d = x_ref.shape[-1]
    y_a = jnp.dot(oa_ref[...], wn_ref[...], preferred_element_type=F32)
    gl = jax.nn.gelu(ys_ref[...], approximate=True)
    zz = jnp.dot(gl.astype(BF16), wg_ref[...], preferred_element_type=F32)
    y_b = zz[:, :d] * jax.nn.sigmoid(zz[:, d:])
    zm = zm_ref[...]
    mix_in = jax.nn.sigmoid(zm[:, :d]) * y_a + jax.nn.sigmoid(zm[:, d:]) * y_b
    mix = jnp.dot(mix_in.astype(BF16), wo_ref[...], preferred_element_type=F32)
    gate = mod_ref[0, 2:3, :]
    r = alpha * x_ref[...] + gate * mix
    o_ref[...] = _layer_norm(r) * g_ref[...] + b_ref[...]


def _merge(oa, ys, zm, x2, mod3, wn_pad, wg, wo, ln_g, ln_b, seq, alpha):
    n, d = x2.shape
    tiles_per_batch = seq // TM_MERGE
    row = lambda w: pl.BlockSpec((TM_MERGE, w), lambda i: (i, 0))
    return pl.pallas_call(
        functools.partial(_merge_kernel, alpha=alpha),
        grid=(n // TM_MERGE,),
        in_specs=[row(oa.shape[1]), row(ys.shape[1]), row(zm.shape[1]), row(d),
                  pl.BlockSpec((1, 6, d), lambda i: (i // tiles_per_batch, 0, 0)),
                  _const_spec(wn_pad.shape), _const_spec(wg.shape), _const_spec(wo.shape),
                  _const_spec((1, d)), _const_spec((1, d))],
        out_specs=row(d),
        out_shape=jax.ShapeDtypeStruct((n, d), F32),
        compiler_params=_cparams(("arbitrary",)),
        name="merge",
    )(oa, ys, zm, x2, mod3, wn_pad, wg, wo, ln_g.reshape(1, d), ln_b.reshape(1, d))


def _pad_nsa_out(w):
    rows = []
    for h in range(HPG):
        for g in range(KV_GROUPS):
            hd = g * HPG + h
            rows.append(w[hd * HEAD_DIM:(hd + 1) * HEAD_DIM])
    return jnp.concatenate(rows, axis=0).astype(BF16)


def _ffn_kernel(x_ref, mod_ref, wup_ref, cw_ref, cb_ref, wdn_ref, g_ref, b_ref, o_ref, tail_ref, act_scr,
                *, alpha, tiles_per_batch):
    i = pl.program_id(0)
    tm = x_ref.shape[0]

    @pl.when(i % tiles_per_batch == 0)
    def _():
        tail_ref[...] = jnp.zeros(tail_ref.shape, F32)

    x = x_ref[...]
    shift = mod_ref[0, 3:4, :]
    scale = mod_ref[0, 4:5, :]
    gate = mod_ref[0, 5:6, :]
    h2 = (_layer_norm(x) * (1.0 + scale) + shift).astype(BF16)
    row = lax.broadcasted_iota(jnp.int32, (SUBLANES, FF_CHUNK), 0)

    def shift_rows(a, prev_rows):
        rolled = pltpu.roll(a, 1, axis=0)
        head = jnp.where(row == 0, prev_rows[SUBLANES - 1:SUBLANES], rolled[:SUBLANES])
        return jnp.concatenate([head, rolled[SUBLANES:]], axis=0)

    def conv_cols(c0):
        cols = slice(c0, c0 + FF_CHUNK)
        up = jnp.dot(h2, wup_ref[:, cols], preferred_element_type=F32)
        tail = tail_ref[:, cols]
        tail_ref[:, cols] = up[tm - SUBLANES:tm]
        w = cw_ref[:, cols]
        inner = shift_rows(w[0:1] * up, w[0:1] * tail) + w[1:2] * up
        prev_inner = pltpu.roll(w[0:1] * tail, 1, axis=0) + w[1:2] * tail
        return shift_rows(inner, prev_inner) + w[2:3] * up + cb_ref[:, cols]

    for k in range(D_FF // FF_CHUNK):
        val = conv_cols(k * FF_CHUNK)
        gte = conv_cols(D_FF + k * FF_CHUNK)
        act_scr[:, k * FF_CHUNK:(k + 1) * FF_CHUNK] = (gte * jax.nn.sigmoid(gte) * val).astype(BF16)
    ff = jnp.dot(act_scr[...], wdn_ref[...], preferred_element_type=F32)
    r = alpha * x + gate * ff
    o_ref[...] = _layer_norm(r) * g_ref[...] + b_ref[...]


def _ffn(x1, mod3, wup, conv_w, conv_b, wdn, ln_g, ln_b, seq, alpha):
    n, d = x1.shape
    tiles_per_batch = seq // TM_FFN
    ff2 = wup.shape[1]
    return pl.pallas_call(
        functools.partial(_ffn_kernel, alpha=alpha, tiles_per_batch=tiles_per_batch),
        grid=(n // TM_FFN,),
        in_specs=[pl.BlockSpec((TM_FFN, d), lambda i: (i, 0)),
                  pl.BlockSpec((1, 6, d), lambda i: (i // tiles_per_batch, 0, 0)),
                  _const_spec(wup.shape), _const_spec(conv_w.shape), _const_spec((1, ff2)),
                  _const_spec(wdn.shape), _const_spec((1, d)), _const_spec((1, d))],
        out_specs=pl.BlockSpec((TM_FFN, d), lambda i: (i, 0)),
        out_shape=jax.ShapeDtypeStruct((n, d), F32),
        scratch_shapes=[pltpu.VMEM((SUBLANES, ff2), F32), pltpu.VMEM((TM_FFN, ff2 // 2), BF16)],
        compiler_params=_cparams(("arbitrary",)),
        name="ffn",
    )(x1, mod3, wup, conv_w, conv_b.reshape(1, ff2), wdn, ln_g.reshape(1, d), ln_b.reshape(1, d))


def kernel(x, c, w_ada, b_ada, w_in, pe_ck, w_ck1, w_ck2, pe_cv, w_cv1, w_cv2, w_nsa_out,
           s5_a_re, s5_a_im, s5_b_re, s5_b_im, s5_c_re, s5_c_im, s5_d, s5_log_dt, w_s5_glu,
           w_o, ln1_g, ln1_b, w_up, conv_w, conv_b, w_down, ln2_g, ln2_b):
    batch, seq, d = x.shape
    depth = w_ada.shape[0]
    alpha = (2.0 * depth) ** 0.25
    n = batch * seq
    n_chunk16 = seq // CMP_STRIDE
    xf = x.reshape(n, d)
    for l in range(depth):
        mod3 = _ada(c, w_ada[l], b_ada[l]).reshape(batch, 6, d)

        zq, zc, zkv, zg, zs, zm = _inproj(xf, mod3, _build_w_all(w_in[l]), seq)

        wk1, pek, wk2 = _cmp_weights(pe_ck[l], w_ck1[l], w_ck2[l])
        wv1, pev, wv2 = _cmp_weights(pe_cv[l], w_cv1[l], w_cv2[l])
        ch = zc.reshape(2, batch, n_chunk16, CMP_STRIDE * KV_SLAB)
        kvc = _compress(ch, jnp.stack([wk1, wv1]), jnp.stack([pek, pev]), jnp.stack([wk2, wv2]))

        oa = _attention(zq, kvc, zkv, zg, batch, seq)

        mats = _s5_matrices(s5_a_re[l], s5_a_im[l], s5_b_re[l], s5_b_im[l], s5_c_re[l], s5_c_im[l],
                            s5_d[l], s5_log_dt[l])
        ys = _s5(zs, mats, batch, seq)

        x1 = _merge(oa, ys, zm, xf, mod3, _pad_nsa_out(w_nsa_out[l]), w_s5_glu[l].astype(BF16),
                    w_o[l].astype(BF16), ln1_g[l], ln1_b[l], seq, alpha)

        xf = _ffn(x1, mod3, w_up[l].astype(BF16), conv_w[l], conv_b[l], w_down[l].astype(BF16),
                  ln2_g[l], ln2_b[l], seq, alpha)
    return xf.reshape(batch, seq, d)
```

```python
import functools
import math

import jax
import jax.numpy as jnp
import ml_dtypes
import numpy as np
from jax import lax
from jax.experimental import pallas as pl
from jax.experimental.pallas import tpu as pltpu

F32 = jnp.float32
BF16 = jnp.bfloat16

D_MODEL = 1024
NSA_HEADS = 8
KV_GROUPS = 2
HPG = NSA_HEADS // KV_GROUPS
HEAD_DIM = 64
CMP_BLOCK = 32
CMP_STRIDE = 16
CMP_HIDDEN = 128
SEL_BLOCK = 64
SEL_TOPK = 16
WINDOW = 512
Q_BLOCK = 128
S5_GROUP = 16
S5_WIDTH = 512
S5_GROUPS = S5_WIDTH // S5_GROUP
S5_STATE = 64
D_FF = 2816
CONV_WIDTH = 3
LN_EPS = 1e-5
NEG_INF = -1e30
SEL_FORCE = 1e9

LANES = 128
SUBLANES = 8
VMEM_LIMIT = 56 * 1024 * 1024

TM_IN = 512
TM_MERGE = 512
TM_FFN = 512
FF_CHUNK = 256
SEL_TK = 512
WIN_KEYS = WINDOW + Q_BLOCK
WIN_BACK = WINDOW // Q_BLOCK
S5_CHUNK = 8
S5_LB = LANES // S5_GROUP
S5_NLB = S5_WIDTH // LANES

Q_COLS = NSA_HEADS * LANES
KV_SLAB = KV_GROUPS * HEAD_DIM
GATE_COLS = LANES
MERGE_COLS = 2 * D_MODEL
KV_OUT_COLS = 8 * KV_SLAB
F_HI, F_LO = 0, 3
ONE_LANE = 6
F_CHI, F_CLO, F_COFF = 7, 10, 13
LOG2E = 1.4426950408889634
MASK_BITS = 16


def _cparams(sem):
    return pltpu.CompilerParams(dimension_semantics=sem, vmem_limit_bytes=VMEM_LIMIT)


def _const_spec(shape):
    n = len(shape)
    return pl.BlockSpec(shape, lambda *_: (0,) * n)


def _layer_norm(x):
    mu = jnp.mean(x, axis=-1, keepdims=True)
    xc = x - mu
    var = jnp.mean(xc * xc, axis=-1, keepdims=True)
    return xc * lax.rsqrt(var + LN_EPS)


def _nt_dot(a, b):
    return lax.dot_general(a, b, (((1,), (1,)), ((), ())), preferred_element_type=F32)


def _group_lane0(g):
    return g * HEAD_DIM


def _ada_kernel(c_ref, w_ref, b_ref, o_ref):
    c = c_ref[...]
    a = c * jax.nn.sigmoid(c)
    o_ref[...] = jnp.dot(a, w_ref[...], preferred_element_type=F32,
                         precision=lax.Precision.HIGHEST) + b_ref[...]


def _ada(c, w_ada, b_ada):
    b, d = c.shape
    n = w_ada.shape[1]
    blk = 1024
    return pl.pallas_call(
        _ada_kernel,
        grid=(n // blk,),
        in_specs=[pl.BlockSpec((b, d), lambda j: (0, 0)),
                  pl.BlockSpec((d, blk), lambda j: (0, j)),
                  pl.BlockSpec((1, blk), lambda j: (0, j))],
        out_specs=pl.BlockSpec((b, blk), lambda j: (0, j)),
        out_shape=jax.ShapeDtypeStruct((b, n), F32),
        compiler_params=_cparams(("arbitrary",)),
        name="ada",
    )(c, w_ada, b_ada.reshape(1, n))


def _inproj_kernel(x_ref, mod_ref, w_ref, pf_ref, qf_ref, q_ref, c_ref, kv_ref, g_ref, s_ref, m_ref, zc_scr):
    hn = _layer_norm(x_ref[...])
    shift = mod_ref[0, 0:1, :]
    scale = mod_ref[0, 1:2, :]
    h = (hn * (1.0 + scale) + shift).astype(BF16)
    tm = x_ref.shape[0]

    def proj(col, width):
        return jnp.dot(h, w_ref[:, col:col + width], preferred_element_type=F32)

    col = 0
    q_ref[...] = (proj(col, Q_COLS) + qf_ref[...]).astype(q_ref.dtype)
    col += Q_COLS
    for kind in range(2):
        zc_scr[...] = proj(col, KV_SLAB)
        for tok in range(CMP_STRIDE):
            c_ref[kind, :, tok * KV_SLAB:(tok + 1) * KV_SLAB] = zc_scr[
                pl.ds(tok, tm // CMP_STRIDE, stride=CMP_STRIDE), :]
        col += KV_SLAB
    lane = lax.broadcasted_iota(jnp.int32, (x_ref.shape[0], LANES), 1)
    pf = pf_ref[...]
    for part in range(4):
        z = proj(col + part * KV_SLAB, KV_SLAB)
        for g in range(KV_GROUPS):
            own = (lane < HEAD_DIM) if g == 0 else (lane >= HEAD_DIM)
            slot = (part * KV_GROUPS + g) * KV_SLAB
            kv_ref[:, slot:slot + KV_SLAB] = jnp.where(
                own, z, pf[:, g * LANES:(g + 1) * LANES]).astype(kv_ref.dtype)
    col += 4 * KV_SLAB
    g_ref[...] = proj(col, GATE_COLS)
    col += GATE_COLS
    s_ref[...] = proj(col, S5_WIDTH)
    col += S5_WIDTH
    m_ref[...] = proj(col, MERGE_COLS)


def _log2e_terms():
    terms, rest = [], np.float64(LOG2E)
    for _ in range(3):
        t = np.float64(np.float32(rest).astype(ml_dtypes.bfloat16))
        terms.append(float(t))
        rest -= t
    return terms


def _key_position_features(seq):
    p = np.arange(seq)
    out = np.zeros((seq, KV_GROUPS, LANES), np.float32)
    for g in range(KV_GROUPS):
        l0 = _group_lane0(1 - g)
        for k in range(3):
            out[:, g, l0 + F_HI + k] = p // SEL_BLOCK
            out[:, g, l0 + F_LO + k] = p % SEL_BLOCK
        out[:, g, l0 + ONE_LANE] = 1.0
    return jnp.asarray(out.reshape(seq, KV_GROUPS * LANES))


def _query_feature_row():
    c = _log2e_terms()
    row = np.zeros((NSA_HEADS, LANES), np.float32)
    for hd in range(NSA_HEADS):
        g = hd // HPG
        slope = 2.0 ** -(hd + 1)
        l0 = _group_lane0(1 - g)
        for k in range(3):
            row[hd, l0 + F_HI + k] = SEL_BLOCK * slope * c[k]
            row[hd, l0 + F_LO + k] = slope * c[k]
            row[hd, l0 + F_CHI + k] = 16 * CMP_STRIDE * slope * c[k]
            row[hd, l0 + F_CLO + k] = CMP_STRIDE * slope * c[k]
            row[hd, l0 + F_COFF + k] = slope * c[k]
    return jnp.asarray(row.reshape(1, Q_COLS))


def _inproj(x2, mod3, w_all, seq):
    n, d = x2.shape
    tiles_per_batch = seq // TM_IN
    rows16 = TM_IN // CMP_STRIDE
    widths = (Q_COLS, KV_OUT_COLS, GATE_COLS, S5_WIDTH, MERGE_COLS)
    dtypes = (BF16, BF16, F32, F32, F32)
    pf = _key_position_features(seq)
    qf = _query_feature_row()
    row_spec = lambda w: pl.BlockSpec((TM_IN, w), lambda i: (i, 0))
    out_specs = [row_spec(Q_COLS),
                 pl.BlockSpec((2, rows16, CMP_STRIDE * KV_SLAB), lambda i: (0, i, 0))]
    out_specs += [row_spec(w) for w in widths[1:]]
    out_shape = [jax.ShapeDtypeStruct((n, Q_COLS), BF16),
                 jax.ShapeDtypeStruct((2, n // CMP_STRIDE, CMP_STRIDE * KV_SLAB), F32)]
    out_shape += [jax.ShapeDtypeStruct((n, w), dt) for w, dt in zip(widths[1:], dtypes[1:])]
    return pl.pallas_call(
        _inproj_kernel,
        grid=(n // TM_IN,),
        in_specs=[pl.BlockSpec((TM_IN, d), lambda i: (i, 0)),
                  pl.BlockSpec((1, 6, d), lambda i: (i // tiles_per_batch, 0, 0)),
                  _const_spec(w_all.shape),
                  pl.BlockSpec((TM_IN, KV_GROUPS * LANES), lambda i: (i % tiles_per_batch, 0)),
                  _const_spec(qf.shape)],
        out_specs=out_specs,
        out_shape=out_shape,
        scratch_shapes=[pltpu.VMEM((TM_IN, KV_SLAB), F32)],
        compiler_params=_cparams(("arbitrary",)),
        name="inproj",
    )(x2, mod3, w_all, pf, qf)


def _build_w_all(w_in):
    d = w_in.shape[0]
    cq = NSA_HEADS * HEAD_DIM
    ckv = 6 * KV_SLAB
    cg = 3 * NSA_HEADS
    zeros = jnp.zeros((d, HEAD_DIM), w_in.dtype)
    pieces = []
    for hd in range(NSA_HEADS):
        wq = w_in[:, hd * HEAD_DIM:(hd + 1) * HEAD_DIM] * (HEAD_DIM ** -0.5 * LOG2E)
        pieces += [wq, zeros] if hd < HPG else [zeros, wq]
    wq_pad = jnp.concatenate(pieces, axis=1)
    wkv = w_in[:, cq:cq + ckv]
    wg = jnp.pad(w_in[:, cq + ckv:cq + ckv + cg], ((0, 0), (0, GATE_COLS - cg)))
    rest = w_in[:, cq + ckv + cg:]
    return jnp.concatenate([wq_pad, wkv, wg, rest], axis=1).astype(BF16)


def _cmp_kernel(ch_ref, w1_ref, pe_ref, w2_ref, o_ref):
    ch = ch_ref[0, 0]
    a = jnp.dot((ch + pe_ref[0, 0]).astype(BF16), w1_ref[0, 0], preferred_element_type=F32)
    b = jnp.dot((ch + pe_ref[0, 1]).astype(BF16), w1_ref[0, 1], preferred_element_type=F32)
    n = a.shape[0]
    hsum = a + pltpu.roll(b, n - 1, axis=0)
    hact = hsum * jax.nn.sigmoid(hsum)
    o_ref[0, 0] = jnp.dot(hact.astype(BF16), w2_ref[0], preferred_element_type=F32).astype(o_ref.dtype)


def _cmp_weights(pe, w1, w2):
    half = CMP_BLOCK // 2
    halves = []
    pes = []
    for lo in (0, half):
        w = w1[lo:lo + half]
        z = jnp.zeros_like(w)
        w_g0 = jnp.concatenate([w, z], axis=-1)
        w_g1 = jnp.concatenate([z, w], axis=-1)
        halves.append(jnp.stack([w_g0, w_g1], axis=1).reshape(half * KV_SLAB, KV_GROUPS * CMP_HIDDEN))
        p = pe[lo:lo + half]
        pes.append(jnp.broadcast_to(p[:, None, :], (half, KV_GROUPS, HEAD_DIM)).reshape(1, half * KV_SLAB))
    z2 = jnp.zeros_like(w2)
    w2_blk = jnp.concatenate([jnp.concatenate([w2, z2], axis=1), jnp.concatenate([z2, w2], axis=1)], axis=0)
    return jnp.stack(halves).astype(BF16), jnp.stack(pes), w2_blk.astype(BF16)


def _compress(ch, w1s, pes, w2s):
    kinds, b, nchunk, width = ch.shape
    hid = w1s.shape[-1]
    return pl.pallas_call(
        _cmp_kernel,
        grid=(kinds, b),
        in_specs=[pl.BlockSpec((1, 1, nchunk, width), lambda k, i: (k, i, 0, 0)),
                  pl.BlockSpec((1, 2, width, hid), lambda k, i: (k, 0, 0, 0)),
                  pl.BlockSpec((1, 2, 1, width), lambda k, i: (k, 0, 0, 0)),
                  pl.BlockSpec((1, hid, KV_SLAB), lambda k, i: (k, 0, 0))],
        out_specs=pl.BlockSpec((1, 1, nchunk, KV_SLAB), lambda k, i: (k, i, 0, 0)),
        out_shape=jax.ShapeDtypeStruct((kinds, b, nchunk, KV_SLAB), BF16),
        compiler_params=_cparams(("arbitrary", "arbitrary")),
        name="compress",
    )(ch, w1s, pes, w2s)


def _softmax_parts(s):
    m = jnp.max(s, axis=-1, keepdims=True)
    e = jnp.exp2(s - m)
    return m, e, jnp.sum(e, axis=-1, keepdims=True)


def _attn_kernel(q_ref, kc_ref, vc_ref, ks0_ref, ks1_ref, vs0_ref, vs1_ref, kw0_ref, kw1_ref,
                 vw0_ref, vw1_ref, zg_ref,
                 ov_ref, cf_ref, hot_ref, tri_ref, wm_ref, gsel_ref, o_ref, v_scr, lhs_scr, m_scr, acc_scr,
                 part_scr, tile_smem,
                 *, n_cmp, n_sel):
    qb = pl.program_id(1)
    t0 = qb * Q_BLOCK
    ncp = kc_ref.shape[2]
    gates = jax.nn.sigmoid(zg_ref[...])
    g_hi = gates.astype(BF16)
    g_lo = (gates - g_hi.astype(F32)).astype(BF16)
    gates_b = jnp.dot(jnp.concatenate([g_hi, g_lo], axis=1), gsel_ref[...], preferred_element_type=F32)

    def gate(g, h, branch):
        c = (g * HPG + h) * 3 + branch
        return gates_b[:, c * LANES:(c + 1) * LANES]
    r_col = lax.broadcasted_iota(jnp.int32, (Q_BLOCK, 1), 0)
    lane = lax.broadcasted_iota(jnp.int32, (Q_BLOCK, LANES), 1)
    tri = tri_ref[...]
    wmask = wm_ref[jnp.minimum(qb, WIN_BACK)]
    hrows = [slice(h * Q_BLOCK, (h + 1) * Q_BLOCK) for h in range(HPG)]

    t0a = pl.multiple_of(t0, Q_BLOCK)
    w0 = pl.multiple_of(jnp.maximum(qb - WIN_BACK, 0) * Q_BLOCK, Q_BLOCK)

    scores = []
    for g in range(KV_GROUPS):
        ks_ref = (ks0_ref, ks1_ref)[g]
        kw_ref = (kw0_ref, kw1_ref)[g]
        own_c = lax.broadcasted_iota(jnp.int32, (ncp, LANES), 1)
        own_c = (own_c < HEAD_DIM) if g == 0 else (own_c >= HEAD_DIM)
        lhs_pos = jnp.concatenate(
            [q_ref[:, (g * HPG + h) * LANES:(g * HPG + h + 1) * LANES] for h in range(HPG)], axis=0)
        kc_aug = jnp.where(own_c, kc_ref[0, 0], cf_ref[g])
        scores.append([lhs_pos, _nt_dot(lhs_pos, kc_aug)])
    for g in range(KV_GROUPS):
        ks_ref = (ks0_ref, ks1_ref)[g]
        kw_ref = (kw0_ref, kw1_ref)[g]
        lhs_pos = scores[g][0]
        scores[g].append(_nt_dot(lhs_pos, ks_ref[pl.ds(t0a, Q_BLOCK), :]))
        scores[g].append(_nt_dot(lhs_pos, kw_ref[pl.ds(w0, WIN_KEYS), :]))

    block_used = []
    cmp_out = []
    for g in range(KV_GROUPS):
        sc = scores[g][1]

        n_i = lax.broadcasted_iota(jnp.int32, (Q_BLOCK, ncp), 1)
        r_i = lax.broadcasted_iota(jnp.int32, (Q_BLOCK, ncp), 0)
        valid_c = jnp.logical_and((n_i * CMP_STRIDE + (CMP_BLOCK - 1)) <= (t0 + r_i), n_i < n_cmp)
        row_any = ((t0 + r_col) >= (CMP_BLOCK - 1)).astype(F32)
        sc = jnp.where(valid_c[None], sc.reshape(HPG, Q_BLOCK, ncp), NEG_INF)
        _, e, l = _softmax_parts(sc)
        p_c = (e * (row_any[None] / l)).astype(BF16)
        o_c = jnp.dot(p_c.reshape(HPG * Q_BLOCK, ncp), vc_ref[0, 0], preferred_element_type=F32)
        imp = jnp.dot(jnp.concatenate([p_c[h] for h in range(HPG)], axis=1), ov_ref[...],
                      preferred_element_type=F32)
        cmp_out.append((o_c, imp))

    for g in range(KV_GROUPS):
        lhs_pos = scores[g][0]
        imp = cmp_out[g][1]

        imp_t = imp.T[:n_sel]
        j_i = lax.broadcasted_iota(jnp.int32, (n_sel, Q_BLOCK), 0)
        cur = lax.shift_right_logical(t0 + lax.broadcasted_iota(jnp.int32, (n_sel, Q_BLOCK), 1),
                                      int(math.log2(SEL_BLOCK)))
        forced = jnp.logical_or(j_i == 0, jnp.logical_or(j_i == cur, j_i == cur - 1))
        valid = j_i <= cur
        v = jnp.where(forced, SEL_FORCE, jnp.where(valid, imp_t, -SEL_FORCE))
        v_scr[g] = v
        n_chunk = n_sel // SUBLANES
        chunks = [v[k * SUBLANES:(k + 1) * SUBLANES] for k in range(n_chunk)]
        ranks = [jnp.zeros((SUBLANES, Q_BLOCK), jnp.int32) for _ in range(n_chunk)]
        sub_i = lax.broadcasted_iota(jnp.int32, (SUBLANES, Q_BLOCK), 0)
        for i in range(n_sel):
            vi = v_scr[g, i:i + 1, :]
            ki = i // SUBLANES
            for k in range(n_chunk):
                if k > ki:
                    beats = jnp.where(vi >= chunks[k], 1, 0)
                elif k < ki:
                    beats = jnp.where(vi > chunks[k], 1, 0)
                else:
                    beats = jnp.where(sub_i > (i - ki * SUBLANES),
                                      jnp.where(vi >= chunks[k], 1, 0),
                                      jnp.where(vi > chunks[k], 1, 0))
                ranks[k] = ranks[k] + beats
        rank = jnp.concatenate(ranks, axis=0)
        chosen = jnp.logical_and(jnp.logical_and(rank < SEL_TOPK, valid), j_i < 2 * qb)
        bias_t = jnp.where(chosen, 0.0, NEG_INF)
        bias_t = jnp.concatenate([bias_t, jnp.full((LANES - n_sel, Q_BLOCK), NEG_INF, F32)], axis=0)
        bias_f = bias_t.T
        block_used.append(jnp.max(bias_f, axis=0, keepdims=True))
        bias = bias_f.astype(BF16)
        lhs_scr[g] = jnp.concatenate([lhs_pos, jnp.concatenate([bias] * HPG, axis=0)], axis=1)

    for g in range(KV_GROUPS):
        _, _, sd, sw = scores[g]
        o_c = cmp_out[g][0]
        vs_ref = (vs0_ref, vs1_ref)[g]
        vw_ref = (vw0_ref, vw1_ref)[g]
        f0 = _group_lane0(1 - g)

        vd = vs_ref[pl.ds(t0a, Q_BLOCK), :]
        sd = (sd.reshape(HPG, Q_BLOCK, Q_BLOCK) + tri[None]).reshape(HPG * Q_BLOCK, Q_BLOCK)
        m = jnp.max(sd, axis=-1, keepdims=True)
        m_scr[g] = jnp.broadcast_to(m, m_scr.shape[1:])
        acc_scr[g] = jnp.dot(jnp.exp2(sd - m).astype(BF16), vd, preferred_element_type=F32)

        v_w = vw_ref[pl.ds(w0, WIN_KEYS), :]
        sw = (sw.reshape(HPG, Q_BLOCK, WIN_KEYS) + wmask[None]).reshape(HPG * Q_BLOCK, WIN_KEYS)
        e = jnp.exp2(sw - jnp.max(sw, axis=-1, keepdims=True))
        o_w = jnp.dot(e.astype(BF16), v_w, preferred_element_type=F32)
        o_w = o_w * (1.0 / o_w[:, f0 + ONE_LANE:f0 + ONE_LANE + 1])

        for h in range(HPG):
            part_scr[g, hrows[h]] = gate(g, h, 0) * o_c[hrows[h]] + gate(g, h, 2) * o_w[hrows[h]]

    n_key_tiles = hot_ref.shape[0] // Q_BLOCK - 1
    per_step = SEL_TK // Q_BLOCK
    used = jnp.maximum(block_used[0], block_used[1])
    used = jnp.maximum(used, pltpu.roll(used, LANES - 1, axis=1))
    lane1 = lax.broadcasted_iota(jnp.int32, (1, LANES), 1)
    tile_of_lane = lax.shift_right_logical(lane1, 1)
    bit = jnp.where(jnp.logical_and(used > -1.0, (lane1 & 1) == 0),
                    jnp.left_shift(1, tile_of_lane & (MASK_BITS - 1)), 0).astype(F32)
    word = lax.shift_right_logical(tile_of_lane, int(math.log2(MASK_BITS)))
    masks = [jnp.sum(jnp.where(word == w, bit, 0.0), axis=1, keepdims=True)[0, 0].astype(jnp.int32)
             for w in range((n_key_tiles + MASK_BITS - 1) // MASK_BITS)]
    count = jnp.int32(0)
    for j in range(n_key_tiles):
        tile_smem[count] = j
        count = count + (lax.shift_right_logical(masks[j // MASK_BITS], j % MASK_BITS) & 1)
    for k in range(per_step - 1):
        tile_smem[count + k] = n_key_tiles

    def tile_offsets(i, k):
        j = tile_smem[i * per_step + k]
        hot_off = pl.multiple_of(j * Q_BLOCK, Q_BLOCK)
        kv_off = pl.multiple_of(jnp.minimum(j, n_key_tiles - 1) * Q_BLOCK, Q_BLOCK)
        return kv_off, hot_off

    def sel_body(i, carry):
        offs = [tile_offsets(i, k) for k in range(per_step)]
        hot = jnp.concatenate([hot_ref[pl.ds(ho, Q_BLOCK), :] for _, ho in offs], axis=0)
        s_g = []
        for g in range(KV_GROUPS):
            ks_ref = (ks0_ref, ks1_ref)[g]
            k_t = jnp.concatenate([ks_ref[pl.ds(ko, Q_BLOCK), :] for ko, _ in offs], axis=0)
            s_g.append(_nt_dot(lhs_scr[g], jnp.concatenate([k_t, hot], axis=1)))
        for g in range(KV_GROUPS):
            vs_ref = (vs0_ref, vs1_ref)[g]
            v_t = jnp.concatenate([vs_ref[pl.ds(ko, Q_BLOCK), :] for ko, _ in offs], axis=0)
            s = s_g[g]
            m_old = m_scr[g]
            m_new = jnp.maximum(m_old, jnp.max(s, axis=-1, keepdims=True))
            p = jnp.exp2(s - jnp.tile(m_new, (1, SEL_TK // LANES)))
            alpha = jnp.exp2(m_old - m_new)
            m_scr[g] = m_new
            pv = jnp.dot(p.astype(BF16), v_t, preferred_element_type=F32)
            acc_scr[g] = alpha * acc_scr[g] + pv
        return carry

    lax.fori_loop(0, (count + per_step - 1) // per_step, sel_body, 0)

    o_s = []
    for g in range(KV_GROUPS):
        one = _group_lane0(1 - g) + ONE_LANE
        acc = acc_scr[g]
        o_s.append(acc * (1.0 / acc[:, one:one + 1]))
    for h in range(HPG):
        outs = [part_scr[g, hrows[h]] + gate(g, h, 1) * o_s[g][hrows[h]] for g in range(KV_GROUPS)]
        o_ref[:, h * LANES:(h + 1) * LANES] = jnp.where(lane < HEAD_DIM, outs[0], outs[1]).astype(o_ref.dtype)


def _attention_tables(seq, ncp, n_cmp, n_sel):
    cstart = np.arange(ncp) * CMP_STRIDE
    sstart = np.arange(n_sel) * SEL_BLOCK
    overlap = ((cstart[:, None] < sstart[None, :] + SEL_BLOCK)
               & (cstart[:, None] + CMP_BLOCK > sstart[None, :])
               & (np.arange(ncp)[:, None] < n_cmp)).astype(np.float32)
    overlap = np.pad(overlap, ((0, 0), (0, LANES - n_sel)))
    ov4 = np.tile(overlap, (HPG, 1))
    n = np.arange(ncp)
    cfeat = np.zeros((KV_GROUPS, ncp, LANES), np.float32)
    for g in range(KV_GROUPS):
        l0 = _group_lane0(1 - g)
        for k in range(3):
            cfeat[g, :, l0 + F_CHI + k] = n // 16
            cfeat[g, :, l0 + F_CLO + k] = n % 16
            cfeat[g, :, l0 + F_COFF + k] = CMP_BLOCK - 1
    blk = np.concatenate([np.arange(seq) // SEL_BLOCK, np.full(Q_BLOCK, n_sel)])
    hot = (blk[:, None] == np.arange(LANES)[None, :]).astype(np.float32)
    r = np.arange(Q_BLOCK)[:, None]
    tri = np.where(np.arange(Q_BLOCK)[None, :] <= r, 0.0, NEG_INF).astype(np.float32)
    c = np.arange(WIN_KEYS)[None, :]
    wm = []
    for qb in range(WIN_BACK + 1):
        dist = (qb * Q_BLOCK + r) - c if qb < WIN_BACK else (WINDOW + r) - c
        wm.append(np.where((dist >= 0) & (dist < WINDOW), 0.0, NEG_INF))
    wm = np.stack(wm).astype(np.float32)
    n_gate = 3 * NSA_HEADS
    gsel = np.zeros((2 * LANES, n_gate, LANES), np.float32)
    for c in range(n_gate):
        gsel[c, c, :] = 1.0
        gsel[LANES + c, c, :] = 1.0
    gsel = jnp.asarray(gsel.reshape(2 * LANES, n_gate * LANES), BF16)
    return (jnp.asarray(ov4, BF16), jnp.asarray(cfeat, BF16), jnp.asarray(hot, BF16), gsel,
            jnp.asarray(tri), jnp.asarray(wm))


def _attention(zq, kvc, zkv, zg, batch, seq):
    n = zq.shape[0]
    nqb = seq // Q_BLOCK
    ncp = kvc.shape[2]
    n_cmp = seq // CMP_STRIDE - 1
    n_sel = seq // SEL_BLOCK
    assert n_sel <= HEAD_DIM and n_sel % SUBLANES == 0 and seq % SEL_TK == 0 and nqb > WIN_BACK
    assert ncp // 16 <= 256 and seq // SEL_BLOCK <= 256
    ov4, cfeat, hot, gsel, tri, wm = _attention_tables(seq, ncp, n_cmp, n_sel)
    out_cols = HPG * LANES

    kernel = functools.partial(_attn_kernel, n_cmp=n_cmp, n_sel=n_sel)
    kv_spec = lambda idx: pl.BlockSpec((seq, KV_SLAB), lambda b, i, idx=idx: (b, idx))
    return pl.pallas_call(
        kernel,
        grid=(batch, nqb),
        in_specs=[pl.BlockSpec((Q_BLOCK, Q_COLS), lambda b, i: (b * nqb + i, 0)),
                  pl.BlockSpec((1, 1, ncp, KV_SLAB), lambda b, i: (0, b, 0, 0)),
                  pl.BlockSpec((1, 1, ncp, KV_SLAB), lambda b, i: (1, b, 0, 0)),
                  *[kv_spec(k) for k in range(8)],
                  pl.BlockSpec((Q_BLOCK, GATE_COLS), lambda b, i: (b * nqb + i, 0)),
                  _const_spec(ov4.shape), _const_spec(cfeat.shape), _const_spec(hot.shape),
                  _const_spec(tri.shape), _const_spec(wm.shape), _const_spec(gsel.shape)],
        out_specs=pl.BlockSpec((Q_BLOCK, out_cols), lambda b, i: (b * nqb + i, 0)),
        out_shape=jax.ShapeDtypeStruct((n, out_cols), BF16),
        scratch_shapes=[pltpu.VMEM((KV_GROUPS, n_sel, Q_BLOCK), F32),
                        pltpu.VMEM((KV_GROUPS, HPG * Q_BLOCK, 2 * LANES), BF16),
                        pltpu.VMEM((KV_GROUPS, HPG * Q_BLOCK, LANES), F32),
                        pltpu.VMEM((KV_GROUPS, HPG * Q_BLOCK, LANES), F32),
                        pltpu.VMEM((KV_GROUPS, HPG * Q_BLOCK, LANES), F32),
                        pltpu.SMEM((seq // Q_BLOCK + SEL_TK // Q_BLOCK,), jnp.int32)],
        compiler_params=_cparams(("arbitrary", "arbitrary")),
        name="nsa_attn",
    )(zq, kvc, kvc, *([zkv] * 8), zg, ov4, cfeat, hot, tri, wm, gsel)


def _s5_kernel(u_ref, lagk_ref, mre_ref, mim_ref, cre_ref, cim_ref, lre_ref, lim_ref, d_ref, y_ref,
               ere, eim, xre, xim, tp_scr, *, n_chunks):
    def tok(t):
        return pl.ds(t, n_chunks, stride=S5_CHUNK)

    @pl.when(pl.program_id(1) == 0)
    def _():
        tp_scr[...] = jnp.zeros(tp_scr.shape, tp_scr.dtype)
        for s in range(S5_CHUNK):
            for t in range(s, S5_CHUNK):
                tp_scr[s * LANES:(s + 1) * LANES, t * LANES:(t + 1) * LANES] = lagk_ref[0, t - s]

    u = jnp.concatenate([u_ref[tok(t), :] for t in range(S5_CHUNK)], axis=1).astype(BF16)
    y = jnp.dot(u, tp_scr[...], preferred_element_type=F32)
    ere[...] = jnp.dot(u, mre_ref[0, 0], preferred_element_type=F32)
    eim[...] = jnp.dot(u, mim_ref[0, 0], preferred_element_type=F32)
    lr = lre_ref[0]
    li = lim_ref[0]

    def body(c, carry):
        xr, xi = carry
        row = pl.ds(c, 1)
        xre[row, :] = xr
        xim[row, :] = xi
        er = ere[row, :]
        ei = eim[row, :]
        return lr * xr - li * xi + er, lr * xi + li * xr + ei

    zero = jnp.zeros((1, ere.shape[1]), F32)
    lax.fori_loop(0, n_chunks, body, (zero, zero), unroll=8)
    y = y + jnp.dot(xre[...].astype(BF16), cre_ref[0, 0], preferred_element_type=F32)
    y = y + jnp.dot(xim[...].astype(BF16), cim_ref[0, 0], preferred_element_type=F32)
    d = d_ref[0]
    for t in range(S5_CHUNK):
        y_ref[tok(t), :] = y[:, t * LANES:(t + 1) * LANES] + d * u_ref[tok(t), :]


def _s5_matrices(a_re, a_im, b_re, b_im, c_re, c_im, d_skip, log_dt):
    t = S5_CHUNK
    hp = lax.Precision.HIGHEST
    dt = jnp.exp(log_dt)[:, None]
    lam_re = jnp.minimum(a_re, -1e-4)
    lam_im = a_im
    mag = jnp.exp(lam_re * dt)
    ang = lam_im * dt
    lb_re = mag * jnp.cos(ang)
    lb_im = mag * jnp.sin(ang)
    den = lam_re * lam_re + lam_im * lam_im
    nr = lb_re - 1.0
    coef_re = (nr * lam_re + lb_im * lam_im) / den
    coef_im = (lb_im * lam_re - nr * lam_im) / den
    bb_re = coef_re[..., None] * b_re - coef_im[..., None] * b_im
    bb_im = coef_re[..., None] * b_im + coef_im[..., None] * b_re
    j = jnp.arange(t + 1, dtype=F32)[:, None, None]
    pmag = jnp.exp(j * (lam_re * dt)[None])
    pw_re = pmag * jnp.cos(j * ang[None])
    pw_im = pmag * jnp.sin(j * ang[None])
    cl_re = c_re[None] * pw_re[:, :, None, :] - c_im[None] * pw_im[:, :, None, :]
    cl_im = c_re[None] * pw_im[:, :, None, :] + c_im[None] * pw_re[:, :, None, :]
    kern = (jnp.einsum('jgop,gpi->gjoi', cl_re[:t], bb_re, precision=hp)
            - jnp.einsum('jgop,gpi->gjoi', cl_im[:t], bb_im, precision=hp))
    eye = jnp.eye(S5_LB, dtype=F32)
    nlb = S5_GROUPS // S5_LB
    lagk = jnp.einsum('qgjoi,gh->qjgiho', kern.reshape(nlb, S5_LB, t, S5_GROUP, S5_GROUP), eye,
                      precision=hp).reshape(nlb, t, LANES, LANES)
    rv_re = pw_re[t - 1 - np.arange(t)]
    rv_im = pw_im[t - 1 - np.arange(t)]
    m_re = rv_re[..., None] * bb_re[None] - rv_im[..., None] * bb_im[None]
    m_im = rv_re[..., None] * bb_im[None] + rv_im[..., None] * bb_re[None]

    def lay_m(m):
        m = m.reshape(2, t, nlb, S5_LB, S5_STATE, S5_GROUP)
        return jnp.einsum('csqgpi,gh->cqsgihp', m, eye, precision=hp).reshape(
            2, nlb, t * LANES, S5_LB * S5_STATE)

    def lay_c(c):
        c = c.reshape(2, t, nlb, S5_LB, S5_GROUP, S5_STATE)
        return jnp.einsum('ctqgop,gh->cqgptho', c, eye, precision=hp).reshape(
            2, nlb, S5_LB * S5_STATE, t * LANES)

    lam_t_re = pw_re[t].reshape(nlb, 1, S5_LB * S5_STATE)
    lam_t_im = pw_im[t].reshape(nlb, 1, S5_LB * S5_STATE)
    d_row = d_skip.reshape(nlb, 1, LANES)
    m_ops = lay_m(jnp.stack([m_re, m_im])).astype(BF16)
    c_ops = lay_c(jnp.stack([cl_re[1:], -cl_im[1:]])).astype(BF16)
    return lagk.astype(BF16), m_ops, c_ops, lam_t_re, lam_t_im, d_row


def _s5(zs, mats, batch, seq):
    n_chunks = seq // S5_CHUNK
    lagk, m_ops, c_ops, l_re, l_im, d_row = mats
    nlb = lagk.shape[0]
    wcol = S5_CHUNK * LANES
    wst = S5_LB * S5_STATE
    kernel = functools.partial(_s5_kernel, n_chunks=n_chunks)
    p3 = lambda r, c: pl.BlockSpec((1, r, c), lambda q, b: (q, 0, 0))
    part = lambda k, r, c: pl.BlockSpec((1, 1, r, c), lambda q, b, k=k: (k, q, 0, 0))
    return pl.pallas_call(
        kernel,
        grid=(nlb, batch),
        in_specs=[pl.BlockSpec((seq, LANES), lambda q, b: (b, q)),
                  pl.BlockSpec((1, S5_CHUNK, LANES, LANES), lambda q, b: (q, 0, 0, 0)),
                  part(0, wcol, wst), part(1, wcol, wst), part(0, wst, wcol), part(1, wst, wcol),
                  p3(1, wst), p3(1, wst), p3(1, LANES)],
        out_specs=pl.BlockSpec((seq, LANES), lambda q, b: (b, q)),
        out_shape=jax.ShapeDtypeStruct(zs.shape, F32),
        scratch_shapes=[pltpu.VMEM((n_chunks, wst), F32) for _ in range(4)]
        + [pltpu.VMEM((wcol, wcol), BF16)],
        compiler_params=_cparams(("arbitrary", "arbitrary")),
        name="s5",
    )(zs, lagk, m_ops, m_ops, c_ops, c_ops, l_re, l_im, d_row)


def _merge_kernel(oa_ref, ys_ref, zm_ref, x_ref, mod_ref, wn_ref, wg_ref, wo_ref, g_ref, b_ref, o_ref,
                  *, alpha):
    d = x_ref.shape[-1]
    y_a = jnp.dot(oa_ref[...], wn_ref[...], preferred_element_type=F32)
    gl = jax.nn.gelu(ys_ref[...], approximate=True)
    zz = jnp.dot(gl.astype(BF16), wg_ref[...], preferred_element_type=F32)
    y_b = zz[:, :d] * jax.nn.sigmoid(zz[:, d:])
    zm = zm_ref[...]
    mix_in = jax.nn.sigmoid(zm[:, :d]) * y_a + jax.nn.sigmoid(zm[:, d:]) * y_b
    mix = jnp.dot(mix_in.astype(BF16), wo_ref[...], preferred_element_type=F32)
    gate = mod_ref[0, 2:3, :]
    r = alpha * x_ref[...] + gate * mix
    o_ref[...] = _layer_norm(r) * g_ref[...] + b_ref[...]


def _merge(oa, ys, zm, x2, mod3, wn_pad, wg, wo, ln_g, ln_b, seq, alpha):
    n, d = x2.shape
    tiles_per_batch = seq // TM_MERGE
    row = lambda w: pl.BlockSpec((TM_MERGE, w), lambda i: (i, 0))
    return pl.pallas_call(
        functools.partial(_merge_kernel, alpha=alpha),
        grid=(n // TM_MERGE,),
        in_specs=[row(oa.shape[1]), row(ys.shape[1]), row(zm.shape[1]), row(d),
                  pl.BlockSpec((1, 6, d), lambda i: (i // tiles_per_batch, 0, 0)),
                  _const_spec(wn_pad.shape), _const_spec(wg.shape), _const_spec(wo.shape),
                  _const_spec((1, d)), _const_spec((1, d))],
        out_specs=row(d),
        out_shape=jax.ShapeDtypeStruct((n, d), F32),
        compiler_params=_cparams(("arbitrary",)),
        name="merge",
    )(oa, ys, zm, x2, mod3, wn_pad, wg, wo, ln_g.reshape(1, d), ln_b.reshape(1, d))


def _pad_nsa_out(w):
    rows = []
    for h in range(HPG):
        for g in range(KV_GROUPS):
            hd = g * HPG + h
            rows.append(w[hd * HEAD_DIM:(hd + 1) * HEAD_DIM])
    return jnp.concatenate(rows, axis=0).astype(BF16)


def _ffn_kernel(x_ref, mod_ref, wup_ref, cw_ref, cb_ref, wdn_ref, g_ref, b_ref, o_ref, tail_ref, act_scr,
                *, alpha, tiles_per_batch):
    i = pl.program_id(0)
    tm = x_ref.shape[0]

    @pl.when(i % tiles_per_batch == 0)
    def _():
        tail_ref[...] = jnp.zeros(tail_ref.shape, F32)

    x = x_ref[...]
    shift = mod_ref[0, 3:4, :]
    scale = mod_ref[0, 4:5, :]
    gate = mod_ref[0, 5:6, :]
    h2 = (_layer_norm(x) * (1.0 + scale) + shift).astype(BF16)
    row = lax.broadcasted_iota(jnp.int32, (SUBLANES, FF_CHUNK), 0)

    def shift_rows(a, prev_rows):
        rolled = pltpu.roll(a, 1, axis=0)
        head = jnp.where(row == 0, prev_rows[SUBLANES - 1:SUBLANES], rolled[:SUBLANES])
        return jnp.concatenate([head, rolled[SUBLANES:]], axis=0)

    def conv_cols(c0):
        cols = slice(c0, c0 + FF_CHUNK)
        up = jnp.dot(h2, wup_ref[:, cols], preferred_element_type=F32)
        tail = tail_ref[:, cols]
        tail_ref[:, cols] = up[tm - SUBLANES:tm]
        w = cw_ref[:, cols]
        inner = shift_rows(w[0:1] * up, w[0:1] * tail) + w[1:2] * up
        prev_inner = pltpu.roll(w[0:1] * tail, 1, axis=0) + w[1:2] * tail
        return shift_rows(inner, prev_inner) + w[2:3] * up + cb_ref[:, cols]

    for k in range(D_FF // FF_CHUNK):
        val = conv_cols(k * FF_CHUNK)
        gte = conv_cols(D_FF + k * FF_CHUNK)
        act_scr[:, k * FF_CHUNK:(k + 1) * FF_CHUNK] = (gte * jax.nn.sigmoid(gte) * val).astype(BF16)
    ff = jnp.dot(act_scr[...], wdn_ref[...], preferred_element_type=F32)
    r = alpha * x + gate * ff
    o_ref[...] = _layer_norm(r) * g_ref[...] + b_ref[...]


def _ffn(x1, mod3, wup, conv_w, conv_b, wdn, ln_g, ln_b, seq, alpha):
    n, d = x1.shape
    tiles_per_batch = seq // TM_FFN
    ff2 = wup.shape[1]
    return pl.pallas_call(
        functools.partial(_ffn_kernel, alpha=alpha, tiles_per_batch=tiles_per_batch),
        grid=(n // TM_FFN,),
        in_specs=[pl.BlockSpec((TM_FFN, d), lambda i: (i, 0)),
                  pl.BlockSpec((1, 6, d), lambda i: (i // tiles_per_batch, 0, 0)),
                  _const_spec(wup.shape), _const_spec(conv_w.shape), _const_spec((1, ff2)),
                  _const_spec(wdn.shape), _const_spec((1, d)), _const_spec((1, d))],
        out_specs=pl.BlockSpec((TM_FFN, d), lambda i: (i, 0)),
        out_shape=jax.ShapeDtypeStruct((n, d), F32),
        scratch_shapes=[pltpu.VMEM((SUBLANES, ff2), F32), pltpu.VMEM((TM_FFN, ff2 // 2), BF16)],
        compiler_params=_cparams(("arbitrary",)),
        name="ffn",
    )(x1, mod3, wup, conv_w, conv_b.reshape(1, ff2), wdn, ln_g.reshape(1, d), ln_b.reshape(1, d))


def kernel(x, c, w_ada, b_ada, w_in, pe_ck, w_ck1, w_ck2, pe_cv, w_cv1, w_cv2, w_nsa_out,
           s5_a_re, s5_a_im, s5_b_re, s5_b_im, s5_c_re, s5_c_im, s5_d, s5_log_dt, w_s5_glu,
           w_o, ln1_g, ln1_b, w_up, conv_w, conv_b, w_down, ln2_g, ln2_b):
    batch, seq, d = x.shape
    depth = w_ada.shape[0]
    alpha = (2.0 * depth) ** 0.25
    n = batch * seq
    n_chunk16 = seq // CMP_STRIDE
    xf = x.reshape(n, d)
    for l in range(depth):
        mod3 = _ada(c, w_ada[l], b_ada[l]).reshape(batch, 6, d)

        zq, zc, zkv, zg, zs, zm = _inproj(xf, mod3, _build_w_all(w_in[l]), seq)

        wk1, pek, wk2 = _cmp_weights(pe_ck[l], w_ck1[l], w_ck2[l])
        wv1, pev, wv2 = _cmp_weights(pe_cv[l], w_cv1[l], w_cv2[l])
        ch = zc.reshape(2, batch, n_chunk16, CMP_STRIDE * KV_SLAB)
        kvc = _compress(ch, jnp.stack([wk1, wv1]), jnp.stack([pek, pev]), jnp.stack([wk2, wv2]))

        oa = _attention(zq, kvc, zkv, zg, batch, seq)

        mats = _s5_matrices(s5_a_re[l], s5_a_im[l], s5_b_re[l], s5_b_im[l], s5_c_re[l], s5_c_im[l],
                            s5_d[l], s5_log_dt[l])
        ys = _s5(zs, mats, batch, seq)

        x1 = _merge(oa, ys, zm, xf, mod3, _pad_nsa_out(w_nsa_out[l]), w_s5_glu[l].astype(BF16),
                    w_o[l].astype(BF16), ln1_g[l], ln1_b[l], seq, alpha)

        xf = _ffn(x1, mod3, w_up[l].astype(BF16), conv_w[l], conv_b[l], w_down[l].astype(BF16),
                  ln2_g[l], ln2_b[l], seq, alpha)
    return xf.reshape(batch, seq, d)
```

```python
import functools
import math

import jax
import jax.numpy as jnp
import ml_dtypes
import numpy as np
from jax import lax
from jax.experimental import pallas as pl
from jax.experimental.pallas import tpu as pltpu

F32 = jnp.float32
BF16 = jnp.bfloat16

D_MODEL = 1024
NSA_HEADS = 8
KV_GROUPS = 2
HPG = NSA_HEADS // KV_GROUPS
HEAD_DIM = 64
CMP_BLOCK = 32
CMP_STRIDE = 16
CMP_HIDDEN = 128
SEL_BLOCK = 64
SEL_TOPK = 16
WINDOW = 512
Q_BLOCK = 128
S5_GROUP = 16
S5_WIDTH = 512
S5_GROUPS = S5_WIDTH // S5_GROUP
S5_STATE = 64
D_FF = 2816
CONV_WIDTH = 3
LN_EPS = 1e-5
NEG_INF = -1e30
SEL_FORCE = 1e9

LANES = 128
SUBLANES = 8
VMEM_LIMIT = 56 * 1024 * 1024

TM_IN = 512
TM_MERGE = 512
TM_FFN = 512
FF_CHUNK = 256
FFN_PARTS = 2
SEL_TK = 512
WIN_KEYS = WINDOW + Q_BLOCK
WIN_BACK = WINDOW // Q_BLOCK
S5_CHUNK = 8
S5_LB = LANES // S5_GROUP
S5_NLB = S5_WIDTH // LANES

Q_COLS = NSA_HEADS * LANES
KV_SLAB = KV_GROUPS * HEAD_DIM
GATE_COLS = LANES
MERGE_COLS = 2 * D_MODEL
KV_OUT_COLS = 8 * KV_SLAB
F_HI, F_LO = 0, 3
ONE_LANE = 6
F_CHI, F_CLO, F_COFF = 7, 10, 13
LOG2E = 1.4426950408889634
MASK_BITS = 16


def _cparams(sem):
    return pltpu.CompilerParams(dimension_semantics=sem, vmem_limit_bytes=VMEM_LIMIT)


def _const_spec(shape):
    n = len(shape)
    return pl.BlockSpec(shape, lambda *_: (0,) * n)


def _layer_norm(x):
    mu = jnp.mean(x, axis=-1, keepdims=True)
    xc = x - mu
    var = jnp.mean(xc * xc, axis=-1, keepdims=True)
    return xc * lax.rsqrt(var + LN_EPS)


def _nt_dot(a, b):
    return lax.dot_general(a, b, (((1,), (1,)), ((), ())), preferred_element_type=F32)


def _group_lane0(g):
    return g * HEAD_DIM


def _ada_kernel(c_ref, w_ref, b_ref, o_ref):
    c = c_ref[...]
    a = c * jax.nn.sigmoid(c)
    o_ref[...] = jnp.dot(a.astype(BF16), w_ref[...].astype(BF16), preferred_element_type=F32) + b_ref[...]


def _ada(c, w_ada, b_ada):
    b, d = c.shape
    n = w_ada.shape[1]
    blk = 1024
    return pl.pallas_call(
        _ada_kernel,
        grid=(n // blk,),
        in_specs=[pl.BlockSpec((b, d), lambda j: (0, 0)),
                  pl.BlockSpec((d, blk), lambda j: (0, j)),
                  pl.BlockSpec((1, blk), lambda j: (0, j))],
        out_specs=pl.BlockSpec((b, blk), lambda j: (0, j)),
        out_shape=jax.ShapeDtypeStruct((b, n), F32),
        compiler_params=_cparams(("arbitrary",)),
        name="ada",
    )(c, w_ada, b_ada.reshape(1, n))


def _inproj_kernel(x_ref, mod_ref, w_ref, pf_ref, qf_ref, q_ref, c_ref, kv_ref, g_ref, s_ref, m_ref, zc_scr):
    hn = _layer_norm(x_ref[...])
    shift = mod_ref[0, 0:1, :]
    scale = mod_ref[0, 1:2, :]
    h = (hn * (1.0 + scale) + shift).astype(BF16)
    tm = x_ref.shape[0]

    def proj(col, width):
        return jnp.dot(h, w_ref[:, col:col + width], preferred_element_type=F32)

    col = 0
    q_ref[...] = (proj(col, Q_COLS) + qf_ref[...]).astype(q_ref.dtype)
    col += Q_COLS
    for kind in range(2):
        zc_scr[...] = proj(col, KV_SLAB)
        for tok in range(CMP_STRIDE):
            c_ref[kind, :, tok * KV_SLAB:(tok + 1) * KV_SLAB] = zc_scr[
                pl.ds(tok, tm // CMP_STRIDE, stride=CMP_STRIDE), :]
        col += KV_SLAB
    lane = lax.broadcasted_iota(jnp.int32, (x_ref.shape[0], LANES), 1)
    pf = pf_ref[...]
    for part in range(4):
        z = proj(col + part * KV_SLAB, KV_SLAB)
        for g in range(KV_GROUPS):
            own = (lane < HEAD_DIM) if g == 0 else (lane >= HEAD_DIM)
            slot = (part * KV_GROUPS + g) * KV_SLAB
            kv_ref[:, slot:slot + KV_SLAB] = jnp.where(
                own, z, pf[:, g * LANES:(g + 1) * LANES]).astype(kv_ref.dtype)
    col += 4 * KV_SLAB
    g_ref[...] = proj(col, GATE_COLS)
    col += GATE_COLS
    s_ref[...] = proj(col, S5_WIDTH)
    col += S5_WIDTH
    m_ref[...] = proj(col, MERGE_COLS)


def _log2e_terms():
    terms, rest = [], np.float64(LOG2E)
    for _ in range(3):
        t = np.float64(np.float32(rest).astype(ml_dtypes.bfloat16))
        terms.append(float(t))
        rest -= t
    return terms


def _key_position_features(seq):
    p = np.arange(seq)
    out = np.zeros((seq, KV_GROUPS, LANES), np.float32)
    for g in range(KV_GROUPS):
        l0 = _group_lane0(1 - g)
        for k in range(3):
            out[:, g, l0 + F_HI + k] = p // SEL_BLOCK
            out[:, g, l0 + F_LO + k] = p % SEL_BLOCK
        out[:, g, l0 + ONE_LANE] = 1.0
    return jnp.asarray(out.reshape(seq, KV_GROUPS * LANES))


def _query_feature_row():
    c = _log2e_terms()
    row = np.zeros((NSA_HEADS, LANES), np.float32)
    for hd in range(NSA_HEADS):
        g = hd // HPG
        slope = 2.0 ** -(hd + 1)
        l0 = _group_lane0(1 - g)
        for k in range(3):
            row[hd, l0 + F_HI + k] = SEL_BLOCK * slope * c[k]
            row[hd, l0 + F_LO + k] = slope * c[k]
            row[hd, l0 + F_CHI + k] = 16 * CMP_STRIDE * slope * c[k]
            row[hd, l0 + F_CLO + k] = CMP_STRIDE * slope * c[k]
            row[hd, l0 + F_COFF + k] = slope * c[k]
    return jnp.asarray(row.reshape(1, Q_COLS))


def _inproj(x2, mod3, w_all, seq):
    n, d = x2.shape
    tiles_per_batch = seq // TM_IN
    rows16 = TM_IN // CMP_STRIDE
    widths = (Q_COLS, KV_OUT_COLS, GATE_COLS, S5_WIDTH, MERGE_COLS)
    dtypes = (BF16, BF16, F32, F32, F32)
    pf = _key_position_features(seq)
    qf = _query_feature_row()
    row_spec = lambda w: pl.BlockSpec((TM_IN, w), lambda i: (i, 0))
    out_specs = [row_spec(Q_COLS),
                 pl.BlockSpec((2, rows16, CMP_STRIDE * KV_SLAB), lambda i: (0, i, 0))]
    out_specs += [row_spec(w) for w in widths[1:]]
    out_shape = [jax.ShapeDtypeStruct((n, Q_COLS), BF16),
                 jax.ShapeDtypeStruct((2, n // CMP_STRIDE, CMP_STRIDE * KV_SLAB), F32)]
    out_shape += [jax.ShapeDtypeStruct((n, w), dt) for w, dt in zip(widths[1:], dtypes[1:])]
    return pl.pallas_call(
        _inproj_kernel,
        grid=(n // TM_IN,),
        in_specs=[pl.BlockSpec((TM_IN, d), lambda i: (i, 0)),
                  pl.BlockSpec((1, 6, d), lambda i: (i // tiles_per_batch, 0, 0)),
                  _const_spec(w_all.shape),
                  pl.BlockSpec((TM_IN, KV_GROUPS * LANES), lambda i: (i % tiles_per_batch, 0)),
                  _const_spec(qf.shape)],
        out_specs=out_specs,
        out_shape=out_shape,
        scratch_shapes=[pltpu.VMEM((TM_IN, KV_SLAB), F32)],
        compiler_params=_cparams(("arbitrary",)),
        name="inproj",
    )(x2, mod3, w_all, pf, qf)


def _build_w_all(w_in):
    d = w_in.shape[0]
    cq = NSA_HEADS * HEAD_DIM
    ckv = 6 * KV_SLAB
    cg = 3 * NSA_HEADS
    zeros = jnp.zeros((d, HEAD_DIM), w_in.dtype)
    pieces = []
    for hd in range(NSA_HEADS):
        wq = w_in[:, hd * HEAD_DIM:(hd + 1) * HEAD_DIM] * (HEAD_DIM ** -0.5 * LOG2E)
        pieces += [wq, zeros] if hd < HPG else [zeros, wq]
    wq_pad = jnp.concatenate(pieces, axis=1)
    wkv = w_in[:, cq:cq + ckv]
    wg = jnp.pad(w_in[:, cq + ckv:cq + ckv + cg], ((0, 0), (0, GATE_COLS - cg)))
    rest = w_in[:, cq + ckv + cg:]
    return jnp.concatenate([wq_pad, wkv, wg, rest], axis=1).astype(BF16)


def _cmp_kernel(ch_ref, w1_ref, pe_ref, w2_ref, o_ref):
    ch = ch_ref[0, 0]
    a = jnp.dot((ch + pe_ref[0, 0]).astype(BF16), w1_ref[0, 0], preferred_element_type=F32)
    b = jnp.dot((ch + pe_ref[0, 1]).astype(BF16), w1_ref[0, 1], preferred_element_type=F32)
    n = a.shape[0]
    hsum = a + pltpu.roll(b, n - 1, axis=0)
    hact = hsum * jax.nn.sigmoid(hsum)
    o_ref[0, 0] = jnp.dot(hact.astype(BF16), w2_ref[0], preferred_element_type=F32).astype(o_ref.dtype)


def _cmp_weights(pe, w1, w2):
    half = CMP_BLOCK // 2
    halves = []
    pes = []
    for lo in (0, half):
        w = w1[lo:lo + half]
        z = jnp.zeros_like(w)
        w_g0 = jnp.concatenate([w, z], axis=-1)
        w_g1 = jnp.concatenate([z, w], axis=-1)
        halves.append(jnp.stack([w_g0, w_g1], axis=1).reshape(half * KV_SLAB, KV_GROUPS * CMP_HIDDEN))
        p = pe[lo:lo + half]
        pes.append(jnp.broadcast_to(p[:, None, :], (half, KV_GROUPS, HEAD_DIM)).reshape(1, half * KV_SLAB))
    z2 = jnp.zeros_like(w2)
    w2_blk = jnp.concatenate([jnp.concatenate([w2, z2], axis=1), jnp.concatenate([z2, w2], axis=1)], axis=0)
    return jnp.stack(halves).astype(BF16), jnp.stack(pes), w2_blk.astype(BF16)


def _compress(ch, w1s, pes, w2s):
    kinds, b, nchunk, width = ch.shape
    hid = w1s.shape[-1]
    return pl.pallas_call(
        _cmp_kernel,
        grid=(kinds, b),
        in_specs=[pl.BlockSpec((1, 1, nchunk, width), lambda k, i: (k, i, 0, 0)),
                  pl.BlockSpec((1, 2, width, hid), lambda k, i: (k, 0, 0, 0)),
                  pl.BlockSpec((1, 2, 1, width), lambda k, i: (k, 0, 0, 0)),
                  pl.BlockSpec((1, hid, KV_SLAB), lambda k, i: (k, 0, 0))],
        out_specs=pl.BlockSpec((1, 1, nchunk, KV_SLAB), lambda k, i: (k, i, 0, 0)),
        out_shape=jax.ShapeDtypeStruct((kinds, b, nchunk, KV_SLAB), BF16),
        compiler_params=_cparams(("arbitrary", "arbitrary")),
        name="compress",
    )(ch, w1s, pes, w2s)


def _softmax_parts(s):
    m = jnp.max(s, axis=-1, keepdims=True)
    e = jnp.exp2(s - m)
    return m, e, jnp.sum(e, axis=-1, keepdims=True)


def _attn_kernel(q_ref, kc_ref, vc_ref, ks0_ref, ks1_ref, vs0_ref, vs1_ref, kw0_ref, kw1_ref,
                 vw0_ref, vw1_ref, zg_ref,
                 ov_ref, cf_ref, hot_ref, tri_ref, wm_ref, gsel_ref, o_ref, v_scr, lhs_scr, m_scr, acc_scr,
                 part_scr, tile_smem,
                 *, n_cmp, n_sel):
    qb = pl.program_id(1)
    t0 = qb * Q_BLOCK
    ncp = kc_ref.shape[2]
    gates = jax.nn.sigmoid(zg_ref[...])
    g_hi = gates.astype(BF16)
    g_lo = (gates - g_hi.astype(F32)).astype(BF16)
    gates_b = jnp.dot(jnp.concatenate([g_hi, g_lo], axis=1), gsel_ref[...], preferred_element_type=F32)

    def gate(g, h, branch):
        c = (g * HPG + h) * 3 + branch
        return gates_b[:, c * LANES:(c + 1) * LANES]
    r_col = lax.broadcasted_iota(jnp.int32, (Q_BLOCK, 1), 0)
    lane = lax.broadcasted_iota(jnp.int32, (Q_BLOCK, LANES), 1)
    tri = tri_ref[...]
    wmask = wm_ref[jnp.minimum(qb, WIN_BACK)]
    hrows = [slice(h * Q_BLOCK, (h + 1) * Q_BLOCK) for h in range(HPG)]

    t0a = pl.multiple_of(t0, Q_BLOCK)
    w0 = pl.multiple_of(jnp.maximum(qb - WIN_BACK, 0) * Q_BLOCK, Q_BLOCK)

    scores = []
    for g in range(KV_GROUPS):
        ks_ref = (ks0_ref, ks1_ref)[g]
        kw_ref = (kw0_ref, kw1_ref)[g]
        own_c = lax.broadcasted_iota(jnp.int32, (ncp, LANES), 1)
        own_c = (own_c < HEAD_DIM) if g == 0 else (own_c >= HEAD_DIM)
        lhs_pos = jnp.concatenate(
            [q_ref[:, (g * HPG + h) * LANES:(g * HPG + h + 1) * LANES] for h in range(HPG)], axis=0)
        kc_aug = jnp.where(own_c, kc_ref[0, 0], cf_ref[g])
        scores.append([lhs_pos, _nt_dot(lhs_pos, kc_aug)])
    for g in range(KV_GROUPS):
        ks_ref = (ks0_ref, ks1_ref)[g]
        kw_ref = (kw0_ref, kw1_ref)[g]
        lhs_pos = scores[g][0]
        scores[g].append(_nt_dot(lhs_pos, ks_ref[pl.ds(t0a, Q_BLOCK), :]))
        scores[g].append(_nt_dot(lhs_pos, kw_ref[pl.ds(w0, WIN_KEYS), :]))

    block_used = []
    cmp_out = []
    for g in range(KV_GROUPS):
        sc = scores[g][1]

        n_i = lax.broadcasted_iota(jnp.int32, (Q_BLOCK, ncp), 1)
        r_i = lax.broadcasted_iota(jnp.int32, (Q_BLOCK, ncp), 0)
        valid_c = jnp.logical_and((n_i * CMP_STRIDE + (CMP_BLOCK - 1)) <= (t0 + r_i), n_i < n_cmp)
        row_any = ((t0 + r_col) >= (CMP_BLOCK - 1)).astype(F32)
        sc = jnp.where(valid_c[None], sc.reshape(HPG, Q_BLOCK, ncp), NEG_INF)
        _, e, l = _softmax_parts(sc)
        p_c = (e * (row_any[None] / l)).astype(BF16)
        o_c = jnp.dot(p_c.reshape(HPG * Q_BLOCK, ncp), vc_ref[0, 0], preferred_element_type=F32)
        imp = jnp.dot(jnp.concatenate([p_c[h] for h in range(HPG)], axis=1), ov_ref[...],
                      preferred_element_type=F32)
        cmp_out.append((o_c, imp))

    for g in range(KV_GROUPS):
        lhs_pos = scores[g][0]
        imp = cmp_out[g][1]

        imp_t = imp.T[:n_sel]
        j_i = lax.broadcasted_iota(jnp.int32, (n_sel, Q_BLOCK), 0)
        cur = lax.shift_right_logical(t0 + lax.broadcasted_iota(jnp.int32, (n_sel, Q_BLOCK), 1),
                                      int(math.log2(SEL_BLOCK)))
        forced = jnp.logical_or(j_i == 0, jnp.logical_or(j_i == cur, j_i == cur - 1))
        valid = j_i <= cur
        v = jnp.where(forced, SEL_FORCE, jnp.where(valid, imp_t, -SEL_FORCE))
        v_scr[g] = v
        n_chunk = n_sel // SUBLANES
        chunks = [v[k * SUBLANES:(k + 1) * SUBLANES] for k in range(n_chunk)]
        ranks = [jnp.zeros((SUBLANES, Q_BLOCK), jnp.int32) for _ in range(n_chunk)]
        sub_i = lax.broadcasted_iota(jnp.int32, (SUBLANES, Q_BLOCK), 0)
        for i in range(n_sel):
            vi = v_scr[g, i:i + 1, :]
            ki = i // SUBLANES
            for k in range(n_chunk):
                if k > ki:
                    beats = jnp.where(vi >= chunks[k], 1, 0)
                elif k < ki:
                    beats = jnp.where(vi > chunks[k], 1, 0)
                else:
                    beats = jnp.where(sub_i > (i - ki * SUBLANES),
                                      jnp.where(vi >= chunks[k], 1, 0),
                                      jnp.where(vi > chunks[k], 1, 0))
                ranks[k] = ranks[k] + beats
        rank = jnp.concatenate(ranks, axis=0)
        chosen = jnp.logical_and(jnp.logical_and(rank < SEL_TOPK, valid), j_i < 2 * qb)
        bias_t = jnp.where(chosen, 0.0, NEG_INF)
        bias_t = jnp.concatenate([bias_t, jnp.full((LANES - n_sel, Q_BLOCK), NEG_INF, F32)], axis=0)
        bias_f = bias_t.T
        block_used.append(jnp.max(bias_f, axis=0, keepdims=True))
        bias = bias_f.astype(BF16)
        lhs_scr[g] = jnp.concatenate([lhs_pos, jnp.concatenate([bias] * HPG, axis=0)], axis=1)

    for g in range(KV_GROUPS):
        _, _, sd, sw = scores[g]
        o_c = cmp_out[g][0]
        vs_ref = (vs0_ref, vs1_ref)[g]
        vw_ref = (vw0_ref, vw1_ref)[g]
        f0 = _group_lane0(1 - g)

        vd = vs_ref[pl.ds(t0a, Q_BLOCK), :]
        sd = (sd.reshape(HPG, Q_BLOCK, Q_BLOCK) + tri[None]).reshape(HPG * Q_BLOCK, Q_BLOCK)
        m = jnp.max(sd, axis=-1, keepdims=True)
        m_scr[g] = jnp.broadcast_to(m, m_scr.shape[1:])
        acc_scr[g] = jnp.dot(jnp.exp2(sd - m).astype(BF16), vd, preferred_element_type=F32)

        v_w = vw_ref[pl.ds(w0, WIN_KEYS), :]
        sw = (sw.reshape(HPG, Q_BLOCK, WIN_KEYS) + wmask[None]).reshape(HPG * Q_BLOCK, WIN_KEYS)
        e = jnp.exp2(sw - jnp.max(sw, axis=-1, keepdims=True))
        o_w = jnp.dot(e.astype(BF16), v_w, preferred_element_type=F32)
        o_w = o_w * (1.0 / o_w[:, f0 + ONE_LANE:f0 + ONE_LANE + 1])

        for h in range(HPG):
            part_scr[g, hrows[h]] = gate(g, h, 0) * o_c[hrows[h]] + gate(g, h, 2) * o_w[hrows[h]]

    n_key_tiles = hot_ref.shape[0] // Q_BLOCK - 1
    per_step = SEL_TK // Q_BLOCK
    used = jnp.maximum(block_used[0], block_used[1])
    used = jnp.maximum(used, pltpu.roll(used, LANES - 1, axis=1))
    lane1 = lax.broadcasted_iota(jnp.int32, (1, LANES), 1)
    tile_of_lane = lax.shift_right_logical(lane1, 1)
    bit = jnp.where(jnp.logical_and(used > -1.0, (lane1 & 1) == 0),
                    jnp.left_shift(1, tile_of_lane & (MASK_BITS - 1)), 0).astype(F32)
    word = lax.shift_right_logical(tile_of_lane, int(math.log2(MASK_BITS)))
    masks = [jnp.sum(jnp.where(word == w, bit, 0.0), axis=1, keepdims=True)[0, 0].astype(jnp.int32)
             for w in range((n_key_tiles + MASK_BITS - 1) // MASK_BITS)]
    count = jnp.int32(0)
    for j in range(n_key_tiles):
        tile_smem[count] = j
        count = count + (lax.shift_right_logical(masks[j // MASK_BITS], j % MASK_BITS) & 1)
    for k in range(per_step - 1):
        tile_smem[count + k] = n_key_tiles

    def tile_offsets(i, k):
        j = tile_smem[i * per_step + k]
        hot_off = pl.multiple_of(j * Q_BLOCK, Q_BLOCK)
        kv_off = pl.multiple_of(jnp.minimum(j, n_key_tiles - 1) * Q_BLOCK, Q_BLOCK)
        return kv_off, hot_off

    def sel_body(i, carry):
        offs = [tile_offsets(i, k) for k in range(per_step)]
        hot = jnp.concatenate([hot_ref[pl.ds(ho, Q_BLOCK), :] for _, ho in offs], axis=0)
        s_g = []
        for g in range(KV_GROUPS):
            ks_ref = (ks0_ref, ks1_ref)[g]
            k_t = jnp.concatenate([ks_ref[pl.ds(ko, Q_BLOCK), :] for ko, _ in offs], axis=0)
            s_g.append(_nt_dot(lhs_scr[g], jnp.concatenate([k_t, hot], axis=1)))
        for g in range(KV_GROUPS):
            vs_ref = (vs0_ref, vs1_ref)[g]
            v_t = jnp.concatenate([vs_ref[pl.ds(ko, Q_BLOCK), :] for ko, _ in offs], axis=0)
            s = s_g[g]
            m_old = m_scr[g]
            m_new = jnp.maximum(m_old, jnp.max(s, axis=-1, keepdims=True))
            p = jnp.exp2(s - jnp.tile(m_new, (1, SEL_TK // LANES)))
            alpha = jnp.exp2(m_old - m_new)
            m_scr[g] = m_new
            pv = jnp.dot(p.astype(BF16), v_t, preferred_element_type=F32)
            acc_scr[g] = alpha * acc_scr[g] + pv
        return carry

    lax.fori_loop(0, (count + per_step - 1) // per_step, sel_body, 0)

    o_s = []
    for g in range(KV_GROUPS):
        one = _group_lane0(1 - g) + ONE_LANE
        acc = acc_scr[g]
        o_s.append(acc * (1.0 / acc[:, one:one + 1]))
    for h in range(HPG):
        outs = [part_scr[g, hrows[h]] + gate(g, h, 1) * o_s[g][hrows[h]] for g in range(KV_GROUPS)]
        o_ref[:, h * LANES:(h + 1) * LANES] = jnp.where(lane < HEAD_DIM, outs[0], outs[1]).astype(o_ref.dtype)


def _attention_tables(seq, ncp, n_cmp, n_sel):
    cstart = np.arange(ncp) * CMP_STRIDE
    sstart = np.arange(n_sel) * SEL_BLOCK
    overlap = ((cstart[:, None] < sstart[None, :] + SEL_BLOCK)
               & (cstart[:, None] + CMP_BLOCK > sstart[None, :])
               & (np.arange(ncp)[:, None] < n_cmp)).astype(np.float32)
    overlap = np.pad(overlap, ((0, 0), (0, LANES - n_sel)))
    ov4 = np.tile(overlap, (HPG, 1))
    n = np.arange(ncp)
    cfeat = np.zeros((KV_GROUPS, ncp, LANES), np.float32)
    for g in range(KV_GROUPS):
        l0 = _group_lane0(1 - g)
        for k in range(3):
            cfeat[g, :, l0 + F_CHI + k] = n // 16
            cfeat[g, :, l0 + F_CLO + k] = n % 16
            cfeat[g, :, l0 + F_COFF + k] = CMP_BLOCK - 1
    blk = np.concatenate([np.arange(seq) // SEL_BLOCK, np.full(Q_BLOCK, n_sel)])
    hot = (blk[:, None] == np.arange(LANES)[None, :]).astype(np.float32)
    r = np.arange(Q_BLOCK)[:, None]
    tri = np.where(np.arange(Q_BLOCK)[None, :] <= r, 0.0, NEG_INF).astype(np.float32)
    c = np.arange(WIN_KEYS)[None, :]
    wm = []
    for qb in range(WIN_BACK + 1):
        dist = (qb * Q_BLOCK + r) - c if qb < WIN_BACK else (WINDOW + r) - c
        wm.append(np.where((dist >= 0) & (dist < WINDOW), 0.0, NEG_INF))
    wm = np.stack(wm).astype(np.float32)
    n_gate = 3 * NSA_HEADS
    gsel = np.zeros((2 * LANES, n_gate, LANES), np.float32)
    for c in range(n_gate):
        gsel[c, c, :] = 1.0
        gsel[LANES + c, c, :] = 1.0
    gsel = jnp.asarray(gsel.reshape(2 * LANES, n_gate * LANES), BF16)
    return (jnp.asarray(ov4, BF16), jnp.asarray(cfeat, BF16), jnp.asarray(hot, BF16), gsel,
            jnp.asarray(tri), jnp.asarray(wm))


def _attention(zq, kvc, zkv, zg, batch, seq):
    n = zq.shape[0]
    nqb = seq // Q_BLOCK
    ncp = kvc.shape[2]
    n_cmp = seq // CMP_STRIDE - 1
    n_sel = seq // SEL_BLOCK
    assert n_sel <= HEAD_DIM and n_sel % SUBLANES == 0 and seq % SEL_TK == 0 and nqb > WIN_BACK
    assert ncp // 16 <= 256 and seq // SEL_BLOCK <= 256
    ov4, cfeat, hot, gsel, tri, wm = _attention_tables(seq, ncp, n_cmp, n_sel)
    out_cols = HPG * LANES

    kernel = functools.partial(_attn_kernel, n_cmp=n_cmp, n_sel=n_sel)
    kv_spec = lambda idx: pl.BlockSpec((seq, KV_SLAB), lambda b, i, idx=idx: (b, idx))
    return pl.pallas_call(
        kernel,
        grid=(batch, nqb),
        in_specs=[pl.BlockSpec((Q_BLOCK, Q_COLS), lambda b, i: (b * nqb + i, 0)),
                  pl.BlockSpec((1, 1, ncp, KV_SLAB), lambda b, i: (0, b, 0, 0)),
                  pl.BlockSpec((1, 1, ncp, KV_SLAB), lambda b, i: (1, b, 0, 0)),
                  *[kv_spec(k) for k in range(8)],
                  pl.BlockSpec((Q_BLOCK, GATE_COLS), lambda b, i: (b * nqb + i, 0)),
                  _const_spec(ov4.shape), _const_spec(cfeat.shape), _const_spec(hot.shape),
                  _const_spec(tri.shape), _const_spec(wm.shape), _const_spec(gsel.shape)],
        out_specs=pl.BlockSpec((Q_BLOCK, out_cols), lambda b, i: (b * nqb + i, 0)),
        out_shape=jax.ShapeDtypeStruct((n, out_cols), BF16),
        scratch_shapes=[pltpu.VMEM((KV_GROUPS, n_sel, Q_BLOCK), F32),
                        pltpu.VMEM((KV_GROUPS, HPG * Q_BLOCK, 2 * LANES), BF16),
                        pltpu.VMEM((KV_GROUPS, HPG * Q_BLOCK, LANES), F32),
                        pltpu.VMEM((KV_GROUPS, HPG * Q_BLOCK, LANES), F32),
                        pltpu.VMEM((KV_GROUPS, HPG * Q_BLOCK, LANES), F32),
                        pltpu.SMEM((seq // Q_BLOCK + SEL_TK // Q_BLOCK,), jnp.int32)],
        compiler_params=_cparams(("arbitrary", "arbitrary")),
        name="nsa_attn",
    )(zq, kvc, kvc, *([zkv] * 8), zg, ov4, cfeat, hot, tri, wm, gsel)


def _s5_kernel(u_ref, lagk_ref, mre_ref, mim_ref, cre_ref, cim_ref, lre_ref, lim_ref, d_ref, y_ref,
               ere, eim, xre, xim, tp_scr, *, n_chunks):
    def tok(t):
        return pl.ds(t, n_chunks, stride=S5_CHUNK)

    @pl.when(pl.program_id(1) == 0)
    def _():
        tp_scr[...] = jnp.zeros(tp_scr.shape, tp_scr.dtype)
        for s in range(S5_CHUNK):
            for t in range(s, S5_CHUNK):
                tp_scr[s * LANES:(s + 1) * LANES, t * LANES:(t + 1) * LANES] = lagk_ref[0, t - s]

    u = jnp.concatenate([u_ref[tok(t), :] for t in range(S5_CHUNK)], axis=1).astype(BF16)
    y = jnp.dot(u, tp_scr[...], preferred_element_type=F32)
    ere[...] = jnp.dot(u, mre_ref[0, 0], preferred_element_type=F32)
    eim[...] = jnp.dot(u, mim_ref[0, 0], preferred_element_type=F32)
    lr = lre_ref[0]
    li = lim_ref[0]

    def body(c, carry):
        xr, xi = carry
        row = pl.ds(c, 1)
        xre[row, :] = xr
        xim[row, :] = xi
        er = ere[row, :]
        ei = eim[row, :]
        return lr * xr - li * xi + er, lr * xi + li * xr + ei

    zero = jnp.zeros((1, ere.shape[1]), F32)
    lax.fori_loop(0, n_chunks, body, (zero, zero), unroll=8)
    y = y + jnp.dot(xre[...].astype(BF16), cre_ref[0, 0], preferred_element_type=F32)
    y = y + jnp.dot(xim[...].astype(BF16), cim_ref[0, 0], preferred_element_type=F32)
    d = d_ref[0]
    for t in range(S5_CHUNK):
        y_ref[tok(t), :] = y[:, t * LANES:(t + 1) * LANES] + d * u_ref[tok(t), :]


def _s5_matrices(a_re, a_im, b_re, b_im, c_re, c_im, d_skip, log_dt):
    t = S5_CHUNK
    hp = lax.Precision.HIGHEST
    dt = jnp.exp(log_dt)[:, None]
    lam_re = jnp.minimum(a_re, -1e-4)
    lam_im = a_im
    mag = jnp.exp(lam_re * dt)
    ang = lam_im * dt
    lb_re = mag * jnp.cos(ang)
    lb_im = mag * jnp.sin(ang)
    den = lam_re * lam_re + lam_im * lam_im
    nr = lb_re - 1.0
    coef_re = (nr * lam_re + lb_im * lam_im) / den
    coef_im = (lb_im * lam_re - nr * lam_im) / den
    bb_re = coef_re[..., None] * b_re - coef_im[..., None] * b_im
    bb_im = coef_re[..., None] * b_im + coef_im[..., None] * b_re
    j = jnp.arange(t + 1, dtype=F32)[:, None, None]
    pmag = jnp.exp(j * (lam_re * dt)[None])
    pw_re = pmag * jnp.cos(j * ang[None])
    pw_im = pmag * jnp.sin(j * ang[None])
    cl_re = c_re[None] * pw_re[:, :, None, :] - c_im[None] * pw_im[:, :, None, :]
    cl_im = c_re[None] * pw_im[:, :, None, :] + c_im[None] * pw_re[:, :, None, :]
    kern = (jnp.einsum('jgop,gpi->gjoi', cl_re[:t], bb_re, precision=hp)
            - jnp.einsum('jgop,gpi->gjoi', cl_im[:t], bb_im, precision=hp))
    eye = jnp.eye(S5_LB, dtype=F32)
    nlb = S5_GROUPS // S5_LB
    lagk = jnp.einsum('qgjoi,gh->qjgiho', kern.reshape(nlb, S5_LB, t, S5_GROUP, S5_GROUP), eye,
                      precision=hp).reshape(nlb, t, LANES, LANES)
    rv_re = pw_re[t - 1 - np.arange(t)]
    rv_im = pw_im[t - 1 - np.arange(t)]
    m_re = rv_re[..., None] * bb_re[None] - rv_im[..., None] * bb_im[None]
    m_im = rv_re[..., None] * bb_im[None] + rv_im[..., None] * bb_re[None]

    def lay_m(m):
        m = m.reshape(2, t, nlb, S5_LB, S5_STATE, S5_GROUP)
        return jnp.einsum('csqgpi,gh->cqsgihp', m, eye, precision=hp).reshape(
            2, nlb, t * LANES, S5_LB * S5_STATE)

    def lay_c(c):
        c = c.reshape(2, t, nlb, S5_LB, S5_GROUP, S5_STATE)
        return jnp.einsum('ctqgop,gh->cqgptho', c, eye, precision=hp).reshape(
            2, nlb, S5_LB * S5_STATE, t * LANES)

    lam_t_re = pw_re[t].reshape(nlb, 1, S5_LB * S5_STATE)
    lam_t_im = pw_im[t].reshape(nlb, 1, S5_LB * S5_STATE)
    d_row = d_skip.reshape(nlb, 1, LANES)
    m_ops = lay_m(jnp.stack([m_re, m_im])).astype(BF16)
    c_ops = lay_c(jnp.stack([cl_re[1:], -cl_im[1:]])).astype(BF16)
    return lagk.astype(BF16), m_ops, c_ops, lam_t_re, lam_t_im, d_row


def _s5(zs, mats, batch, seq):
    n_chunks = seq // S5_CHUNK
    lagk, m_ops, c_ops, l_re, l_im, d_row = mats
    nlb = lagk.shape[0]
    wcol = S5_CHUNK * LANES
    wst = S5_LB * S5_STATE
    kernel = functools.partial(_s5_kernel, n_chunks=n_chunks)
    p3 = lambda r, c: pl.BlockSpec((1, r, c), lambda q, b: (q, 0, 0))
    part = lambda k, r, c: pl.BlockSpec((1, 1, r, c), lambda q, b, k=k: (k, q, 0, 0))
    return pl.pallas_call(
        kernel,
        grid=(nlb, batch),
        in_specs=[pl.BlockSpec((seq, LANES), lambda q, b: (b, q)),
                  pl.BlockSpec((1, S5_CHUNK, LANES, LANES), lambda q, b: (q, 0, 0, 0)),
                  part(0, wcol, wst), part(1, wcol, wst), part(0, wst, wcol), part(1, wst, wcol),
                  p3(1, wst), p3(1, wst), p3(1, LANES)],
        out_specs=pl.BlockSpec((seq, LANES), lambda q, b: (b, q)),
        out_shape=jax.ShapeDtypeStruct(zs.shape, F32),
        scratch_shapes=[pltpu.VMEM((n_chunks, wst), F32) for _ in range(4)]
        + [pltpu.VMEM((wcol, wcol), BF16)],
        compiler_params=_cparams(("arbitrary", "arbitrary")),
        name="s5",
    )(zs, lagk, m_ops, m_ops, c_ops, c_ops, l_re, l_im, d_row)


def _merge_kernel(oa_ref, ys_ref, zm_ref, x_ref, mod_ref, wn_ref, wg_ref, wo_ref, g_ref, b_ref, o_ref,
                  *, alpha):
    d = x_ref.shape[-1]
    y_a = jnp.dot(oa_ref[...], wn_ref[...], preferred_element_type=F32)
    gl = jax.nn.gelu(ys_ref[...], approximate=True)
    zz = jnp.dot(gl.astype(BF16), wg_ref[...], preferred_element_type=F32)
    y_b = zz[:, :d] * jax.nn.sigmoid(zz[:, d:])
    zm = zm_ref[...]
    mix_in = jax.nn.sigmoid(zm[:, :d]) * y_a + jax.nn.sigmoid(zm[:, d:]) * y_b
    mix = jnp.dot(mix_in.astype(BF16), wo_ref[...], preferred_element_type=F32)
    gate = mod_ref[0, 2:3, :]
    r = alpha * x_ref[...] + gate * mix
    o_ref[...] = _layer_norm(r) * g_ref[...] + b_ref[...]


def _merge(oa, ys, zm, x2, mod3, wn_pad, wg, wo, ln_g, ln_b, seq, alpha):
    n, d = x2.shape
    tiles_per_batch = seq // TM_MERGE
    row = lambda w: pl.BlockSpec((TM_MERGE, w), lambda i: (i, 0))
    return pl.pallas_call(
        functools.partial(_merge_kernel, alpha=alpha),
        grid=(n // TM_MERGE,),
        in_specs=[row(oa.shape[1]), row(ys.shape[1]), row(zm.shape[1]), row(d),
                  pl.BlockSpec((1, 6, d), lambda i: (i // tiles_per_batch, 0, 0)),
                  _const_spec(wn_pad.shape), _const_spec(wg.shape), _const_spec(wo.shape),
                  _const_spec((1, d)), _const_spec((1, d))],
        out_specs=row(d),
        out_shape=jax.ShapeDtypeStruct((n, d), F32),
        compiler_params=_cparams(("arbitrary",)),
        name="merge",
    )(oa, ys, zm, x2, mod3, wn_pad, wg, wo, ln_g.reshape(1, d), ln_b.reshape(1, d))


def _pad_nsa_out(w):
    rows = []
    for h in range(HPG):
        for g in range(KV_GROUPS):
            hd = g * HPG + h
            rows.append(w[hd * HEAD_DIM:(hd + 1) * HEAD_DIM])
    return jnp.concatenate(rows, axis=0).astype(BF16)


def _ffn_kernel(x_ref, mod_ref, wup_ref, cw_ref, cb_ref, wdn_ref, g_ref, b_ref, o_ref, tail_ref, act_scr,
                *, alpha, tiles_per_batch):
    i = pl.program_id(0)
    tm = x_ref.shape[0]

    @pl.when(i % tiles_per_batch == 0)
    def _():
        tail_ref[...] = jnp.zeros(tail_ref.shape, F32)

    shift = mod_ref[0, 3:4, :]
    scale = mod_ref[0, 4:5, :]
    gate = mod_ref[0, 5:6, :]
    row = lax.broadcasted_iota(jnp.int32, (SUBLANES, FF_CHUNK), 0)

    def shift_rows(a, prev_rows):
        rolled = pltpu.roll(a, 1, axis=0)
        head = jnp.where(row == 0, prev_rows[SUBLANES - 1:SUBLANES], rolled[:SUBLANES])
        return jnp.concatenate([head, rolled[SUBLANES:]], axis=0)

    part_rows = tm // FFN_PARTS
    for part in range(FFN_PARTS):
        rows = slice(part * part_rows, (part + 1) * part_rows)
        x = x_ref[rows, :]
        h2 = (_layer_norm(x) * (1.0 + scale) + shift).astype(BF16)

        def conv_cols(c0):
            cols = slice(c0, c0 + FF_CHUNK)
            up = jnp.dot(h2, wup_ref[:, cols], preferred_element_type=F32)
            tail = tail_ref[:, cols]
            tail_ref[:, cols] = up[part_rows - SUBLANES:part_rows]
            w = cw_ref[:, cols]
            inner = shift_rows(w[0:1] * up, w[0:1] * tail) + w[1:2] * up
            prev_inner = pltpu.roll(w[0:1] * tail, 1, axis=0) + w[1:2] * tail
            return shift_rows(inner, prev_inner) + w[2:3] * up + cb_ref[:, cols]

        for k in range(D_FF // FF_CHUNK):
            val = conv_cols(k * FF_CHUNK)
            gte = conv_cols(D_FF + k * FF_CHUNK)
            act_scr[rows, k * FF_CHUNK:(k + 1) * FF_CHUNK] = (gte * jax.nn.sigmoid(gte) * val).astype(BF16)
        ff = jnp.dot(act_scr[rows, :], wdn_ref[...], preferred_element_type=F32)
        r = alpha * x + gate * ff
        o_ref[rows, :] = _layer_norm(r) * g_ref[...] + b_ref[...]


def _ffn(x1, mod3, wup, conv_w, conv_b, wdn, ln_g, ln_b, seq, alpha):
    n, d = x1.shape
    tiles_per_batch = seq // TM_FFN
    ff2 = wup.shape[1]
    return pl.pallas_call(
        functools.partial(_ffn_kernel, alpha=alpha, tiles_per_batch=tiles_per_batch),
        grid=(n // TM_FFN,),
        in_specs=[pl.BlockSpec((TM_FFN, d), lambda i: (i, 0)),
                  pl.BlockSpec((1, 6, d), lambda i: (i // tiles_per_batch, 0, 0)),
                  _const_spec(wup.shape), _const_spec(conv_w.shape), _const_spec((1, ff2)),
                  _const_spec(wdn.shape), _const_spec((1, d)), _const_spec((1, d))],
        out_specs=pl.BlockSpec((TM_FFN, d), lambda i: (i, 0)),
        out_shape=jax.ShapeDtypeStruct((n, d), F32),
        scratch_shapes=[pltpu.VMEM((SUBLANES, ff2), F32), pltpu.VMEM((TM_FFN, ff2 // 2), BF16)],
        compiler_params=_cparams(("arbitrary",)),
        name="ffn",
    )(x1, mod3, wup, conv_w, conv_b.reshape(1, ff2), wdn, ln_g.reshape(1, d), ln_b.reshape(1, d))


def kernel(x, c, w_ada, b_ada, w_in, pe_ck, w_ck1, w_ck2, pe_cv, w_cv1, w_cv2, w_nsa_out,
           s5_a_re, s5_a_im, s5_b_re, s5_b_im, s5_c_re, s5_c_im, s5_d, s5_log_dt, w_s5_glu,
           w_o, ln1_g, ln1_b, w_up, conv_w, conv_b, w_down, ln2_g, ln2_b):
    batch, seq, d = x.shape
    depth = w_ada.shape[0]
    alpha = (2.0 * depth) ** 0.25
    n = batch * seq
    n_chunk16 = seq // CMP_STRIDE
    xf = x.reshape(n, d)
    for l in range(depth):
        mod3 = _ada(c, w_ada[l], b_ada[l]).reshape(batch, 6, d)

        zq, zc, zkv, zg, zs, zm = _inproj(xf, mod3, _build_w_all(w_in[l]), seq)

        wk1, pek, wk2 = _cmp_weights(pe_ck[l], w_ck1[l], w_ck2[l])
        wv1, pev, wv2 = _cmp_weights(pe_cv[l], w_cv1[l], w_cv2[l])
        ch = zc.reshape(2, batch, n_chunk16, CMP_STRIDE * KV_SLAB)
        kvc = _compress(ch, jnp.stack([wk1, wv1]), jnp.stack([pek, pev]), jnp.stack([wk2, wv2]))

        oa = _attention(zq, kvc, zkv, zg, batch, seq)

        mats = _s5_matrices(s5_a_re[l], s5_a_im[l], s5_b_re[l], s5_b_im[l], s5_c_re[l], s5_c_im[l],
                            s5_d[l], s5_log_dt[l])
        ys = _s5(zs, mats, batch, seq)

        x1 = _merge(oa, ys, zm, xf, mod3, _pad_nsa_out(w_nsa_out[l]), w_s5_glu[l].astype(BF16),
                    w_o[l].astype(BF16), ln1_g[l], ln1_b[l], seq, alpha)

        xf = _ffn(x1, mod3, w_up[l].astype(BF16), conv_w[l], conv_b[l], w_down[l].astype(BF16),
                  ln2_g[l], ln2_b[l], seq, alpha)
    return xf.reshape(batch, seq, d)
```

```python
import functools
import math

import jax
import jax.numpy as jnp
import ml_dtypes
import numpy as np
from jax import lax
from jax.experimental import pallas as pl
from jax.experimental.pallas import tpu as pltpu

F32 = jnp.float32
BF16 = jnp.bfloat16

D_MODEL = 1024
NSA_HEADS = 8
KV_GROUPS = 2
HPG = NSA_HEADS // KV_GROUPS
HEAD_DIM = 64
CMP_BLOCK = 32
CMP_STRIDE = 16
CMP_HIDDEN = 128
SEL_BLOCK = 64
SEL_TOPK = 16
WINDOW = 512
Q_BLOCK = 128
S5_GROUP = 16
S5_WIDTH = 512
S5_GROUPS = S5_WIDTH // S5_GROUP
S5_STATE = 64
D_FF = 2816
CONV_WIDTH = 3
LN_EPS = 1e-5
NEG_INF = -1e30
SEL_FORCE = 1e9

LANES = 128
SUBLANES = 8
VMEM_LIMIT = 56 * 1024 * 1024

TM_IN = 512
TM_MERGE = 512
TM_FFN = 512
FF_CHUNK = 256
FFN_PARTS = 2
SEL_TK = 512
WIN_KEYS = WINDOW + Q_BLOCK
WIN_BACK = WINDOW // Q_BLOCK
S5_CHUNK = 8
S5_LB = LANES // S5_GROUP
ADA_COLS = 1024

Q_COLS = NSA_HEADS * LANES
KV_SLAB = KV_GROUPS * HEAD_DIM
GATE_COLS = LANES
MERGE_COLS = 2 * D_MODEL
KV_OUT_COLS = 8 * KV_SLAB
F_HI, F_LO = 0, 3
ONE_LANE = 6
F_CHI, F_CLO, F_COFF = 7, 10, 13
CMP_RADIX = 16
LOG2E = 1.4426950408889634
MASK_BITS = 16


def _cparams(sem):
    return pltpu.CompilerParams(dimension_semantics=sem, vmem_limit_bytes=VMEM_LIMIT)


def _const_spec(shape):
    n = len(shape)
    return pl.BlockSpec(shape, lambda *_: (0,) * n)


def _layer_norm(x):
    mu = jnp.mean(x, axis=-1, keepdims=True)
    xc = x - mu
    var = jnp.mean(xc * xc, axis=-1, keepdims=True)
    return xc * lax.rsqrt(var + LN_EPS)


def _nt_dot(a, b):
    return lax.dot_general(a, b, (((1,), (1,)), ((), ())), preferred_element_type=F32)


def _group_lane0(g):
    return g * HEAD_DIM


def _ada_kernel(c_ref, w_ref, b_ref, o_ref):
    c = c_ref[...]
    a = c * jax.nn.sigmoid(c)
    o_ref[...] = jnp.dot(a.astype(BF16), w_ref[...].astype(BF16), preferred_element_type=F32) + b_ref[...]


def _ada(c, w_ada, b_ada):
    b, d = c.shape
    n = w_ada.shape[1]
    blk = ADA_COLS
    return pl.pallas_call(
        _ada_kernel,
        grid=(n // blk,),
        in_specs=[pl.BlockSpec((b, d), lambda j: (0, 0)),
                  pl.BlockSpec((d, blk), lambda j: (0, j)),
                  pl.BlockSpec((1, blk), lambda j: (0, j))],
        out_specs=pl.BlockSpec((b, blk), lambda j: (0, j)),
        out_shape=jax.ShapeDtypeStruct((b, n), F32),
        compiler_params=_cparams(("arbitrary",)),
        name="ada",
    )(c, w_ada, b_ada.reshape(1, n))


def _inproj_kernel(x_ref, mod_ref, w_ref, pf_ref, qf_ref, q_ref, c_ref, kv_ref, g_ref, s_ref, m_ref, zc_scr):
    hn = _layer_norm(x_ref[...])
    shift = mod_ref[0, 0:1, :]
    scale = mod_ref[0, 1:2, :]
    h = (hn * (1.0 + scale) + shift).astype(BF16)
    tm = x_ref.shape[0]

    def proj(col, width):
        return jnp.dot(h, w_ref[:, col:col + width], preferred_element_type=F32)

    col = 0
    q_ref[...] = (proj(col, Q_COLS) + qf_ref[...]).astype(q_ref.dtype)
    col += Q_COLS
    for kind in range(2):
        zc_scr[...] = proj(col, KV_SLAB)
        for tok in range(CMP_STRIDE):
            c_ref[kind, :, tok * KV_SLAB:(tok + 1) * KV_SLAB] = zc_scr[
                pl.ds(tok, tm // CMP_STRIDE, stride=CMP_STRIDE), :]
        col += KV_SLAB
    lane = lax.broadcasted_iota(jnp.int32, (x_ref.shape[0], LANES), 1)
    pf = pf_ref[...]
    for part in range(4):
        z = proj(col + part * KV_SLAB, KV_SLAB)
        for g in range(KV_GROUPS):
            own = (lane < HEAD_DIM) if g == 0 else (lane >= HEAD_DIM)
            slot = (part * KV_GROUPS + g) * KV_SLAB
            kv_ref[:, slot:slot + KV_SLAB] = jnp.where(
                own, z, pf[:, g * LANES:(g + 1) * LANES]).astype(kv_ref.dtype)
    col += 4 * KV_SLAB
    g_ref[...] = proj(col, GATE_COLS)
    col += GATE_COLS
    s_ref[...] = proj(col, S5_WIDTH)
    col += S5_WIDTH
    m_ref[...] = proj(col, MERGE_COLS)


def _log2e_terms():
    terms, rest = [], np.float64(LOG2E)
    for _ in range(3):
        t = np.float64(np.float32(rest).astype(ml_dtypes.bfloat16))
        terms.append(float(t))
        rest -= t
    return terms


def _key_position_features(seq):
    p = np.arange(seq)
    out = np.zeros((seq, KV_GROUPS, LANES), np.float32)
    for g in range(KV_GROUPS):
        l0 = _group_lane0(1 - g)
        for k in range(3):
            out[:, g, l0 + F_HI + k] = p // SEL_BLOCK
            out[:, g, l0 + F_LO + k] = p % SEL_BLOCK
        out[:, g, l0 + ONE_LANE] = 1.0
    return jnp.asarray(out.reshape(seq, KV_GROUPS * LANES))


def _query_feature_row():
    c = _log2e_terms()
    row = np.zeros((NSA_HEADS, LANES), np.float32)
    for hd in range(NSA_HEADS):
        g = hd // HPG
        slope = 2.0 ** -(hd + 1)
        l0 = _group_lane0(1 - g)
        for k in range(3):
            row[hd, l0 + F_HI + k] = SEL_BLOCK * slope * c[k]
            row[hd, l0 + F_LO + k] = slope * c[k]
            row[hd, l0 + F_CHI + k] = CMP_RADIX * CMP_STRIDE * slope * c[k]
            row[hd, l0 + F_CLO + k] = CMP_STRIDE * slope * c[k]
            row[hd, l0 + F_COFF + k] = slope * c[k]
    return jnp.asarray(row.reshape(1, Q_COLS))


def _inproj(x2, mod3, w_all, seq):
    n, d = x2.shape
    tiles_per_batch = seq // TM_IN
    rows16 = TM_IN // CMP_STRIDE
    widths = (Q_COLS, KV_OUT_COLS, GATE_COLS, S5_WIDTH, MERGE_COLS)
    dtypes = (BF16, BF16, F32, F32, F32)
    pf = _key_position_features(seq)
    qf = _query_feature_row()
    row_spec = lambda w: pl.BlockSpec((TM_IN, w), lambda i: (i, 0))
    out_specs = [row_spec(Q_COLS),
                 pl.BlockSpec((2, rows16, CMP_STRIDE * KV_SLAB), lambda i: (0, i, 0))]
    out_specs += [row_spec(w) for w in widths[1:]]
    out_shape = [jax.ShapeDtypeStruct((n, Q_COLS), BF16),
                 jax.ShapeDtypeStruct((2, n // CMP_STRIDE, CMP_STRIDE * KV_SLAB), F32)]
    out_shape += [jax.ShapeDtypeStruct((n, w), dt) for w, dt in zip(widths[1:], dtypes[1:])]
    return pl.pallas_call(
        _inproj_kernel,
        grid=(n // TM_IN,),
        in_specs=[pl.BlockSpec((TM_IN, d), lambda i: (i, 0)),
                  pl.BlockSpec((1, 6, d), lambda i: (i // tiles_per_batch, 0, 0)),
                  _const_spec(w_all.shape),
                  pl.BlockSpec((TM_IN, KV_GROUPS * LANES), lambda i: (i % tiles_per_batch, 0)),
                  _const_spec(qf.shape)],
        out_specs=out_specs,
        out_shape=out_shape,
        scratch_shapes=[pltpu.VMEM((TM_IN, KV_SLAB), F32)],
        compiler_params=_cparams(("arbitrary",)),
        name="inproj",
    )(x2, mod3, w_all, pf, qf)


def _build_w_all(w_in):
    d = w_in.shape[0]
    cq = NSA_HEADS * HEAD_DIM
    ckv = 6 * KV_SLAB
    cg = 3 * NSA_HEADS
    zeros = jnp.zeros((d, HEAD_DIM), w_in.dtype)
    pieces = []
    for hd in range(NSA_HEADS):
        wq = w_in[:, hd * HEAD_DIM:(hd + 1) * HEAD_DIM] * (HEAD_DIM ** -0.5 * LOG2E)
        pieces += [wq, zeros] if hd < HPG else [zeros, wq]
    wq_pad = jnp.concatenate(pieces, axis=1)
    wkv = w_in[:, cq:cq + ckv]
    wg = jnp.pad(w_in[:, cq + ckv:cq + ckv + cg], ((0, 0), (0, GATE_COLS - cg)))
    rest = w_in[:, cq + ckv + cg:]
    return jnp.concatenate([wq_pad, wkv, wg, rest], axis=1).astype(BF16)


def _cmp_kernel(ch_ref, w1_ref, pe_ref, w2_ref, o_ref):
    ch = ch_ref[0, 0]
    a = jnp.dot((ch + pe_ref[0, 0]).astype(BF16), w1_ref[0, 0], preferred_element_type=F32)
    b = jnp.dot((ch + pe_ref[0, 1]).astype(BF16), w1_ref[0, 1], preferred_element_type=F32)
    n = a.shape[0]
    hsum = a + pltpu.roll(b, n - 1, axis=0)
    hact = hsum * jax.nn.sigmoid(hsum)
    o_ref[0, 0] = jnp.dot(hact.astype(BF16), w2_ref[0], preferred_element_type=F32).astype(o_ref.dtype)


def _cmp_weights(pe, w1, w2):
    half = CMP_BLOCK // 2
    halves = []
    pes = []
    for lo in (0, half):
        w = w1[lo:lo + half]
        z = jnp.zeros_like(w)
        w_g0 = jnp.concatenate([w, z], axis=-1)
        w_g1 = jnp.concatenate([z, w], axis=-1)
        halves.append(jnp.stack([w_g0, w_g1], axis=1).reshape(half * KV_SLAB, KV_GROUPS * CMP_HIDDEN))
        p = pe[lo:lo + half]
        pes.append(jnp.broadcast_to(p[:, None, :], (half, KV_GROUPS, HEAD_DIM)).reshape(1, half * KV_SLAB))
    z2 = jnp.zeros_like(w2)
    w2_blk = jnp.concatenate([jnp.concatenate([w2, z2], axis=1), jnp.concatenate([z2, w2], axis=1)], axis=0)
    return jnp.stack(halves).astype(BF16), jnp.stack(pes), w2_blk.astype(BF16)


def _compress(ch, w1s, pes, w2s):
    kinds, b, nchunk, width = ch.shape
    hid = w1s.shape[-1]
    return pl.pallas_call(
        _cmp_kernel,
        grid=(kinds, b),
        in_specs=[pl.BlockSpec((1, 1, nchunk, width), lambda k, i: (k, i, 0, 0)),
                  pl.BlockSpec((1, 2, width, hid), lambda k, i: (k, 0, 0, 0)),
                  pl.BlockSpec((1, 2, 1, width), lambda k, i: (k, 0, 0, 0)),
                  pl.BlockSpec((1, hid, KV_SLAB), lambda k, i: (k, 0, 0))],
        out_specs=pl.BlockSpec((1, 1, nchunk, KV_SLAB), lambda k, i: (k, i, 0, 0)),
        out_shape=jax.ShapeDtypeStruct((kinds, b, nchunk, KV_SLAB), BF16),
        compiler_params=_cparams(("arbitrary", "arbitrary")),
        name="compress",
    )(ch, w1s, pes, w2s)


def _softmax_parts(s):
    m = jnp.max(s, axis=-1, keepdims=True)
    e = jnp.exp2(s - m)
    return m, e, jnp.sum(e, axis=-1, keepdims=True)


def _attn_kernel(q_ref, kc_ref, vc_ref, ks0_ref, ks1_ref, vs0_ref, vs1_ref, kw0_ref, kw1_ref,
                 vw0_ref, vw1_ref, zg_ref,
                 ov_ref, cf_ref, hot_ref, tri_ref, wm_ref, gsel_ref, o_ref, v_scr, lhs_scr, m_scr, acc_scr,
                 part_scr, tile_smem,
                 *, n_cmp, n_sel):
    qb = pl.program_id(1)
    t0 = qb * Q_BLOCK
    ncp = kc_ref.shape[2]
    gates = jax.nn.sigmoid(zg_ref[...])
    g_hi = gates.astype(BF16)
    g_lo = (gates - g_hi.astype(F32)).astype(BF16)
    gates_b = jnp.dot(jnp.concatenate([g_hi, g_lo], axis=1), gsel_ref[...], preferred_element_type=F32)

    def gate(g, h, branch):
        c = (g * HPG + h) * 3 + branch
        return gates_b[:, c * LANES:(c + 1) * LANES]
    r_col = lax.broadcasted_iota(jnp.int32, (Q_BLOCK, 1), 0)
    lane = lax.broadcasted_iota(jnp.int32, (Q_BLOCK, LANES), 1)
    tri = tri_ref[...]
    wmask = wm_ref[jnp.minimum(qb, WIN_BACK)]
    hrows = [slice(h * Q_BLOCK, (h + 1) * Q_BLOCK) for h in range(HPG)]

    t0a = pl.multiple_of(t0, Q_BLOCK)
    w0 = pl.multiple_of(jnp.maximum(qb - WIN_BACK, 0) * Q_BLOCK, Q_BLOCK)

    scores = []
    for g in range(KV_GROUPS):
        ks_ref = (ks0_ref, ks1_ref)[g]
        kw_ref = (kw0_ref, kw1_ref)[g]
        own_c = lax.broadcasted_iota(jnp.int32, (ncp, LANES), 1)
        own_c = (own_c < HEAD_DIM) if g == 0 else (own_c >= HEAD_DIM)
        lhs_pos = jnp.concatenate(
            [q_ref[:, (g * HPG + h) * LANES:(g * HPG + h + 1) * LANES] for h in range(HPG)], axis=0)
        kc_aug = jnp.where(own_c, kc_ref[0, 0], cf_ref[g])
        scores.append([lhs_pos, _nt_dot(lhs_pos, kc_aug)])
    for g in range(KV_GROUPS):
        ks_ref = (ks0_ref, ks1_ref)[g]
        kw_ref = (kw0_ref, kw1_ref)[g]
        lhs_pos = scores[g][0]
        scores[g].append(_nt_dot(lhs_pos, ks_ref[pl.ds(t0a, Q_BLOCK), :]))
        scores[g].append(_nt_dot(lhs_pos, kw_ref[pl.ds(w0, WIN_KEYS), :]))

    block_used = []
    cmp_out = []
    for g in range(KV_GROUPS):
        sc = scores[g][1]

        n_i = lax.broadcasted_iota(jnp.int32, (Q_BLOCK, ncp), 1)
        r_i = lax.broadcasted_iota(jnp.int32, (Q_BLOCK, ncp), 0)
        valid_c = jnp.logical_and((n_i * CMP_STRIDE + (CMP_BLOCK - 1)) <= (t0 + r_i), n_i < n_cmp)
        row_any = ((t0 + r_col) >= (CMP_BLOCK - 1)).astype(F32)
        sc = jnp.where(valid_c[None], sc.reshape(HPG, Q_BLOCK, ncp), NEG_INF)
        _, e, l = _softmax_parts(sc)
        p_c = (e * (row_any[None] / l)).astype(BF16)
        o_c = jnp.dot(p_c.reshape(HPG * Q_BLOCK, ncp), vc_ref[0, 0], preferred_element_type=F32)
        imp = jnp.dot(jnp.concatenate([p_c[h] for h in range(HPG)], axis=1), ov_ref[...],
                      preferred_element_type=F32)
        cmp_out.append((o_c, imp))

    for g in range(KV_GROUPS):
        lhs_pos = scores[g][0]
        imp = cmp_out[g][1]

        imp_t = imp.T[:n_sel]
        j_i = lax.broadcasted_iota(jnp.int32, (n_sel, Q_BLOCK), 0)
        cur = lax.shift_right_logical(t0 + lax.broadcasted_iota(jnp.int32, (n_sel, Q_BLOCK), 1),
                                      int(math.log2(SEL_BLOCK)))
        forced = jnp.logical_or(j_i == 0, jnp.logical_or(j_i == cur, j_i == cur - 1))
        valid = j_i <= cur
        v = jnp.where(forced, SEL_FORCE, jnp.where(valid, imp_t, -SEL_FORCE))
        v_scr[g] = v
        n_chunk = n_sel // SUBLANES
        chunks = [v[k * SUBLANES:(k + 1) * SUBLANES] for k in range(n_chunk)]
        ranks = [jnp.zeros((SUBLANES, Q_BLOCK), jnp.int32) for _ in range(n_chunk)]
        sub_i = lax.broadcasted_iota(jnp.int32, (SUBLANES, Q_BLOCK), 0)
        for i in range(n_sel):
            vi = v_scr[g, i:i + 1, :]
            ki = i // SUBLANES
            for k in range(n_chunk):
                if k > ki:
                    beats = jnp.where(vi >= chunks[k], 1, 0)
                elif k < ki:
                    beats = jnp.where(vi > chunks[k], 1, 0)
                else:
                    beats = jnp.where(sub_i > (i - ki * SUBLANES),
                                      jnp.where(vi >= chunks[k], 1, 0),
                                      jnp.where(vi > chunks[k], 1, 0))
                ranks[k] = ranks[k] + beats
        rank = jnp.concatenate(ranks, axis=0)
        chosen = jnp.logical_and(jnp.logical_and(rank < SEL_TOPK, valid), j_i < 2 * qb)
        bias_t = jnp.where(chosen, 0.0, NEG_INF)
        bias_t = jnp.concatenate([bias_t, jnp.full((LANES - n_sel, Q_BLOCK), NEG_INF, F32)], axis=0)
        bias_f = bias_t.T
        block_used.append(jnp.max(bias_f, axis=0, keepdims=True))
        bias = bias_f.astype(BF16)
        lhs_scr[g] = jnp.concatenate([lhs_pos, jnp.concatenate([bias] * HPG, axis=0)], axis=1)

    for g in range(KV_GROUPS):
        _, _, sd, sw = scores[g]
        o_c = cmp_out[g][0]
        vs_ref = (vs0_ref, vs1_ref)[g]
        vw_ref = (vw0_ref, vw1_ref)[g]
        f0 = _group_lane0(1 - g)

        vd = vs_ref[pl.ds(t0a, Q_BLOCK), :]
        sd = (sd.reshape(HPG, Q_BLOCK, Q_BLOCK) + tri[None]).reshape(HPG * Q_BLOCK, Q_BLOCK)
        m = jnp.max(sd, axis=-1, keepdims=True)
        m_scr[g] = jnp.broadcast_to(m, m_scr.shape[1:])
        acc_scr[g] = jnp.dot(jnp.exp2(sd - m).astype(BF16), vd, preferred_element_type=F32)

        v_w = vw_ref[pl.ds(w0, WIN_KEYS), :]
        sw = (sw.reshape(HPG, Q_BLOCK, WIN_KEYS) + wmask[None]).reshape(HPG * Q_BLOCK, WIN_KEYS)
        e = jnp.exp2(sw - jnp.max(sw, axis=-1, keepdims=True))
        o_w = jnp.dot(e.astype(BF16), v_w, preferred_element_type=F32)
        o_w = o_w * (1.0 / o_w[:, f0 + ONE_LANE:f0 + ONE_LANE + 1])

        for h in range(HPG):
            part_scr[g, hrows[h]] = gate(g, h, 0) * o_c[hrows[h]] + gate(g, h, 2) * o_w[hrows[h]]

    n_key_tiles = hot_ref.shape[0] // Q_BLOCK - 1
    per_step = SEL_TK // Q_BLOCK
    used = jnp.maximum(block_used[0], block_used[1])
    used = jnp.maximum(used, pltpu.roll(used, LANES - 1, axis=1))
    lane1 = lax.broadcasted_iota(jnp.int32, (1, LANES), 1)
    tile_of_lane = lax.shift_right_logical(lane1, 1)
    bit = jnp.where(jnp.logical_and(used > -1.0, (lane1 & 1) == 0),
                    jnp.left_shift(1, tile_of_lane & (MASK_BITS - 1)), 0).astype(F32)
    word = lax.shift_right_logical(tile_of_lane, int(math.log2(MASK_BITS)))
    masks = [jnp.sum(jnp.where(word == w, bit, 0.0), axis=1, keepdims=True)[0, 0].astype(jnp.int32)
             for w in range((n_key_tiles + MASK_BITS - 1) // MASK_BITS)]
    count = jnp.int32(0)
    for j in range(n_key_tiles):
        tile_smem[count] = j
        count = count + (lax.shift_right_logical(masks[j // MASK_BITS], j % MASK_BITS) & 1)
    for k in range(per_step - 1):
        tile_smem[count + k] = n_key_tiles

    def tile_offsets(i, k):
        j = tile_smem[i * per_step + k]
        hot_off = pl.multiple_of(j * Q_BLOCK, Q_BLOCK)
        kv_off = pl.multiple_of(jnp.minimum(j, n_key_tiles - 1) * Q_BLOCK, Q_BLOCK)
        return kv_off, hot_off

    def sel_body(i, carry):
        offs = [tile_offsets(i, k) for k in range(per_step)]
        hot = jnp.concatenate([hot_ref[pl.ds(ho, Q_BLOCK), :] for _, ho in offs], axis=0)
        s_g = []
        for g in range(KV_GROUPS):
            ks_ref = (ks0_ref, ks1_ref)[g]
            k_t = jnp.concatenate([ks_ref[pl.ds(ko, Q_BLOCK), :] for ko, _ in offs], axis=0)
            s_g.append(_nt_dot(lhs_scr[g], jnp.concatenate([k_t, hot], axis=1)))
        for g in range(KV_GROUPS):
            vs_ref = (vs0_ref, vs1_ref)[g]
            v_t = jnp.concatenate([vs_ref[pl.ds(ko, Q_BLOCK), :] for ko, _ in offs], axis=0)
            s = s_g[g]
            m_old = m_scr[g]
            m_new = jnp.maximum(m_old, jnp.max(s, axis=-1, keepdims=True))
            p = jnp.exp2(s - jnp.tile(m_new, (1, SEL_TK // LANES)))
            alpha = jnp.exp2(m_old - m_new)
            m_scr[g] = m_new
            pv = jnp.dot(p.astype(BF16), v_t, preferred_element_type=F32)
            acc_scr[g] = alpha * acc_scr[g] + pv
        return carry

    lax.fori_loop(0, (count + per_step - 1) // per_step, sel_body, 0)

    o_s = []
    for g in range(KV_GROUPS):
        one = _group_lane0(1 - g) + ONE_LANE
        acc = acc_scr[g]
        o_s.append(acc * (1.0 / acc[:, one:one + 1]))
    for h in range(HPG):
        outs = [part_scr[g, hrows[h]] + gate(g, h, 1) * o_s[g][hrows[h]] for g in range(KV_GROUPS)]
        o_ref[:, h * LANES:(h + 1) * LANES] = jnp.where(lane < HEAD_DIM, outs[0], outs[1]).astype(o_ref.dtype)


def _attention_tables(seq, ncp, n_cmp, n_sel):
    cstart = np.arange(ncp) * CMP_STRIDE
    sstart = np.arange(n_sel) * SEL_BLOCK
    overlap = ((cstart[:, None] < sstart[None, :] + SEL_BLOCK)
               & (cstart[:, None] + CMP_BLOCK > sstart[None, :])
               & (np.arange(ncp)[:, None] < n_cmp)).astype(np.float32)
    overlap = np.pad(overlap, ((0, 0), (0, LANES - n_sel)))
    ov4 = np.tile(overlap, (HPG, 1))
    n = np.arange(ncp)
    cfeat = np.zeros((KV_GROUPS, ncp, LANES), np.float32)
    for g in range(KV_GROUPS):
        l0 = _group_lane0(1 - g)
        for k in range(3):
            cfeat[g, :, l0 + F_CHI + k] = n // CMP_RADIX
            cfeat[g, :, l0 + F_CLO + k] = n % CMP_RADIX
            cfeat[g, :, l0 + F_COFF + k] = CMP_BLOCK - 1
    blk = np.concatenate([np.arange(seq) // SEL_BLOCK, np.full(Q_BLOCK, n_sel)])
    hot = (blk[:, None] == np.arange(LANES)[None, :]).astype(np.float32)
    r = np.arange(Q_BLOCK)[:, None]
    tri = np.where(np.arange(Q_BLOCK)[None, :] <= r, 0.0, NEG_INF).astype(np.float32)
    c = np.arange(WIN_KEYS)[None, :]
    wm = []
    for qb in range(WIN_BACK + 1):
        dist = (qb * Q_BLOCK + r) - c if qb < WIN_BACK else (WINDOW + r) - c
        wm.append(np.where((dist >= 0) & (dist < WINDOW), 0.0, NEG_INF))
    wm = np.stack(wm).astype(np.float32)
    n_gate = 3 * NSA_HEADS
    gsel = np.zeros((2 * LANES, n_gate, LANES), np.float32)
    for c in range(n_gate):
        gsel[c, c, :] = 1.0
        gsel[LANES + c, c, :] = 1.0
    gsel = jnp.asarray(gsel.reshape(2 * LANES, n_gate * LANES), BF16)
    return (jnp.asarray(ov4, BF16), jnp.asarray(cfeat, BF16), jnp.asarray(hot, BF16), gsel,
            jnp.asarray(tri), jnp.asarray(wm))


def _attention(zq, kvc, zkv, zg, batch, seq):
    n = zq.shape[0]
    nqb = seq // Q_BLOCK
    ncp = kvc.shape[2]
    n_cmp = seq // CMP_STRIDE - 1
    n_sel = seq // SEL_BLOCK
    assert n_sel <= HEAD_DIM and n_sel % SUBLANES == 0 and seq % SEL_TK == 0 and nqb > WIN_BACK
    assert ncp // CMP_RADIX <= 256 and seq // SEL_BLOCK <= 256
    ov4, cfeat, hot, gsel, tri, wm = _attention_tables(seq, ncp, n_cmp, n_sel)
    out_cols = HPG * LANES

    kernel = functools.partial(_attn_kernel, n_cmp=n_cmp, n_sel=n_sel)
    kv_spec = lambda idx: pl.BlockSpec((seq, KV_SLAB), lambda b, i, idx=idx: (b, idx))
    return pl.pallas_call(
        kernel,
        grid=(batch, nqb),
        in_specs=[pl.BlockSpec((Q_BLOCK, Q_COLS), lambda b, i: (b * nqb + i, 0)),
                  pl.BlockSpec((1, 1, ncp, KV_SLAB), lambda b, i: (0, b, 0, 0)),
                  pl.BlockSpec((1, 1, ncp, KV_SLAB), lambda b, i: (1, b, 0, 0)),
                  *[kv_spec(k) for k in range(8)],
                  pl.BlockSpec((Q_BLOCK, GATE_COLS), lambda b, i: (b * nqb + i, 0)),
                  _const_spec(ov4.shape), _const_spec(cfeat.shape), _const_spec(hot.shape),
                  _const_spec(tri.shape), _const_spec(wm.shape), _const_spec(gsel.shape)],
        out_specs=pl.BlockSpec((Q_BLOCK, out_cols), lambda b, i: (b * nqb + i, 0)),
        out_shape=jax.ShapeDtypeStruct((n, out_cols), BF16),
        scratch_shapes=[pltpu.VMEM((KV_GROUPS, n_sel, Q_BLOCK), F32),
                        pltpu.VMEM((KV_GROUPS, HPG * Q_BLOCK, 2 * LANES), BF16),
                        pltpu.VMEM((KV_GROUPS, HPG * Q_BLOCK, LANES), F32),
                        pltpu.VMEM((KV_GROUPS, HPG * Q_BLOCK, LANES), F32),
                        pltpu.VMEM((KV_GROUPS, HPG * Q_BLOCK, LANES), F32),
                        pltpu.SMEM((seq // Q_BLOCK + SEL_TK // Q_BLOCK,), jnp.int32)],
        compiler_params=_cparams(("arbitrary", "arbitrary")),
        name="nsa_attn",
    )(zq, kvc, kvc, *([zkv] * 8), zg, ov4, cfeat, hot, tri, wm, gsel)


def _s5_kernel(u_ref, lagk_ref, mre_ref, mim_ref, cre_ref, cim_ref, lre_ref, lim_ref, d_ref, y_ref,
               ere, eim, xre, xim, tp_scr, *, n_chunks):
    def tok(t):
        return pl.ds(t, n_chunks, stride=S5_CHUNK)

    @pl.when(pl.program_id(1) == 0)
    def _():
        tp_scr[...] = jnp.zeros(tp_scr.shape, tp_scr.dtype)
        for s in range(S5_CHUNK):
            for t in range(s, S5_CHUNK):
                tp_scr[s * LANES:(s + 1) * LANES, t * LANES:(t + 1) * LANES] = lagk_ref[0, t - s]

    u = jnp.concatenate([u_ref[tok(t), :] for t in range(S5_CHUNK)], axis=1).astype(BF16)
    y = jnp.dot(u, tp_scr[...], preferred_element_type=F32)
    ere[...] = jnp.dot(u, mre_ref[0, 0], preferred_element_type=F32)
    eim[...] = jnp.dot(u, mim_ref[0, 0], preferred_element_type=F32)
    lr = lre_ref[0]
    li = lim_ref[0]

    def body(c, carry):
        xr, xi = carry
        row = pl.ds(c, 1)
        xre[row, :] = xr
        xim[row, :] = xi
        er = ere[row, :]
        ei = eim[row, :]
        return lr * xr - li * xi + er, lr * xi + li * xr + ei

    zero = jnp.zeros((1, ere.shape[1]), F32)
    lax.fori_loop(0, n_chunks, body, (zero, zero), unroll=8)
    y = y + jnp.dot(xre[...].astype(BF16), cre_ref[0, 0], preferred_element_type=F32)
    y = y + jnp.dot(xim[...].astype(BF16), cim_ref[0, 0], preferred_element_type=F32)
    d = d_ref[0]
    for t in range(S5_CHUNK):
        y_ref[tok(t), :] = y[:, t * LANES:(t + 1) * LANES] + d * u_ref[tok(t), :]


def _s5_matrices(a_re, a_im, b_re, b_im, c_re, c_im, d_skip, log_dt):
    t = S5_CHUNK
    hp = lax.Precision.HIGHEST
    dt = jnp.exp(log_dt)[:, None]
    lam_re = jnp.minimum(a_re, -1e-4)
    lam_im = a_im
    mag = jnp.exp(lam_re * dt)
    ang = lam_im * dt
    lb_re = mag * jnp.cos(ang)
    lb_im = mag * jnp.sin(ang)
    den = lam_re * lam_re + lam_im * lam_im
    nr = lb_re - 1.0
    coef_re = (nr * lam_re + lb_im * lam_im) / den
    coef_im = (lb_im * lam_re - nr * lam_im) / den
    bb_re = coef_re[..., None] * b_re - coef_im[..., None] * b_im
    bb_im = coef_re[..., None] * b_im + coef_im[..., None] * b_re
    j = jnp.arange(t + 1, dtype=F32)[:, None, None]
    pmag = jnp.exp(j * (lam_re * dt)[None])
    pw_re = pmag * jnp.cos(j * ang[None])
    pw_im = pmag * jnp.sin(j * ang[None])
    cl_re = c_re[None] * pw_re[:, :, None, :] - c_im[None] * pw_im[:, :, None, :]
    cl_im = c_re[None] * pw_im[:, :, None, :] + c_im[None] * pw_re[:, :, None, :]
    kern = (jnp.einsum('jgop,gpi->gjoi', cl_re[:t], bb_re, precision=hp)
            - jnp.einsum('jgop,gpi->gjoi', cl_im[:t], bb_im, precision=hp))
    eye = jnp.eye(S5_LB, dtype=F32)
    nlb = S5_GROUPS // S5_LB
    lagk = jnp.einsum('qgjoi,gh->qjgiho', kern.reshape(nlb, S5_LB, t, S5_GROUP, S5_GROUP), eye,
                      precision=hp).reshape(nlb, t, LANES, LANES)
    rv_re = pw_re[t - 1 - np.arange(t)]
    rv_im = pw_im[t - 1 - np.arange(t)]
    m_re = rv_re[..., None] * bb_re[None] - rv_im[..., None] * bb_im[None]
    m_im = rv_re[..., None] * bb_im[None] + rv_im[..., None] * bb_re[None]

    def lay_m(m):
        m = m.reshape(2, t, nlb, S5_LB, S5_STATE, S5_GROUP)
        return jnp.einsum('csqgpi,gh->cqsgihp', m, eye, precision=hp).reshape(
            2, nlb, t * LANES, S5_LB * S5_STATE)

    def lay_c(c):
        c = c.reshape(2, t, nlb, S5_LB, S5_GROUP, S5_STATE)
        return jnp.einsum('ctqgop,gh->cqgptho', c, eye, precision=hp).reshape(
            2, nlb, S5_LB * S5_STATE, t * LANES)

    lam_t_re = pw_re[t].reshape(nlb, 1, S5_LB * S5_STATE)
    lam_t_im = pw_im[t].reshape(nlb, 1, S5_LB * S5_STATE)
    d_row = d_skip.reshape(nlb, 1, LANES)
    m_ops = lay_m(jnp.stack([m_re, m_im])).astype(BF16)
    c_ops = lay_c(jnp.stack([cl_re[1:], -cl_im[1:]])).astype(BF16)
    return lagk.astype(BF16), m_ops, c_ops, lam_t_re, lam_t_im, d_row


def _s5(zs, mats, batch, seq):
    n_chunks = seq // S5_CHUNK
    lagk, m_ops, c_ops, l_re, l_im, d_row = mats
    nlb = lagk.shape[0]
    wcol = S5_CHUNK * LANES
    wst = S5_LB * S5_STATE
    kernel = functools.partial(_s5_kernel, n_chunks=n_chunks)
    p3 = lambda r, c: pl.BlockSpec((1, r, c), lambda q, b: (q, 0, 0))
    part = lambda k, r, c: pl.BlockSpec((1, 1, r, c), lambda q, b, k=k: (k, q, 0, 0))
    return pl.pallas_call(
        kernel,
        grid=(nlb, batch),
        in_specs=[pl.BlockSpec((seq, LANES), lambda q, b: (b, q)),
                  pl.BlockSpec((1, S5_CHUNK, LANES, LANES), lambda q, b: (q, 0, 0, 0)),
                  part(0, wcol, wst), part(1, wcol, wst), part(0, wst, wcol), part(1, wst, wcol),
                  p3(1, wst), p3(1, wst), p3(1, LANES)],
        out_specs=pl.BlockSpec((seq, LANES), lambda q, b: (b, q)),
        out_shape=jax.ShapeDtypeStruct(zs.shape, F32),
        scratch_shapes=[pltpu.VMEM((n_chunks, wst), F32) for _ in range(4)]
        + [pltpu.VMEM((wcol, wcol), BF16)],
        compiler_params=_cparams(("arbitrary", "arbitrary")),
        name="s5",
    )(zs, lagk, m_ops, m_ops, c_ops, c_ops, l_re, l_im, d_row)


def _merge_kernel(oa_ref, ys_ref, zm_ref, x_ref, mod_ref, wn_ref, wg_ref, wo_ref, g_ref, b_ref, o_ref,
                  *, alpha):
    d = x_ref.shape[-1]
    y_a = jnp.dot(oa_ref[...], wn_ref[...], preferred_element_type=F32)
    gl = jax.nn.gelu(ys_ref[...], approximate=True)
    zz = jnp.dot(gl.astype(BF16), wg_ref[...], preferred_element_type=F32)
    y_b = zz[:, :d] * jax.nn.sigmoid(zz[:, d:])
    zm = zm_ref[...]
    mix_in = jax.nn.sigmoid(zm[:, :d]) * y_a + jax.nn.sigmoid(zm[:, d:]) * y_b
    mix = jnp.dot(mix_in.astype(BF16), wo_ref[...], preferred_element_type=F32)
    gate = mod_ref[0, 2:3, :]
    r = alpha * x_ref[...] + gate * mix
    o_ref[...] = _layer_norm(r) * g_ref[...] + b_ref[...]


def _merge(oa, ys, zm, x2, mod3, wn_pad, wg, wo, ln_g, ln_b, seq, alpha):
    n, d = x2.shape
    tiles_per_batch = seq // TM_MERGE
    row = lambda w: pl.BlockSpec((TM_MERGE, w), lambda i: (i, 0))
    return pl.pallas_call(
        functools.partial(_merge_kernel, alpha=alpha),
        grid=(n // TM_MERGE,),
        in_specs=[row(oa.shape[1]), row(ys.shape[1]), row(zm.shape[1]), row(d),
                  pl.BlockSpec((1, 6, d), lambda i: (i // tiles_per_batch, 0, 0)),
                  _const_spec(wn_pad.shape), _const_spec(wg.shape), _const_spec(wo.shape),
                  _const_spec((1, d)), _const_spec((1, d))],
        out_specs=row(d),
        out_shape=jax.ShapeDtypeStruct((n, d), F32),
        compiler_params=_cparams(("arbitrary",)),
        name="merge",
    )(oa, ys, zm, x2, mod3, wn_pad, wg, wo, ln_g.reshape(1, d), ln_b.reshape(1, d))


def _pad_nsa_out(w):
    rows = []
    for h in range(HPG):
        for g in range(KV_GROUPS):
            hd = g * HPG + h
            rows.append(w[hd * HEAD_DIM:(hd + 1) * HEAD_DIM])
    return jnp.concatenate(rows, axis=0).astype(BF16)


def _ffn_kernel(x_ref, mod_ref, wup_ref, cw_ref, cb_ref, wdn_ref, g_ref, b_ref, o_ref, tail_ref, act_scr,
                *, alpha, tiles_per_batch):
    i = pl.program_id(0)
    tm = x_ref.shape[0]

    @pl.when(i % tiles_per_batch == 0)
    def _():
        tail_ref[...] = jnp.zeros(tail_ref.shape, F32)

    shift = mod_ref[0, 3:4, :]
    scale = mod_ref[0, 4:5, :]
    gate = mod_ref[0, 5:6, :]
    row = lax.broadcasted_iota(jnp.int32, (SUBLANES, FF_CHUNK), 0)

    def shift_rows(a, prev_rows):
        rolled = pltpu.roll(a, 1, axis=0)
        head = jnp.where(row == 0, prev_rows[SUBLANES - 1:SUBLANES], rolled[:SUBLANES])
        return jnp.concatenate([head, rolled[SUBLANES:]], axis=0)

    part_rows = tm // FFN_PARTS
    for part in range(FFN_PARTS):
        rows = slice(part * part_rows, (part + 1) * part_rows)
        x = x_ref[rows, :]
        h2 = (_layer_norm(x) * (1.0 + scale) + shift).astype(BF16)

        def conv_cols(c0):
            cols = slice(c0, c0 + FF_CHUNK)
            up = jnp.dot(h2, wup_ref[:, cols], preferred_element_type=F32)
            tail = tail_ref[:, cols]
            tail_ref[:, cols] = up[part_rows - SUBLANES:part_rows]
            w = cw_ref[:, cols]
            inner = shift_rows(w[0:1] * up, w[0:1] * tail) + w[1:2] * up
            prev_inner = pltpu.roll(w[0:1] * tail, 1, axis=0) + w[1:2] * tail
            return shift_rows(inner, prev_inner) + w[2:3] * up + cb_ref[:, cols]

        for k in range(D_FF // FF_CHUNK):
            val = conv_cols(k * FF_CHUNK)
            gte = conv_cols(D_FF + k * FF_CHUNK)
            act_scr[rows, k * FF_CHUNK:(k + 1) * FF_CHUNK] = (gte * jax.nn.sigmoid(gte) * val).astype(BF16)
        ff = jnp.dot(act_scr[rows, :], wdn_ref[...], preferred_element_type=F32)
        r = alpha * x + gate * ff
        o_ref[rows, :] = _layer_norm(r) * g_ref[...] + b_ref[...]


def _ffn(x1, mod3, wup, conv_w, conv_b, wdn, ln_g, ln_b, seq, alpha):
    n, d = x1.shape
    tiles_per_batch = seq // TM_FFN
    ff2 = wup.shape[1]
    return pl.pallas_call(
        functools.partial(_ffn_kernel, alpha=alpha, tiles_per_batch=tiles_per_batch),
        grid=(n // TM_FFN,),
        in_specs=[pl.BlockSpec((TM_FFN, d), lambda i: (i, 0)),
                  pl.BlockSpec((1, 6, d), lambda i: (i // tiles_per_batch, 0, 0)),
                  _const_spec(wup.shape), _const_spec(conv_w.shape), _const_spec((1, ff2)),
                  _const_spec(wdn.shape), _const_spec((1, d)), _const_spec((1, d))],
        out_specs=pl.BlockSpec((TM_FFN, d), lambda i: (i, 0)),
        out_shape=jax.ShapeDtypeStruct((n, d), F32),
        scratch_shapes=[pltpu.VMEM((SUBLANES, ff2), F32), pltpu.VMEM((TM_FFN, ff2 // 2), BF16)],
        compiler_params=_cparams(("arbitrary",)),
        name="ffn",
    )(x1, mod3, wup, conv_w, conv_b.reshape(1, ff2), wdn, ln_g.reshape(1, d), ln_b.reshape(1, d))


def kernel(x, c, w_ada, b_ada, w_in, pe_ck, w_ck1, w_ck2, pe_cv, w_cv1, w_cv2, w_nsa_out,
           s5_a_re, s5_a_im, s5_b_re, s5_b_im, s5_c_re, s5_c_im, s5_d, s5_log_dt, w_s5_glu,
           w_o, ln1_g, ln1_b, w_up, conv_w, conv_b, w_down, ln2_g, ln2_b):
    batch, seq, d = x.shape
    depth = w_ada.shape[0]
    alpha = (2.0 * depth) ** 0.25
    n = batch * seq
    n_chunk16 = seq // CMP_STRIDE
    xf = x.reshape(n, d)
    for l in range(depth):
        mod3 = _ada(c, w_ada[l], b_ada[l]).reshape(batch, 6, d)

        zq, zc, zkv, zg, zs, zm = _inproj(xf, mod3, _build_w_all(w_in[l]), seq)

        wk1, pek, wk2 = _cmp_weights(pe_ck[l], w_ck1[l], w_ck2[l])
        wv1, pev, wv2 = _cmp_weights(pe_cv[l], w_cv1[l], w_cv2[l])
        ch = zc.reshape(2, batch, n_chunk16, CMP_STRIDE * KV_SLAB)
        kvc = _compress(ch, jnp.stack([wk1, wv1]), jnp.stack([pek, pev]), jnp.stack([wk2, wv2]))

        oa = _attention(zq, kvc, zkv, zg, batch, seq)

        mats = _s5_matrices(s5_a_re[l], s5_a_im[l], s5_b_re[l], s5_b_im[l], s5_c_re[l], s5_c_im[l],
                            s5_d[l], s5_log_dt[l])
        ys = _s5(zs, mats, batch, seq)

        x1 = _merge(oa, ys, zm, xf, mod3, _pad_nsa_out(w_nsa_out[l]), w_s5_glu[l].astype(BF16),
                    w_o[l].astype(BF16), ln1_g[l], ln1_b[l], seq, alpha)

        xf = _ffn(x1, mod3, w_up[l].astype(BF16), conv_w[l], conv_b[l], w_down[l].astype(BF16),
                  ln2_g[l], ln2_b[l], seq, alpha)
    return xf.reshape(batch, seq, d)
```

```python
import functools
import math

import jax
import jax.numpy as jnp
import ml_dtypes
import numpy as np
from jax import lax
from jax.experimental import pallas as pl
from jax.experimental.pallas import tpu as pltpu

F32 = jnp.float32
BF16 = jnp.bfloat16

D_MODEL = 1024
NSA_HEADS = 8
KV_GROUPS = 2
HPG = NSA_HEADS // KV_GROUPS
HEAD_DIM = 64
CMP_BLOCK = 32
CMP_STRIDE = 16
CMP_HIDDEN = 128
SEL_BLOCK = 64
SEL_TOPK = 16
WINDOW = 512
Q_BLOCK = 128
S5_GROUP = 16
S5_WIDTH = 512
S5_GROUPS = S5_WIDTH // S5_GROUP
S5_STATE = 64
D_FF = 2816
CONV_WIDTH = 3
LN_EPS = 1e-5
NEG_INF = -1e30
SEL_FORCE = 1e9

LANES = 128
SUBLANES = 8
VMEM_LIMIT = 56 * 1024 * 1024

TM_IN = 512
TM_MERGE = 512
TM_FFN = 512
FF_CHUNK = 256
FFN_PARTS = 2
SEL_TK = 768
WIN_KEYS = WINDOW + Q_BLOCK
WIN_BACK = WINDOW // Q_BLOCK
S5_CHUNK = 8
S5_LB = LANES // S5_GROUP
ADA_COLS = 1024

Q_COLS = NSA_HEADS * LANES
KV_SLAB = KV_GROUPS * HEAD_DIM
GATE_COLS = LANES
MERGE_COLS = 2 * D_MODEL
KV_OUT_COLS = 8 * KV_SLAB
F_HI, F_LO = 0, 3
ONE_LANE = 6
F_CHI, F_CLO, F_COFF = 7, 10, 13
CMP_RADIX = 16
LOG2E = 1.4426950408889634
MASK_BITS = 16


def _cparams(sem):
    return pltpu.CompilerParams(dimension_semantics=sem, vmem_limit_bytes=VMEM_LIMIT)


def _const_spec(shape):
    n = len(shape)
    return pl.BlockSpec(shape, lambda *_: (0,) * n)


def _layer_norm(x):
    mu = jnp.mean(x, axis=-1, keepdims=True)
    xc = x - mu
    var = jnp.mean(xc * xc, axis=-1, keepdims=True)
    return xc * lax.rsqrt(var + LN_EPS)


def _nt_dot(a, b):
    return lax.dot_general(a, b, (((1,), (1,)), ((), ())), preferred_element_type=F32)


def _group_lane0(g):
    return g * HEAD_DIM


def _ada_kernel(c_ref, w_ref, b_ref, o_ref):
    c = c_ref[...]
    a = c * jax.nn.sigmoid(c)
    o_ref[...] = jnp.dot(a.astype(BF16), w_ref[...].astype(BF16), preferred_element_type=F32) + b_ref[...]


def _ada(c, w_ada, b_ada):
    b, d = c.shape
    n = w_ada.shape[1]
    blk = ADA_COLS
    return pl.pallas_call(
        _ada_kernel,
        grid=(n // blk,),
        in_specs=[pl.BlockSpec((b, d), lambda j: (0, 0)),
                  pl.BlockSpec((d, blk), lambda j: (0, j)),
                  pl.BlockSpec((1, blk), lambda j: (0, j))],
        out_specs=pl.BlockSpec((b, blk), lambda j: (0, j)),
        out_shape=jax.ShapeDtypeStruct((b, n), F32),
        compiler_params=_cparams(("arbitrary",)),
        name="ada",
    )(c, w_ada, b_ada.reshape(1, n))


def _inproj_kernel(x_ref, mod_ref, w_ref, pf_ref, qf_ref, q_ref, c_ref, kv_ref, g_ref, s_ref, m_ref, zc_scr):
    hn = _layer_norm(x_ref[...])
    shift = mod_ref[0, 0:1, :]
    scale = mod_ref[0, 1:2, :]
    h = (hn * (1.0 + scale) + shift).astype(BF16)
    tm = x_ref.shape[0]

    def proj(col, width):
        return jnp.dot(h, w_ref[:, col:col + width], preferred_element_type=F32)

    col = 0
    q_ref[...] = (proj(col, Q_COLS) + qf_ref[...]).astype(q_ref.dtype)
    col += Q_COLS
    for kind in range(2):
        zc_scr[...] = proj(col, KV_SLAB)
        for tok in range(CMP_STRIDE):
            c_ref[kind, :, tok * KV_SLAB:(tok + 1) * KV_SLAB] = zc_scr[
                pl.ds(tok, tm // CMP_STRIDE, stride=CMP_STRIDE), :]
        col += KV_SLAB
    lane = lax.broadcasted_iota(jnp.int32, (x_ref.shape[0], LANES), 1)
    pf = pf_ref[...]
    for part in range(4):
        z = proj(col + part * KV_SLAB, KV_SLAB)
        for g in range(KV_GROUPS):
            own = (lane < HEAD_DIM) if g == 0 else (lane >= HEAD_DIM)
            slot = (part * KV_GROUPS + g) * KV_SLAB
            kv_ref[:, slot:slot + KV_SLAB] = jnp.where(
                own, z, pf[:, g * LANES:(g + 1) * LANES]).astype(kv_ref.dtype)
    col += 4 * KV_SLAB
    g_ref[...] = proj(col, GATE_COLS)
    col += GATE_COLS
    s_ref[...] = proj(col, S5_WIDTH)
    col += S5_WIDTH
    m_ref[...] = proj(col, MERGE_COLS)


def _log2e_terms():
    terms, rest = [], np.float64(LOG2E)
    for _ in range(3):
        t = np.float64(np.float32(rest).astype(ml_dtypes.bfloat16))
        terms.append(float(t))
        rest -= t
    return terms


def _key_position_features(seq):
    p = np.arange(seq)
    out = np.zeros((seq, KV_GROUPS, LANES), np.float32)
    for g in range(KV_GROUPS):
        l0 = _group_lane0(1 - g)
        for k in range(3):
            out[:, g, l0 + F_HI + k] = p // SEL_BLOCK
            out[:, g, l0 + F_LO + k] = p % SEL_BLOCK
        out[:, g, l0 + ONE_LANE] = 1.0
    return jnp.asarray(out.reshape(seq, KV_GROUPS * LANES))


def _query_feature_row():
    c = _log2e_terms()
    row = np.zeros((NSA_HEADS, LANES), np.float32)
    for hd in range(NSA_HEADS):
        g = hd // HPG
        slope = 2.0 ** -(hd + 1)
        l0 = _group_lane0(1 - g)
        for k in range(3):
            row[hd, l0 + F_HI + k] = SEL_BLOCK * slope * c[k]
            row[hd, l0 + F_LO + k] = slope * c[k]
            row[hd, l0 + F_CHI + k] = CMP_RADIX * CMP_STRIDE * slope * c[k]
            row[hd, l0 + F_CLO + k] = CMP_STRIDE * slope * c[k]
            row[hd, l0 + F_COFF + k] = slope * c[k]
    return jnp.asarray(row.reshape(1, Q_COLS))


def _inproj(x2, mod3, w_all, seq):
    n, d = x2.shape
    tiles_per_batch = seq // TM_IN
    rows16 = TM_IN // CMP_STRIDE
    widths = (Q_COLS, KV_OUT_COLS, GATE_COLS, S5_WIDTH, MERGE_COLS)
    dtypes = (BF16, BF16, F32, F32, F32)
    pf = _key_position_features(seq)
    qf = _query_feature_row()
    row_spec = lambda w: pl.BlockSpec((TM_IN, w), lambda i: (i, 0))
    out_specs = [row_spec(Q_COLS),
                 pl.BlockSpec((2, rows16, CMP_STRIDE * KV_SLAB), lambda i: (0, i, 0))]
    out_specs += [row_spec(w) for w in widths[1:]]
    out_shape = [jax.ShapeDtypeStruct((n, Q_COLS), BF16),
                 jax.ShapeDtypeStruct((2, n // CMP_STRIDE, CMP_STRIDE * KV_SLAB), F32)]
    out_shape += [jax.ShapeDtypeStruct((n, w), dt) for w, dt in zip(widths[1:], dtypes[1:])]
    return pl.pallas_call(
        _inproj_kernel,
        grid=(n // TM_IN,),
        in_specs=[pl.BlockSpec((TM_IN, d), lambda i: (i, 0)),
                  pl.BlockSpec((1, 6, d), lambda i: (i // tiles_per_batch, 0, 0)),
                  _const_spec(w_all.shape),
                  pl.BlockSpec((TM_IN, KV_GROUPS * LANES), lambda i: (i % tiles_per_batch, 0)),
                  _const_spec(qf.shape)],
        out_specs=out_specs,
        out_shape=out_shape,
        scratch_shapes=[pltpu.VMEM((TM_IN, KV_SLAB), F32)],
        compiler_params=_cparams(("arbitrary",)),
        name="inproj",
    )(x2, mod3, w_all, pf, qf)


def _build_w_all(w_in):
    d = w_in.shape[0]
    cq = NSA_HEADS * HEAD_DIM
    ckv = 6 * KV_SLAB
    cg = 3 * NSA_HEADS
    zeros = jnp.zeros((d, HEAD_DIM), w_in.dtype)
    pieces = []
    for hd in range(NSA_HEADS):
        wq = w_in[:, hd * HEAD_DIM:(hd + 1) * HEAD_DIM] * (HEAD_DIM ** -0.5 * LOG2E)
        pieces += [wq, zeros] if hd < HPG else [zeros, wq]
    wq_pad = jnp.concatenate(pieces, axis=1)
    wkv = w_in[:, cq:cq + ckv]
    wg = jnp.pad(w_in[:, cq + ckv:cq + ckv + cg], ((0, 0), (0, GATE_COLS - cg)))
    rest = w_in[:, cq + ckv + cg:]
    return jnp.concatenate([wq_pad, wkv, wg, rest], axis=1).astype(BF16)


def _cmp_kernel(ch_ref, w1_ref, pe_ref, w2_ref, o_ref):
    ch = ch_ref[0, 0]
    a = jnp.dot((ch + pe_ref[0, 0]).astype(BF16), w1_ref[0, 0], preferred_element_type=F32)
    b = jnp.dot((ch + pe_ref[0, 1]).astype(BF16), w1_ref[0, 1], preferred_element_type=F32)
    n = a.shape[0]
    hsum = a + pltpu.roll(b, n - 1, axis=0)
    hact = hsum * jax.nn.sigmoid(hsum)
    o_ref[0, 0] = jnp.dot(hact.astype(BF16), w2_ref[0], preferred_element_type=F32).astype(o_ref.dtype)


def _cmp_weights(pe, w1, w2):
    half = CMP_BLOCK // 2
    halves = []
    pes = []
    for lo in (0, half):
        w = w1[lo:lo + half]
        z = jnp.zeros_like(w)
        w_g0 = jnp.concatenate([w, z], axis=-1)
        w_g1 = jnp.concatenate([z, w], axis=-1)
        halves.append(jnp.stack([w_g0, w_g1], axis=1).reshape(half * KV_SLAB, KV_GROUPS * CMP_HIDDEN))
        p = pe[lo:lo + half]
        pes.append(jnp.broadcast_to(p[:, None, :], (half, KV_GROUPS, HEAD_DIM)).reshape(1, half * KV_SLAB))
    z2 = jnp.zeros_like(w2)
    w2_blk = jnp.concatenate([jnp.concatenate([w2, z2], axis=1), jnp.concatenate([z2, w2], axis=1)], axis=0)
    return jnp.stack(halves).astype(BF16), jnp.stack(pes), w2_blk.astype(BF16)


def _compress(ch, w1s, pes, w2s):
    kinds, b, nchunk, width = ch.shape
    hid = w1s.shape[-1]
    return pl.pallas_call(
        _cmp_kernel,
        grid=(kinds, b),
        in_specs=[pl.BlockSpec((1, 1, nchunk, width), lambda k, i: (k, i, 0, 0)),
                  pl.BlockSpec((1, 2, width, hid), lambda k, i: (k, 0, 0, 0)),
                  pl.BlockSpec((1, 2, 1, width), lambda k, i: (k, 0, 0, 0)),
                  pl.BlockSpec((1, hid, KV_SLAB), lambda k, i: (k, 0, 0))],
        out_specs=pl.BlockSpec((1, 1, nchunk, KV_SLAB), lambda k, i: (k, i, 0, 0)),
        out_shape=jax.ShapeDtypeStruct((kinds, b, nchunk, KV_SLAB), BF16),
        compiler_params=_cparams(("arbitrary", "arbitrary")),
        name="compress",
    )(ch, w1s, pes, w2s)


def _softmax_parts(s):
    m = jnp.max(s, axis=-1, keepdims=True)
    e = jnp.exp2(s - m)
    return m, e, jnp.sum(e, axis=-1, keepdims=True)


def _attn_kernel(q_ref, kc_ref, vc_ref, ks0_ref, ks1_ref, vs0_ref, vs1_ref, kw0_ref, kw1_ref,
                 vw0_ref, vw1_ref, zg_ref,
                 ov_ref, cf_ref, hot_ref, tri_ref, wm_ref, gsel_ref, o_ref, v_scr, lhs_scr, m_scr, acc_scr,
                 part_scr, tile_smem,
                 *, n_cmp, n_sel):
    qb = pl.program_id(1)
    t0 = qb * Q_BLOCK
    ncp = kc_ref.shape[2]
    gates = jax.nn.sigmoid(zg_ref[...])
    g_hi = gates.astype(BF16)
    g_lo = (gates - g_hi.astype(F32)).astype(BF16)
    gates_b = jnp.dot(jnp.concatenate([g_hi, g_lo], axis=1), gsel_ref[...], preferred_element_type=F32)

    def gate(g, h, branch):
        c = (g * HPG + h) * 3 + branch
        return gates_b[:, c * LANES:(c + 1) * LANES]
    r_col = lax.broadcasted_iota(jnp.int32, (Q_BLOCK, 1), 0)
    lane = lax.broadcasted_iota(jnp.int32, (Q_BLOCK, LANES), 1)
    tri = tri_ref[...]
    wmask = wm_ref[jnp.minimum(qb, WIN_BACK)]
    hrows = [slice(h * Q_BLOCK, (h + 1) * Q_BLOCK) for h in range(HPG)]

    t0a = pl.multiple_of(t0, Q_BLOCK)
    w0 = pl.multiple_of(jnp.maximum(qb - WIN_BACK, 0) * Q_BLOCK, Q_BLOCK)

    scores = []
    for g in range(KV_GROUPS):
        ks_ref = (ks0_ref, ks1_ref)[g]
        kw_ref = (kw0_ref, kw1_ref)[g]
        own_c = lax.broadcasted_iota(jnp.int32, (ncp, LANES), 1)
        own_c = (own_c < HEAD_DIM) if g == 0 else (own_c >= HEAD_DIM)
        lhs_pos = jnp.concatenate(
            [q_ref[:, (g * HPG + h) * LANES:(g * HPG + h + 1) * LANES] for h in range(HPG)], axis=0)
        kc_aug = jnp.where(own_c, kc_ref[0, 0], cf_ref[g])
        scores.append([lhs_pos, _nt_dot(lhs_pos, kc_aug)])
    for g in range(KV_GROUPS):
        ks_ref = (ks0_ref, ks1_ref)[g]
        kw_ref = (kw0_ref, kw1_ref)[g]
        lhs_pos = scores[g][0]
        scores[g].append(_nt_dot(lhs_pos, ks_ref[pl.ds(t0a, Q_BLOCK), :]))
        scores[g].append(_nt_dot(lhs_pos, kw_ref[pl.ds(w0, WIN_KEYS), :]))

    block_used = []
    cmp_out = []
    for g in range(KV_GROUPS):
        sc = scores[g][1]

        n_i = lax.broadcasted_iota(jnp.int32, (Q_BLOCK, ncp), 1)
        r_i = lax.broadcasted_iota(jnp.int32, (Q_BLOCK, ncp), 0)
        valid_c = jnp.logical_and((n_i * CMP_STRIDE + (CMP_BLOCK - 1)) <= (t0 + r_i), n_i < n_cmp)
        row_any = ((t0 + r_col) >= (CMP_BLOCK - 1)).astype(F32)
        sc = jnp.where(valid_c[None], sc.reshape(HPG, Q_BLOCK, ncp), NEG_INF)
        _, e, l = _softmax_parts(sc)
        p_c = (e * (row_any[None] / l)).astype(BF16)
        o_c = jnp.dot(p_c.reshape(HPG * Q_BLOCK, ncp), vc_ref[0, 0], preferred_element_type=F32)
        imp = jnp.dot(jnp.concatenate([p_c[h] for h in range(HPG)], axis=1), ov_ref[...],
                      preferred_element_type=F32)
        cmp_out.append((o_c, imp))

    for g in range(KV_GROUPS):
        lhs_pos = scores[g][0]
        imp = cmp_out[g][1]

        imp_t = imp.T[:n_sel]
        j_i = lax.broadcasted_iota(jnp.int32, (n_sel, Q_BLOCK), 0)
        cur = lax.shift_right_logical(t0 + lax.broadcasted_iota(jnp.int32, (n_sel, Q_BLOCK), 1),
                                      int(math.log2(SEL_BLOCK)))
        forced = jnp.logical_or(j_i == 0, jnp.logical_or(j_i == cur, j_i == cur - 1))
        valid = j_i <= cur
        v = jnp.where(forced, SEL_FORCE, jnp.where(valid, imp_t, -SEL_FORCE))
        v_scr[g] = v
        n_chunk = n_sel // SUBLANES
        chunks = [v[k * SUBLANES:(k + 1) * SUBLANES] for k in range(n_chunk)]
        ranks = [jnp.zeros((SUBLANES, Q_BLOCK), jnp.int32) for _ in range(n_chunk)]
        sub_i = lax.broadcasted_iota(jnp.int32, (SUBLANES, Q_BLOCK), 0)
        for i in range(n_sel):
            vi = v_scr[g, i:i + 1, :]
            ki = i // SUBLANES
            for k in range(n_chunk):
                if k > ki:
                    beats = jnp.where(vi >= chunks[k], 1, 0)
                elif k < ki:
                    beats = jnp.where(vi > chunks[k], 1, 0)
                else:
                    beats = jnp.where(sub_i > (i - ki * SUBLANES),
                                      jnp.where(vi >= chunks[k], 1, 0),
                                      jnp.where(vi > chunks[k], 1, 0))
                ranks[k] = ranks[k] + beats
        rank = jnp.concatenate(ranks, axis=0)
        chosen = jnp.logical_and(jnp.logical_and(rank < SEL_TOPK, valid), j_i < 2 * qb)
        bias_t = jnp.where(chosen, 0.0, NEG_INF)
        bias_t = jnp.concatenate([bias_t, jnp.full((LANES - n_sel, Q_BLOCK), NEG_INF, F32)], axis=0)
        bias_f = bias_t.T
        block_used.append(jnp.max(bias_f, axis=0, keepdims=True))
        bias = bias_f.astype(BF16)
        lhs_scr[g] = jnp.concatenate([lhs_pos, jnp.concatenate([bias] * HPG, axis=0)], axis=1)

    for g in range(KV_GROUPS):
        _, _, sd, sw = scores[g]
        o_c = cmp_out[g][0]
        vs_ref = (vs0_ref, vs1_ref)[g]
        vw_ref = (vw0_ref, vw1_ref)[g]
        f0 = _group_lane0(1 - g)

        vd = vs_ref[pl.ds(t0a, Q_BLOCK), :]
        sd = (sd.reshape(HPG, Q_BLOCK, Q_BLOCK) + tri[None]).reshape(HPG * Q_BLOCK, Q_BLOCK)
        m = jnp.max(sd, axis=-1, keepdims=True)
        m_scr[g] = jnp.broadcast_to(m, m_scr.shape[1:])
        acc_scr[g] = jnp.dot(jnp.exp2(sd - m).astype(BF16), vd, preferred_element_type=F32)

        v_w = vw_ref[pl.ds(w0, WIN_KEYS), :]
        sw = (sw.reshape(HPG, Q_BLOCK, WIN_KEYS) + wmask[None]).reshape(HPG * Q_BLOCK, WIN_KEYS)
        e = jnp.exp2(sw - jnp.max(sw, axis=-1, keepdims=True))
        o_w = jnp.dot(e.astype(BF16), v_w, preferred_element_type=F32)
        o_w = o_w * (1.0 / o_w[:, f0 + ONE_LANE:f0 + ONE_LANE + 1])

        for h in range(HPG):
            part_scr[g, hrows[h]] = gate(g, h, 0) * o_c[hrows[h]] + gate(g, h, 2) * o_w[hrows[h]]

    n_key_tiles = hot_ref.shape[0] // Q_BLOCK - 1
    per_step = SEL_TK // Q_BLOCK
    used = jnp.maximum(block_used[0], block_used[1])
    used = jnp.maximum(used, pltpu.roll(used, LANES - 1, axis=1))
    lane1 = lax.broadcasted_iota(jnp.int32, (1, LANES), 1)
    tile_of_lane = lax.shift_right_logical(lane1, 1)
    bit = jnp.where(jnp.logical_and(used > -1.0, (lane1 & 1) == 0),
                    jnp.left_shift(1, tile_of_lane & (MASK_BITS - 1)), 0).astype(F32)
    word = lax.shift_right_logical(tile_of_lane, int(math.log2(MASK_BITS)))
    masks = [jnp.sum(jnp.where(word == w, bit, 0.0), axis=1, keepdims=True)[0, 0].astype(jnp.int32)
             for w in range((n_key_tiles + MASK_BITS - 1) // MASK_BITS)]
    count = jnp.int32(0)
    for j in range(n_key_tiles):
        tile_smem[count] = j
        count = count + (lax.shift_right_logical(masks[j // MASK_BITS], j % MASK_BITS) & 1)
    for k in range(per_step - 1):
        tile_smem[count + k] = n_key_tiles

    def tile_offsets(i, k):
        j = tile_smem[i * per_step + k]
        hot_off = pl.multiple_of(j * Q_BLOCK, Q_BLOCK)
        kv_off = pl.multiple_of(jnp.minimum(j, n_key_tiles - 1) * Q_BLOCK, Q_BLOCK)
        return kv_off, hot_off

    def sel_body(i, carry):
        offs = [tile_offsets(i, k) for k in range(per_step)]
        hot = jnp.concatenate([hot_ref[pl.ds(ho, Q_BLOCK), :] for _, ho in offs], axis=0)
        s_g = []
        for g in range(KV_GROUPS):
            ks_ref = (ks0_ref, ks1_ref)[g]
            k_t = jnp.concatenate([ks_ref[pl.ds(ko, Q_BLOCK), :] for ko, _ in offs], axis=0)
            s_g.append(_nt_dot(lhs_scr[g], jnp.concatenate([k_t, hot], axis=1)))
        for g in range(KV_GROUPS):
            vs_ref = (vs0_ref, vs1_ref)[g]
            v_t = jnp.concatenate([vs_ref[pl.ds(ko, Q_BLOCK), :] for ko, _ in offs], axis=0)
            s = s_g[g]
            m_old = m_scr[g]
            m_new = jnp.maximum(m_old, jnp.max(s, axis=-1, keepdims=True))
            p = jnp.exp2(s - jnp.tile(m_new, (1, SEL_TK // LANES)))
            alpha = jnp.exp2(m_old - m_new)
            m_scr[g] = m_new
            pv = jnp.dot(p.astype(BF16), v_t, preferred_element_type=F32)
            acc_scr[g] = alpha * acc_scr[g] + pv
        return carry

    lax.fori_loop(0, (count + per_step - 1) // per_step, sel_body, 0)

    o_s = []
    for g in range(KV_GROUPS):
        one = _group_lane0(1 - g) + ONE_LANE
        acc = acc_scr[g]
        o_s.append(acc * (1.0 / acc[:, one:one + 1]))
    for h in range(HPG):
        outs = [part_scr[g, hrows[h]] + gate(g, h, 1) * o_s[g][hrows[h]] for g in range(KV_GROUPS)]
        o_ref[:, h * LANES:(h + 1) * LANES] = jnp.where(lane < HEAD_DIM, outs[0], outs[1]).astype(o_ref.dtype)


def _attention_tables(seq, ncp, n_cmp, n_sel):
    cstart = np.arange(ncp) * CMP_STRIDE
    sstart = np.arange(n_sel) * SEL_BLOCK
    overlap = ((cstart[:, None] < sstart[None, :] + SEL_BLOCK)
               & (cstart[:, None] + CMP_BLOCK > sstart[None, :])
               & (np.arange(ncp)[:, None] < n_cmp)).astype(np.float32)
    overlap = np.pad(overlap, ((0, 0), (0, LANES - n_sel)))
    ov4 = np.tile(overlap, (HPG, 1))
    n = np.arange(ncp)
    cfeat = np.zeros((KV_GROUPS, ncp, LANES), np.float32)
    for g in range(KV_GROUPS):
        l0 = _group_lane0(1 - g)
        for k in range(3):
            cfeat[g, :, l0 + F_CHI + k] = n // CMP_RADIX
            cfeat[g, :, l0 + F_CLO + k] = n % CMP_RADIX
            cfeat[g, :, l0 + F_COFF + k] = CMP_BLOCK - 1
    blk = np.concatenate([np.arange(seq) // SEL_BLOCK, np.full(Q_BLOCK, n_sel)])
    hot = (blk[:, None] == np.arange(LANES)[None, :]).astype(np.float32)
    r = np.arange(Q_BLOCK)[:, None]
    tri = np.where(np.arange(Q_BLOCK)[None, :] <= r, 0.0, NEG_INF).astype(np.float32)
    c = np.arange(WIN_KEYS)[None, :]
    wm = []
    for qb in range(WIN_BACK + 1):
        dist = (qb * Q_BLOCK + r) - c if qb < WIN_BACK else (WINDOW + r) - c
        wm.append(np.where((dist >= 0) & (dist < WINDOW), 0.0, NEG_INF))
    wm = np.stack(wm).astype(np.float32)
    n_gate = 3 * NSA_HEADS
    gsel = np.zeros((2 * LANES, n_gate, LANES), np.float32)
    for c in range(n_gate):
        gsel[c, c, :] = 1.0
        gsel[LANES + c, c, :] = 1.0
    gsel = jnp.asarray(gsel.reshape(2 * LANES, n_gate * LANES), BF16)
    return (jnp.asarray(ov4, BF16), jnp.asarray(cfeat, BF16), jnp.asarray(hot, BF16), gsel,
            jnp.asarray(tri), jnp.asarray(wm))


def _attention(zq, kvc, zkv, zg, batch, seq):
    n = zq.shape[0]
    nqb = seq // Q_BLOCK
    ncp = kvc.shape[2]
    n_cmp = seq // CMP_STRIDE - 1
    n_sel = seq // SEL_BLOCK
    assert n_sel <= HEAD_DIM and n_sel % SUBLANES == 0 and SEL_TK % Q_BLOCK == 0 and nqb > WIN_BACK
    assert ncp // CMP_RADIX <= 256 and seq // SEL_BLOCK <= 256
    ov4, cfeat, hot, gsel, tri, wm = _attention_tables(seq, ncp, n_cmp, n_sel)
    out_cols = HPG * LANES

    kernel = functools.partial(_attn_kernel, n_cmp=n_cmp, n_sel=n_sel)
    kv_spec = lambda idx: pl.BlockSpec((seq, KV_SLAB), lambda b, i, idx=idx: (b, idx))
    return pl.pallas_call(
        kernel,
        grid=(batch, nqb),
        in_specs=[pl.BlockSpec((Q_BLOCK, Q_COLS), lambda b, i: (b * nqb + i, 0)),
                  pl.BlockSpec((1, 1, ncp, KV_SLAB), lambda b, i: (0, b, 0, 0)),
                  pl.BlockSpec((1, 1, ncp, KV_SLAB), lambda b, i: (1, b, 0, 0)),
                  *[kv_spec(k) for k in range(8)],
                  pl.BlockSpec((Q_BLOCK, GATE_COLS), lambda b, i: (b * nqb + i, 0)),
                  _const_spec(ov4.shape), _const_spec(cfeat.shape), _const_spec(hot.shape),
                  _const_spec(tri.shape), _const_spec(wm.shape), _const_spec(gsel.shape)],
        out_specs=pl.BlockSpec((Q_BLOCK, out_cols), lambda b, i: (b * nqb + i, 0)),
        out_shape=jax.ShapeDtypeStruct((n, out_cols), BF16),
        scratch_shapes=[pltpu.VMEM((KV_GROUPS, n_sel, Q_BLOCK), F32),
                        pltpu.VMEM((KV_GROUPS, HPG * Q_BLOCK, 2 * LANES), BF16),
                        pltpu.VMEM((KV_GROUPS, HPG * Q_BLOCK, LANES), F32),
                        pltpu.VMEM((KV_GROUPS, HPG * Q_BLOCK, LANES), F32),
                        pltpu.VMEM((KV_GROUPS, HPG * Q_BLOCK, LANES), F32),
                        pltpu.SMEM((seq // Q_BLOCK + SEL_TK // Q_BLOCK,), jnp.int32)],
        compiler_params=_cparams(("arbitrary", "arbitrary")),
        name="nsa_attn",
    )(zq, kvc, kvc, *([zkv] * 8), zg, ov4, cfeat, hot, tri, wm, gsel)


def _s5_kernel(u_ref, lagk_ref, mre_ref, mim_ref, cre_ref, cim_ref, lre_ref, lim_ref, d_ref, y_ref,
               ere, eim, xre, xim, tp_scr, *, n_chunks):
    def tok(t):
        return pl.ds(t, n_chunks, stride=S5_CHUNK)

    @pl.when(pl.program_id(1) == 0)
    def _():
        tp_scr[...] = jnp.zeros(tp_scr.shape, tp_scr.dtype)
        for s in range(S5_CHUNK):
            for t in range(s, S5_CHUNK):
                tp_scr[s * LANES:(s + 1) * LANES, t * LANES:(t + 1) * LANES] = lagk_ref[0, t - s]

    u = jnp.concatenate([u_ref[tok(t), :] for t in range(S5_CHUNK)], axis=1).astype(BF16)
    y = jnp.dot(u, tp_scr[...], preferred_element_type=F32)
    ere[...] = jnp.dot(u, mre_ref[0, 0], preferred_element_type=F32)
    eim[...] = jnp.dot(u, mim_ref[0, 0], preferred_element_type=F32)
    lr = lre_ref[0]
    li = lim_ref[0]

    def body(c, carry):
        xr, xi = carry
        row = pl.ds(c, 1)
        xre[row, :] = xr
        xim[row, :] = xi
        er = ere[row, :]
        ei = eim[row, :]
        return lr * xr - li * xi + er, lr * xi + li * xr + ei

    zero = jnp.zeros((1, ere.shape[1]), F32)
    lax.fori_loop(0, n_chunks, body, (zero, zero), unroll=8)
    y = y + jnp.dot(xre[...].astype(BF16), cre_ref[0, 0], preferred_element_type=F32)
    y = y + jnp.dot(xim[...].astype(BF16), cim_ref[0, 0], preferred_element_type=F32)
    d = d_ref[0]
    for t in range(S5_CHUNK):
        y_ref[tok(t), :] = y[:, t * LANES:(t + 1) * LANES] + d * u_ref[tok(t), :]


def _s5_matrices(a_re, a_im, b_re, b_im, c_re, c_im, d_skip, log_dt):
    t = S5_CHUNK
    hp = lax.Precision.HIGHEST
    dt = jnp.exp(log_dt)[:, None]
    lam_re = jnp.minimum(a_re, -1e-4)
    lam_im = a_im
    mag = jnp.exp(lam_re * dt)
    ang = lam_im * dt
    lb_re = mag * jnp.cos(ang)
    lb_im = mag * jnp.sin(ang)
    den = lam_re * lam_re + lam_im * lam_im
    nr = lb_re - 1.0
    coef_re = (nr * lam_re + lb_im * lam_im) / den
    coef_im = (lb_im * lam_re - nr * lam_im) / den
    bb_re = coef_re[..., None] * b_re - coef_im[..., None] * b_im
    bb_im = coef_re[..., None] * b_im + coef_im[..., None] * b_re
    j = jnp.arange(t + 1, dtype=F32)[:, None, None]
    pmag = jnp.exp(j * (lam_re * dt)[None])
    pw_re = pmag * jnp.cos(j * ang[None])
    pw_im = pmag * jnp.sin(j * ang[None])
    cl_re = c_re[None] * pw_re[:, :, None, :] - c_im[None] * pw_im[:, :, None, :]
    cl_im = c_re[None] * pw_im[:, :, None, :] + c_im[None] * pw_re[:, :, None, :]
    kern = (jnp.einsum('jgop,gpi->gjoi', cl_re[:t], bb_re, precision=hp)
            - jnp.einsum('jgop,gpi->gjoi', cl_im[:t], bb_im, precision=hp))
    eye = jnp.eye(S5_LB, dtype=F32)
    nlb = S5_GROUPS // S5_LB
    lagk = jnp.einsum('qgjoi,gh->qjgiho', kern.reshape(nlb, S5_LB, t, S5_GROUP, S5_GROUP), eye,
                      precision=hp).reshape(nlb, t, LANES, LANES)
    rv_re = pw_re[t - 1 - np.arange(t)]
    rv_im = pw_im[t - 1 - np.arange(t)]
    m_re = rv_re[..., None] * bb_re[None] - rv_im[..., None] * bb_im[None]
    m_im = rv_re[..., None] * bb_im[None] + rv_im[..., None] * bb_re[None]

    def lay_m(m):
        m = m.reshape(2, t, nlb, S5_LB, S5_STATE, S5_GROUP)
        return jnp.einsum('csqgpi,gh->cqsgihp', m, eye, precision=hp).reshape(
            2, nlb, t * LANES, S5_LB * S5_STATE)

    def lay_c(c):
        c = c.reshape(2, t, nlb, S5_LB, S5_GROUP, S5_STATE)
        return jnp.einsum('ctqgop,gh->cqgptho', c, eye, precision=hp).reshape(
            2, nlb, S5_LB * S5_STATE, t * LANES)

    lam_t_re = pw_re[t].reshape(nlb, 1, S5_LB * S5_STATE)
    lam_t_im = pw_im[t].reshape(nlb, 1, S5_LB * S5_STATE)
    d_row = d_skip.reshape(nlb, 1, LANES)
    m_ops = lay_m(jnp.stack([m_re, m_im])).astype(BF16)
    c_ops = lay_c(jnp.stack([cl_re[1:], -cl_im[1:]])).astype(BF16)
    return lagk.astype(BF16), m_ops, c_ops, lam_t_re, lam_t_im, d_row


def _s5(zs, mats, batch, seq):
    n_chunks = seq // S5_CHUNK
    lagk, m_ops, c_ops, l_re, l_im, d_row = mats
    nlb = lagk.shape[0]
    wcol = S5_CHUNK * LANES
    wst = S5_LB * S5_STATE
    kernel = functools.partial(_s5_kernel, n_chunks=n_chunks)
    p3 = lambda r, c: pl.BlockSpec((1, r, c), lambda q, b: (q, 0, 0))
    part = lambda k, r, c: pl.BlockSpec((1, 1, r, c), lambda q, b, k=k: (k, q, 0, 0))
    return pl.pallas_call(
        kernel,
        grid=(nlb, batch),
        in_specs=[pl.BlockSpec((seq, LANES), lambda q, b: (b, q)),
                  pl.BlockSpec((1, S5_CHUNK, LANES, LANES), lambda q, b: (q, 0, 0, 0)),
                  part(0, wcol, wst), part(1, wcol, wst), part(0, wst, wcol), part(1, wst, wcol),
                  p3(1, wst), p3(1, wst), p3(1, LANES)],
        out_specs=pl.BlockSpec((seq, LANES), lambda q, b: (b, q)),
        out_shape=jax.ShapeDtypeStruct(zs.shape, F32),
        scratch_shapes=[pltpu.VMEM((n_chunks, wst), F32) for _ in range(4)]
        + [pltpu.VMEM((wcol, wcol), BF16)],
        compiler_params=_cparams(("arbitrary", "arbitrary")),
        name="s5",
    )(zs, lagk, m_ops, m_ops, c_ops, c_ops, l_re, l_im, d_row)


def _merge_kernel(oa_ref, ys_ref, zm_ref, x_ref, mod_ref, wn_ref, wg_ref, wo_ref, g_ref, b_ref, o_ref,
                  *, alpha):
    d = x_ref.shape[-1]
    y_a = jnp.dot(oa_ref[...], wn_ref[...], preferred_element_type=F32)
    gl = jax.nn.gelu(ys_ref[...], approximate=True)
    zz = jnp.dot(gl.astype(BF16), wg_ref[...], preferred_element_type=F32)
    y_b = zz[:, :d] * jax.nn.sigmoid(zz[:, d:])
    zm = zm_ref[...]
    mix_in = jax.nn.sigmoid(zm[:, :d]) * y_a + jax.nn.sigmoid(zm[:, d:]) * y_b
    mix = jnp.dot(mix_in.astype(BF16), wo_ref[...], preferred_element_type=F32)
    gate = mod_ref[0, 2:3, :]
    r = alpha * x_ref[...] + gate * mix
    o_ref[...] = _layer_norm(r) * g_ref[...] + b_ref[...]


def _merge(oa, ys, zm, x2, mod3, wn_pad, wg, wo, ln_g, ln_b, seq, alpha):
    n, d = x2.shape
    tiles_per_batch = seq // TM_MERGE
    row = lambda w: pl.BlockSpec((TM_MERGE, w), lambda i: (i, 0))
    return pl.pallas_call(
        functools.partial(_merge_kernel, alpha=alpha),
        grid=(n // TM_MERGE,),
        in_specs=[row(oa.shape[1]), row(ys.shape[1]), row(zm.shape[1]), row(d),
                  pl.BlockSpec((1, 6, d), lambda i: (i // tiles_per_batch, 0, 0)),
                  _const_spec(wn_pad.shape), _const_spec(wg.shape), _const_spec(wo.shape),
                  _const_spec((1, d)), _const_spec((1, d))],
        out_specs=row(d),
        out_shape=jax.ShapeDtypeStruct((n, d), F32),
        compiler_params=_cparams(("arbitrary",)),
        name="merge",
    )(oa, ys, zm, x2, mod3, wn_pad, wg, wo, ln_g.reshape(1, d), ln_b.reshape(1, d))


def _pad_nsa_out(w):
    rows = []
    for h in range(HPG):
        for g in range(KV_GROUPS):
            hd = g * HPG + h
            rows.append(w[hd * HEAD_DIM:(hd + 1) * HEAD_DIM])
    return jnp.concatenate(rows, axis=0).astype(BF16)


def _ffn_kernel(x_ref, mod_ref, wup_ref, cw_ref, cb_ref, wdn_ref, g_ref, b_ref, o_ref, tail_ref, act_scr,
                *, alpha, tiles_per_batch):
    i = pl.program_id(0)
    tm = x_ref.shape[0]

    @pl.when(i % tiles_per_batch == 0)
    def _():
        tail_ref[...] = jnp.zeros(tail_ref.shape, F32)

    shift = mod_ref[0, 3:4, :]
    scale = mod_ref[0, 4:5, :]
    gate = mod_ref[0, 5:6, :]
    row = lax.broadcasted_iota(jnp.int32, (SUBLANES, FF_CHUNK), 0)

    def shift_rows(a, prev_rows):
        rolled = pltpu.roll(a, 1, axis=0)
        head = jnp.where(row == 0, prev_rows[SUBLANES - 1:SUBLANES], rolled[:SUBLANES])
        return jnp.concatenate([head, rolled[SUBLANES:]], axis=0)

    part_rows = tm // FFN_PARTS
    for part in range(FFN_PARTS):
        rows = slice(part * part_rows, (part + 1) * part_rows)
        x = x_ref[rows, :]
        h2 = (_layer_norm(x) * (1.0 + scale) + shift).astype(BF16)

        def conv_cols(c0):
            cols = slice(c0, c0 + FF_CHUNK)
            up = jnp.dot(h2, wup_ref[:, cols], preferred_element_type=F32)
            tail = tail_ref[:, cols]
            tail_ref[:, cols] = up[part_rows - SUBLANES:part_rows]
            w = cw_ref[:, cols]
            inner = shift_rows(w[0:1] * up, w[0:1] * tail) + w[1:2] * up
            prev_inner = pltpu.roll(w[0:1] * tail, 1, axis=0) + w[1:2] * tail
            return shift_rows(inner, prev_inner) + w[2:3] * up + cb_ref[:, cols]

        for k in range(D_FF // FF_CHUNK):
            val = conv_cols(k * FF_CHUNK)
            gte = conv_cols(D_FF + k * FF_CHUNK)
            act_scr[rows, k * FF_CHUNK:(k + 1) * FF_CHUNK] = (gte * jax.nn.sigmoid(gte) * val).astype(BF16)
        ff = jnp.dot(act_scr[rows, :], wdn_ref[...], preferred_element_type=F32)
        r = alpha * x + gate * ff
        o_ref[rows, :] = _layer_norm(r) * g_ref[...] + b_ref[...]


def _ffn(x1, mod3, wup, conv_w, conv_b, wdn, ln_g, ln_b, seq, alpha):
    n, d = x1.shape
    tiles_per_batch = seq // TM_FFN
    ff2 = wup.shape[1]
    return pl.pallas_call(
        functools.partial(_ffn_kernel, alpha=alpha, tiles_per_batch=tiles_per_batch),
        grid=(n // TM_FFN,),
        in_specs=[pl.BlockSpec((TM_FFN, d), lambda i: (i, 0)),
                  pl.BlockSpec((1, 6, d), lambda i: (i // tiles_per_batch, 0, 0)),
                  _const_spec(wup.shape), _const_spec(conv_w.shape), _const_spec((1, ff2)),
                  _const_spec(wdn.shape), _const_spec((1, d)), _const_spec((1, d))],
        out_specs=pl.BlockSpec((TM_FFN, d), lambda i: (i, 0)),
        out_shape=jax.ShapeDtypeStruct((n, d), F32),
        scratch_shapes=[pltpu.VMEM((SUBLANES, ff2), F32), pltpu.VMEM((TM_FFN, ff2 // 2), BF16)],
        compiler_params=_cparams(("arbitrary",)),
        name="ffn",
    )(x1, mod3, wup, conv_w, conv_b.reshape(1, ff2), wdn, ln_g.reshape(1, d), ln_b.reshape(1, d))


def kernel(x, c, w_ada, b_ada, w_in, pe_ck, w_ck1, w_ck2, pe_cv, w_cv1, w_cv2, w_nsa_out,
           s5_a_re, s5_a_im, s5_b_re, s5_b_im, s5_c_re, s5_c_im, s5_d, s5_log_dt, w_s5_glu,
           w_o, ln1_g, ln1_b, w_up, conv_w, conv_b, w_down, ln2_g, ln2_b):
    batch, seq, d = x.shape
    depth = w_ada.shape[0]
    alpha = (2.0 * depth) ** 0.25
    n = batch * seq
    n_chunk16 = seq // CMP_STRIDE
    xf = x.reshape(n, d)
    for l in range(depth):
        mod3 = _ada(c, w_ada[l], b_ada[l]).reshape(batch, 6, d)

        zq, zc, zkv, zg, zs, zm = _inproj(xf, mod3, _build_w_all(w_in[l]), seq)

        wk1, pek, wk2 = _cmp_weights(pe_ck[l], w_ck1[l], w_ck2[l])
        wv1, pev, wv2 = _cmp_weights(pe_cv[l], w_cv1[l], w_cv2[l])
        ch = zc.reshape(2, batch, n_chunk16, CMP_STRIDE * KV_SLAB)
        kvc = _compress(ch, jnp.stack([wk1, wv1]), jnp.stack([pek, pev]), jnp.stack([wk2, wv2]))

        oa = _attention(zq, kvc, zkv, zg, batch, seq)

        mats = _s5_matrices(s5_a_re[l], s5_a_im[l], s5_b_re[l], s5_b_im[l], s5_c_re[l], s5_c_im[l],
                            s5_d[l], s5_log_dt[l])
        ys = _s5(zs, mats, batch, seq)

        x1 = _merge(oa, ys, zm, xf, mod3, _pad_nsa_out(w_nsa_out[l]), w_s5_glu[l].astype(BF16),
                    w_o[l].astype(BF16), ln1_g[l], ln1_b[l], seq, alpha)

        xf = _ffn(x1, mod3, w_up[l].astype(BF16), conv_w[l], conv_b[l], w_down[l].astype(BF16),
                  ln2_g[l], ln2_b[l], seq, alpha)
    return xf.reshape(batch, seq, d)
```

```python
import functools
import math

import jax
import jax.numpy as jnp
import ml_dtypes
import numpy as np
from jax import lax
from jax.experimental import pallas as pl
from jax.experimental.pallas import tpu as pltpu

F32 = jnp.float32
BF16 = jnp.bfloat16

D_MODEL = 1024
NSA_HEADS = 8
KV_GROUPS = 2
HPG = NSA_HEADS // KV_GROUPS
HEAD_DIM = 64
CMP_BLOCK = 32
CMP_STRIDE = 16
CMP_HIDDEN = 128
SEL_BLOCK = 64
SEL_TOPK = 16
WINDOW = 512
Q_BLOCK = 128
S5_GROUP = 16
S5_WIDTH = 512
S5_GROUPS = S5_WIDTH // S5_GROUP
S5_STATE = 64
D_FF = 2816
CONV_WIDTH = 3
LN_EPS = 1e-5
NEG_INF = -1e30
SEL_FORCE = 1e9

LANES = 128
SUBLANES = 8
VMEM_LIMIT = 56 * 1024 * 1024

TM_IN = 512
TM_MERGE = 512
TM_FFN = 512
FF_CHUNK = 256
FFN_PARTS = 2
SEL_TK = 768
WIN_KEYS = WINDOW + Q_BLOCK
WIN_BACK = WINDOW // Q_BLOCK
S5_CHUNK = 8
S5_LB = LANES // S5_GROUP
ADA_COLS = 1024

Q_COLS = NSA_HEADS * LANES
KV_SLAB = KV_GROUPS * HEAD_DIM
GATE_COLS = LANES
MERGE_COLS = 2 * D_MODEL
KV_OUT_COLS = 8 * KV_SLAB
F_HI, F_LO = 0, 3
ONE_LANE = 6
F_CHI, F_CLO, F_COFF = 7, 10, 13
CMP_RADIX = 16
LOG2E = 1.4426950408889634
MASK_BITS = 16


def _cparams(sem):
    return pltpu.CompilerParams(dimension_semantics=sem, vmem_limit_bytes=VMEM_LIMIT)


def _const_spec(shape):
    n = len(shape)
    return pl.BlockSpec(shape, lambda *_: (0,) * n)


def _layer_norm(x):
    mu = jnp.mean(x, axis=-1, keepdims=True)
    xc = x - mu
    var = jnp.mean(xc * xc, axis=-1, keepdims=True)
    return xc * lax.rsqrt(var + LN_EPS)


def _nt_dot(a, b):
    return lax.dot_general(a, b, (((1,), (1,)), ((), ())), preferred_element_type=F32)


def _group_lane0(g):
    return g * HEAD_DIM


def _ada_kernel(c_ref, w_ref, b_ref, o_ref):
    c = c_ref[...]
    a = c * jax.nn.sigmoid(c)
    o_ref[...] = jnp.dot(a.astype(BF16), w_ref[...].astype(BF16), preferred_element_type=F32) + b_ref[...]


def _ada(c, w_ada, b_ada):
    b, d = c.shape
    n = w_ada.shape[1]
    blk = ADA_COLS
    return pl.pallas_call(
        _ada_kernel,
        grid=(n // blk,),
        in_specs=[pl.BlockSpec((b, d), lambda j: (0, 0)),
                  pl.BlockSpec((d, blk), lambda j: (0, j)),
                  pl.BlockSpec((1, blk), lambda j: (0, j))],
        out_specs=pl.BlockSpec((b, blk), lambda j: (0, j)),
        out_shape=jax.ShapeDtypeStruct((b, n), F32),
        compiler_params=_cparams(("arbitrary",)),
        name="ada",
    )(c, w_ada, b_ada.reshape(1, n))


def _inproj_kernel(x_ref, mod_ref, w_ref, pf_ref, qf_ref, q_ref, c_ref, kv_ref, g_ref, s_ref, m_ref, zc_scr):
    hn = _layer_norm(x_ref[...])
    shift = mod_ref[0, 0:1, :]
    scale = mod_ref[0, 1:2, :]
    h = (hn * (1.0 + scale) + shift).astype(BF16)
    tm = x_ref.shape[0]

    def proj(col, width):
        return jnp.dot(h, w_ref[:, col:col + width], preferred_element_type=F32)

    col = 0
    q_ref[...] = (proj(col, Q_COLS) + qf_ref[...]).astype(q_ref.dtype)
    col += Q_COLS
    for kind in range(2):
        zc_scr[...] = proj(col, KV_SLAB)
        for tok in range(CMP_STRIDE):
            c_ref[kind, :, tok * KV_SLAB:(tok + 1) * KV_SLAB] = zc_scr[
                pl.ds(tok, tm // CMP_STRIDE, stride=CMP_STRIDE), :]
        col += KV_SLAB
    lane = lax.broadcasted_iota(jnp.int32, (x_ref.shape[0], LANES), 1)
    pf = pf_ref[...]
    for part in range(4):
        z = proj(col + part * KV_SLAB, KV_SLAB)
        for g in range(KV_GROUPS):
            own = (lane < HEAD_DIM) if g == 0 else (lane >= HEAD_DIM)
            slot = (part * KV_GROUPS + g) * KV_SLAB
            kv_ref[:, slot:slot + KV_SLAB] = jnp.where(
                own, z, pf[:, g * LANES:(g + 1) * LANES]).astype(kv_ref.dtype)
    col += 4 * KV_SLAB
    g_ref[...] = proj(col, GATE_COLS)
    col += GATE_COLS
    s_ref[...] = proj(col, S5_WIDTH)
    col += S5_WIDTH
    m_ref[...] = proj(col, MERGE_COLS)


def _log2e_terms():
    terms, rest = [], np.float64(LOG2E)
    for _ in range(3):
        t = np.float64(np.float32(rest).astype(ml_dtypes.bfloat16))
        terms.append(float(t))
        rest -= t
    return terms


def _key_position_features(seq):
    p = np.arange(seq)
    out = np.zeros((seq, KV_GROUPS, LANES), np.float32)
    for g in range(KV_GROUPS):
        l0 = _group_lane0(1 - g)
        for k in range(3):
            out[:, g, l0 + F_HI + k] = p // SEL_BLOCK
            out[:, g, l0 + F_LO + k] = p % SEL_BLOCK
        out[:, g, l0 + ONE_LANE] = 1.0
    return jnp.asarray(out.reshape(seq, KV_GROUPS * LANES))


def _query_feature_row():
    c = _log2e_terms()
    row = np.zeros((NSA_HEADS, LANES), np.float32)
    for hd in range(NSA_HEADS):
        g = hd // HPG
        slope = 2.0 ** -(hd + 1)
        l0 = _group_lane0(1 - g)
        for k in range(3):
            row[hd, l0 + F_HI + k] = SEL_BLOCK * slope * c[k]
            row[hd, l0 + F_LO + k] = slope * c[k]
            row[hd, l0 + F_CHI + k] = CMP_RADIX * CMP_STRIDE * slope * c[k]
            row[hd, l0 + F_CLO + k] = CMP_STRIDE * slope * c[k]
            row[hd, l0 + F_COFF + k] = slope * c[k]
    return jnp.asarray(row.reshape(1, Q_COLS))


def _inproj(x2, mod3, w_all, seq):
    n, d = x2.shape
    tiles_per_batch = seq // TM_IN
    rows16 = TM_IN // CMP_STRIDE
    widths = (Q_COLS, KV_OUT_COLS, GATE_COLS, S5_WIDTH, MERGE_COLS)
    dtypes = (BF16, BF16, F32, F32, F32)
    pf = _key_position_features(seq)
    qf = _query_feature_row()
    row_spec = lambda w: pl.BlockSpec((TM_IN, w), lambda i: (i, 0))
    out_specs = [row_spec(Q_COLS),
                 pl.BlockSpec((2, rows16, CMP_STRIDE * KV_SLAB), lambda i: (0, i, 0))]
    out_specs += [row_spec(w) for w in widths[1:]]
    out_shape = [jax.ShapeDtypeStruct((n, Q_COLS), BF16),
                 jax.ShapeDtypeStruct((2, n // CMP_STRIDE, CMP_STRIDE * KV_SLAB), F32)]
    out_shape += [jax.ShapeDtypeStruct((n, w), dt) for w, dt in zip(widths[1:], dtypes[1:])]
    return pl.pallas_call(
        _inproj_kernel,
        grid=(n // TM_IN,),
        in_specs=[pl.BlockSpec((TM_IN, d), lambda i: (i, 0)),
                  pl.BlockSpec((1, 6, d), lambda i: (i // tiles_per_batch, 0, 0)),
                  _const_spec(w_all.shape),
                  pl.BlockSpec((TM_IN, KV_GROUPS * LANES), lambda i: (i % tiles_per_batch, 0)),
                  _const_spec(qf.shape)],
        out_specs=out_specs,
        out_shape=out_shape,
        scratch_shapes=[pltpu.VMEM((TM_IN, KV_SLAB), F32)],
        compiler_params=_cparams(("arbitrary",)),
        name="inproj",
    )(x2, mod3, w_all, pf, qf)


def _build_w_all(w_in):
    d = w_in.shape[0]
    cq = NSA_HEADS * HEAD_DIM
    ckv = 6 * KV_SLAB
    cg = 3 * NSA_HEADS
    zeros = jnp.zeros((d, HEAD_DIM), w_in.dtype)
    pieces = []
    for hd in range(NSA_HEADS):
        wq = w_in[:, hd * HEAD_DIM:(hd + 1) * HEAD_DIM] * (HEAD_DIM ** -0.5 * LOG2E)
        pieces += [wq, zeros] if hd < HPG else [zeros, wq]
    wq_pad = jnp.concatenate(pieces, axis=1)
    wkv = w_in[:, cq:cq + ckv]
    wg = jnp.pad(w_in[:, cq + ckv:cq + ckv + cg], ((0, 0), (0, GATE_COLS - cg)))
    rest = w_in[:, cq + ckv + cg:]
    return jnp.concatenate([wq_pad, wkv, wg, rest], axis=1).astype(BF16)


def _cmp_kernel(ch_ref, w1_ref, pe_ref, w2_ref, o_ref):
    ch = ch_ref[0, 0]
    a = jnp.dot((ch + pe_ref[0, 0]).astype(BF16), w1_ref[0, 0], preferred_element_type=F32)
    b = jnp.dot((ch + pe_ref[0, 1]).astype(BF16), w1_ref[0, 1], preferred_element_type=F32)
    n = a.shape[0]
    hsum = a + pltpu.roll(b, n - 1, axis=0)
    hact = hsum * jax.nn.sigmoid(hsum)
    o_ref[0, 0] = jnp.dot(hact.astype(BF16), w2_ref[0], preferred_element_type=F32).astype(o_ref.dtype)


def _cmp_weights(pe, w1, w2):
    half = CMP_BLOCK // 2
    halves = []
    pes = []
    for lo in (0, half):
        w = w1[lo:lo + half]
        z = jnp.zeros_like(w)
        w_g0 = jnp.concatenate([w, z], axis=-1)
        w_g1 = jnp.concatenate([z, w], axis=-1)
        halves.append(jnp.stack([w_g0, w_g1], axis=1).reshape(half * KV_SLAB, KV_GROUPS * CMP_HIDDEN))
        p = pe[lo:lo + half]
        pes.append(jnp.broadcast_to(p[:, None, :], (half, KV_GROUPS, HEAD_DIM)).reshape(1, half * KV_SLAB))
    z2 = jnp.zeros_like(w2)
    w2_blk = jnp.concatenate([jnp.concatenate([w2, z2], axis=1), jnp.concatenate([z2, w2], axis=1)], axis=0)
    return jnp.stack(halves).astype(BF16), jnp.stack(pes), w2_blk.astype(BF16)


def _compress(ch, w1s, pes, w2s):
    kinds, b, nchunk, width = ch.shape
    hid = w1s.shape[-1]
    return pl.pallas_call(
        _cmp_kernel,
        grid=(kinds, b),
        in_specs=[pl.BlockSpec((1, 1, nchunk, width), lambda k, i: (k, i, 0, 0)),
                  pl.BlockSpec((1, 2, width, hid), lambda k, i: (k, 0, 0, 0)),
                  pl.BlockSpec((1, 2, 1, width), lambda k, i: (k, 0, 0, 0)),
                  pl.BlockSpec((1, hid, KV_SLAB), lambda k, i: (k, 0, 0))],
        out_specs=pl.BlockSpec((1, 1, nchunk, KV_SLAB), lambda k, i: (k, i, 0, 0)),
        out_shape=jax.ShapeDtypeStruct((kinds, b, nchunk, KV_SLAB), BF16),
        compiler_params=_cparams(("arbitrary", "arbitrary")),
        name="compress",
    )(ch, w1s, pes, w2s)


def _softmax_parts(s):
    m = jnp.max(s, axis=-1, keepdims=True)
    e = jnp.exp2(s - m)
    return m, e, jnp.sum(e, axis=-1, keepdims=True)


def _attn_kernel(q_ref, kc_ref, vc_ref, ks0_ref, ks1_ref, vs0_ref, vs1_ref, kw0_ref, kw1_ref,
                 vw0_ref, vw1_ref, zg_ref,
                 ov_ref, cf_ref, hot_ref, tri_ref, wm_ref, gsel_ref, o_ref, v_scr, lhs_scr, m_scr, acc_scr,
                 part_scr, tile_smem,
                 *, n_cmp, n_sel):
    qb = pl.program_id(1)
    t0 = qb * Q_BLOCK
    ncp = kc_ref.shape[2]
    gates = jax.nn.sigmoid(zg_ref[...])
    g_hi = gates.astype(BF16)
    g_lo = (gates - g_hi.astype(F32)).astype(BF16)
    gates_b = jnp.dot(jnp.concatenate([g_hi, g_lo], axis=1), gsel_ref[...], preferred_element_type=F32)

    def gate(g, h, branch):
        c = (g * HPG + h) * 3 + branch
        return gates_b[:, c * LANES:(c + 1) * LANES]
    r_col = lax.broadcasted_iota(jnp.int32, (Q_BLOCK, 1), 0)
    lane = lax.broadcasted_iota(jnp.int32, (Q_BLOCK, LANES), 1)
    tri = tri_ref[...]
    wmask = wm_ref[jnp.minimum(qb, WIN_BACK)]
    hrows = [slice(h * Q_BLOCK, (h + 1) * Q_BLOCK) for h in range(HPG)]

    t0a = pl.multiple_of(t0, Q_BLOCK)
    w0 = pl.multiple_of(jnp.maximum(qb - WIN_BACK, 0) * Q_BLOCK, Q_BLOCK)

    scores = []
    for g in range(KV_GROUPS):
        ks_ref = (ks0_ref, ks1_ref)[g]
        kw_ref = (kw0_ref, kw1_ref)[g]
        own_c = lax.broadcasted_iota(jnp.int32, (ncp, LANES), 1)
        own_c = (own_c < HEAD_DIM) if g == 0 else (own_c >= HEAD_DIM)
        lhs_pos = jnp.concatenate(
            [q_ref[:, (g * HPG + h) * LANES:(g * HPG + h + 1) * LANES] for h in range(HPG)], axis=0)
        kc_aug = jnp.where(own_c, kc_ref[0, 0], cf_ref[g])
        scores.append([lhs_pos, _nt_dot(lhs_pos, kc_aug)])
    for g in range(KV_GROUPS):
        ks_ref = (ks0_ref, ks1_ref)[g]
        kw_ref = (kw0_ref, kw1_ref)[g]
        lhs_pos = scores[g][0]
        scores[g].append(_nt_dot(lhs_pos, ks_ref[pl.ds(t0a, Q_BLOCK), :]))
        scores[g].append(_nt_dot(lhs_pos, kw_ref[pl.ds(w0, WIN_KEYS), :]))

    block_used = []
    cmp_out = []
    for g in range(KV_GROUPS):
        sc = scores[g][1]

        n_i = lax.broadcasted_iota(jnp.int32, (Q_BLOCK, ncp), 1)
        r_i = lax.broadcasted_iota(jnp.int32, (Q_BLOCK, ncp), 0)
        valid_c = jnp.logical_and((n_i * CMP_STRIDE + (CMP_BLOCK - 1)) <= (t0 + r_i), n_i < n_cmp)
        row_any = ((t0 + r_col) >= (CMP_BLOCK - 1)).astype(F32)
        sc = jnp.where(valid_c[None], sc.reshape(HPG, Q_BLOCK, ncp), NEG_INF)
        _, e, l = _softmax_parts(sc)
        p_c = (e * (row_any[None] / l)).astype(BF16)
        o_c = jnp.dot(p_c.reshape(HPG * Q_BLOCK, ncp), vc_ref[0, 0], preferred_element_type=F32)
        imp = jnp.dot(jnp.concatenate([p_c[h] for h in range(HPG)], axis=1), ov_ref[...],
                      preferred_element_type=F32)
        cmp_out.append((o_c, imp))

    for g in range(KV_GROUPS):
        lhs_pos = scores[g][0]
        imp = cmp_out[g][1]

        imp_t = imp.T[:n_sel]
        j_i = lax.broadcasted_iota(jnp.int32, (n_sel, Q_BLOCK), 0)
        cur = lax.shift_right_logical(t0 + lax.broadcasted_iota(jnp.int32, (n_sel, Q_BLOCK), 1),
                                      int(math.log2(SEL_BLOCK)))
        forced = jnp.logical_or(j_i == 0, jnp.logical_or(j_i == cur, j_i == cur - 1))
        valid = j_i <= cur
        v = jnp.where(forced, SEL_FORCE, jnp.where(valid, imp_t, -SEL_FORCE))
        v_scr[g] = v
        n_chunk = n_sel // SUBLANES
        chunks = [v[k * SUBLANES:(k + 1) * SUBLANES] for k in range(n_chunk)]
        ranks = [jnp.zeros((SUBLANES, Q_BLOCK), jnp.int32) for _ in range(n_chunk)]
        sub_i = lax.broadcasted_iota(jnp.int32, (SUBLANES, Q_BLOCK), 0)
        for i in range(n_sel):
            vi = v_scr[g, i:i + 1, :]
            ki = i // SUBLANES
            for k in range(n_chunk):
                if k > ki:
                    beats = jnp.where(vi >= chunks[k], 1, 0)
                elif k < ki:
                    beats = jnp.where(vi > chunks[k], 1, 0)
                else:
                    beats = jnp.where(sub_i > (i - ki * SUBLANES),
                                      jnp.where(vi >= chunks[k], 1, 0),
                                      jnp.where(vi > chunks[k], 1, 0))
                ranks[k] = ranks[k] + beats
        rank = jnp.concatenate(ranks, axis=0)
        chosen = jnp.logical_and(jnp.logical_and(rank < SEL_TOPK, valid), j_i < 2 * qb)
        bias_t = jnp.where(chosen, 0.0, NEG_INF)
        bias_t = jnp.concatenate([bias_t, jnp.full((LANES - n_sel, Q_BLOCK), NEG_INF, F32)], axis=0)
        bias_f = bias_t.T
        block_used.append(jnp.max(bias_f, axis=0, keepdims=True))
        bias = bias_f.astype(BF16)
        lhs_scr[g] = jnp.concatenate([lhs_pos, jnp.concatenate([bias] * HPG, axis=0)], axis=1)

    for g in range(KV_GROUPS):
        _, _, sd, sw = scores[g]
        o_c = cmp_out[g][0]
        vs_ref = (vs0_ref, vs1_ref)[g]
        vw_ref = (vw0_ref, vw1_ref)[g]
        f0 = _group_lane0(1 - g)

        vd = vs_ref[pl.ds(t0a, Q_BLOCK), :]
        sd = (sd.reshape(HPG, Q_BLOCK, Q_BLOCK) + tri[None]).reshape(HPG * Q_BLOCK, Q_BLOCK)
        m = jnp.max(sd, axis=-1, keepdims=True)
        m_scr[g] = jnp.broadcast_to(m, m_scr.shape[1:])
        acc_scr[g] = jnp.dot(jnp.exp2(sd - m).astype(BF16), vd, preferred_element_type=F32)

        v_w = vw_ref[pl.ds(w0, WIN_KEYS), :]
        sw = (sw.reshape(HPG, Q_BLOCK, WIN_KEYS) + wmask[None]).reshape(HPG * Q_BLOCK, WIN_KEYS)
        e = jnp.exp2(sw - jnp.max(sw, axis=-1, keepdims=True))
        o_w = jnp.dot(e.astype(BF16), v_w, preferred_element_type=F32)
        o_w = o_w * (1.0 / o_w[:, f0 + ONE_LANE:f0 + ONE_LANE + 1])

        for h in range(HPG):
            part_scr[g, hrows[h]] = gate(g, h, 0) * o_c[hrows[h]] + gate(g, h, 2) * o_w[hrows[h]]

    n_key_tiles = hot_ref.shape[0] // Q_BLOCK - 1
    per_step = SEL_TK // Q_BLOCK
    used = jnp.maximum(block_used[0], block_used[1])
    used = jnp.maximum(used, pltpu.roll(used, LANES - 1, axis=1))
    lane1 = lax.broadcasted_iota(jnp.int32, (1, LANES), 1)
    tile_of_lane = lax.shift_right_logical(lane1, 1)
    bit = jnp.where(jnp.logical_and(used > -1.0, (lane1 & 1) == 0),
                    jnp.left_shift(1, tile_of_lane & (MASK_BITS - 1)), 0).astype(F32)
    word = lax.shift_right_logical(tile_of_lane, int(math.log2(MASK_BITS)))
    masks = [jnp.sum(jnp.where(word == w, bit, 0.0), axis=1, keepdims=True)[0, 0].astype(jnp.int32)
             for w in range((n_key_tiles + MASK_BITS - 1) // MASK_BITS)]
    count = jnp.int32(0)
    for j in range(n_key_tiles):
        tile_smem[count] = j
        count = count + (lax.shift_right_logical(masks[j // MASK_BITS], j % MASK_BITS) & 1)
    for k in range(per_step - 1):
        tile_smem[count + k] = n_key_tiles

    def tile_offsets(i, k):
        j = tile_smem[i * per_step + k]
        hot_off = pl.multiple_of(j * Q_BLOCK, Q_BLOCK)
        kv_off = pl.multiple_of(jnp.minimum(j, n_key_tiles - 1) * Q_BLOCK, Q_BLOCK)
        return kv_off, hot_off

    def sel_body(i, carry):
        offs = [tile_offsets(i, k) for k in range(per_step)]
        hot = jnp.concatenate([hot_ref[pl.ds(ho, Q_BLOCK), :] for _, ho in offs], axis=0)
        s_g = []
        for g in range(KV_GROUPS):
            ks_ref = (ks0_ref, ks1_ref)[g]
            k_t = jnp.concatenate([ks_ref[pl.ds(ko, Q_BLOCK), :] for ko, _ in offs], axis=0)
            s_g.append(_nt_dot(lhs_scr[g], jnp.concatenate([k_t, hot], axis=1)))
        for g in range(KV_GROUPS):
            vs_ref = (vs0_ref, vs1_ref)[g]
            v_t = jnp.concatenate([vs_ref[pl.ds(ko, Q_BLOCK), :] for ko, _ in offs], axis=0)
            s = s_g[g]
            m_old = m_scr[g]
            m_new = jnp.maximum(m_old, jnp.max(s, axis=-1, keepdims=True))
            p = jnp.exp2(s - jnp.tile(m_new, (1, SEL_TK // LANES)))
            alpha = jnp.exp2(m_old - m_new)
            m_scr[g] = m_new
            pv = jnp.dot(p.astype(BF16), v_t, preferred_element_type=F32)
            acc_scr[g] = alpha * acc_scr[g] + pv
        return carry

    lax.fori_loop(0, (count + per_step - 1) // per_step, sel_body, 0)

    o_s = []
    for g in range(KV_GROUPS):
        one = _group_lane0(1 - g) + ONE_LANE
        acc = acc_scr[g]
        o_s.append(acc * (1.0 / acc[:, one:one + 1]))
    for h in range(HPG):
        outs = [part_scr[g, hrows[h]] + gate(g, h, 1) * o_s[g][hrows[h]] for g in range(KV_GROUPS)]
        o_ref[:, h * LANES:(h + 1) * LANES] = jnp.where(lane < HEAD_DIM, outs[0], outs[1]).astype(o_ref.dtype)


def _attention_tables(seq, ncp, n_cmp, n_sel):
    cstart = np.arange(ncp) * CMP_STRIDE
    sstart = np.arange(n_sel) * SEL_BLOCK
    overlap = ((cstart[:, None] < sstart[None, :] + SEL_BLOCK)
               & (cstart[:, None] + CMP_BLOCK > sstart[None, :])
               & (np.arange(ncp)[:, None] < n_cmp)).astype(np.float32)
    overlap = np.pad(overlap, ((0, 0), (0, LANES - n_sel)))
    ov4 = np.tile(overlap, (HPG, 1))
    n = np.arange(ncp)
    cfeat = np.zeros((KV_GROUPS, ncp, LANES), np.float32)
    for g in range(KV_GROUPS):
        l0 = _group_lane0(1 - g)
        for k in range(3):
            cfeat[g, :, l0 + F_CHI + k] = n // CMP_RADIX
            cfeat[g, :, l0 + F_CLO + k] = n % CMP_RADIX
            cfeat[g, :, l0 + F_COFF + k] = CMP_BLOCK - 1
    blk = np.concatenate([np.arange(seq) // SEL_BLOCK, np.full(Q_BLOCK, n_sel)])
    hot = (blk[:, None] == np.arange(LANES)[None, :]).astype(np.float32)
    r = np.arange(Q_BLOCK)[:, None]
    tri = np.where(np.arange(Q_BLOCK)[None, :] <= r, 0.0, NEG_INF).astype(np.float32)
    c = np.arange(WIN_KEYS)[None, :]
    wm = []
    for qb in range(WIN_BACK + 1):
        dist = (qb * Q_BLOCK + r) - c if qb < WIN_BACK else (WINDOW + r) - c
        wm.append(np.where((dist >= 0) & (dist < WINDOW), 0.0, NEG_INF))
    wm = np.stack(wm).astype(np.float32)
    n_gate = 3 * NSA_HEADS
    gsel = np.zeros((2 * LANES, n_gate, LANES), np.float32)
    for c in range(n_gate):
        gsel[c, c, :] = 1.0
        gsel[LANES + c, c, :] = 1.0
    gsel = jnp.asarray(gsel.reshape(2 * LANES, n_gate * LANES), BF16)
    return (jnp.asarray(ov4, BF16), jnp.asarray(cfeat, BF16), jnp.asarray(hot, BF16), gsel,
            jnp.asarray(tri), jnp.asarray(wm))


def _attention(zq, kvc, zkv, zg, batch, seq):
    n = zq.shape[0]
    nqb = seq // Q_BLOCK
    ncp = kvc.shape[2]
    n_cmp = seq // CMP_STRIDE - 1
    n_sel = seq // SEL_BLOCK
    assert n_sel <= HEAD_DIM and n_sel % SUBLANES == 0 and SEL_TK % Q_BLOCK == 0 and nqb > WIN_BACK
    assert ncp // CMP_RADIX <= 256 and seq // SEL_BLOCK <= 256
    ov4, cfeat, hot, gsel, tri, wm = _attention_tables(seq, ncp, n_cmp, n_sel)
    out_cols = HPG * LANES

    kernel = functools.partial(_attn_kernel, n_cmp=n_cmp, n_sel=n_sel)
    kv_spec = lambda idx: pl.BlockSpec((seq, KV_SLAB), lambda b, i, idx=idx: (b, idx))
    return pl.pallas_call(
        kernel,
        grid=(batch, nqb),
        in_specs=[pl.BlockSpec((Q_BLOCK, Q_COLS), lambda b, i: (b * nqb + i, 0)),
                  pl.BlockSpec((1, 1, ncp, KV_SLAB), lambda b, i: (0, b, 0, 0)),
                  pl.BlockSpec((1, 1, ncp, KV_SLAB), lambda b, i: (1, b, 0, 0)),
                  *[kv_spec(k) for k in range(8)],
                  pl.BlockSpec((Q_BLOCK, GATE_COLS), lambda b, i: (b * nqb + i, 0)),
                  _const_spec(ov4.shape), _const_spec(cfeat.shape), _const_spec(hot.shape),
                  _const_spec(tri.shape), _const_spec(wm.shape), _const_spec(gsel.shape)],
        out_specs=pl.BlockSpec((Q_BLOCK, out_cols), lambda b, i: (b * nqb + i, 0)),
        out_shape=jax.ShapeDtypeStruct((n, out_cols), BF16),
        scratch_shapes=[pltpu.VMEM((KV_GROUPS, n_sel, Q_BLOCK), F32),
                        pltpu.VMEM((KV_GROUPS, HPG * Q_BLOCK, 2 * LANES), BF16),
                        pltpu.VMEM((KV_GROUPS, HPG * Q_BLOCK, LANES), F32),
                        pltpu.VMEM((KV_GROUPS, HPG * Q_BLOCK, LANES), F32),
                        pltpu.VMEM((KV_GROUPS, HPG * Q_BLOCK, LANES), F32),
                        pltpu.SMEM((seq // Q_BLOCK + SEL_TK // Q_BLOCK,), jnp.int32)],
        compiler_params=_cparams(("arbitrary", "arbitrary")),
        name="nsa_attn",
    )(zq, kvc, kvc, *([zkv] * 8), zg, ov4, cfeat, hot, tri, wm, gsel)


def _s5_kernel(u_ref, lagk_ref, mre_ref, mim_ref, cre_ref, cim_ref, lre_ref, lim_ref, d_ref, y_ref,
               ere, eim, xre, xim, tp_scr, *, n_chunks):
    def tok(t):
        return pl.ds(t, n_chunks, stride=S5_CHUNK)

    @pl.when(pl.program_id(1) == 0)
    def _():
        tp_scr[...] = jnp.zeros(tp_scr.shape, tp_scr.dtype)
        for s in range(S5_CHUNK):
            for t in range(s, S5_CHUNK):
                tp_scr[s * LANES:(s + 1) * LANES, t * LANES:(t + 1) * LANES] = lagk_ref[0, t - s]

    u = jnp.concatenate([u_ref[tok(t), :] for t in range(S5_CHUNK)], axis=1).astype(BF16)
    y = jnp.dot(u, tp_scr[...], preferred_element_type=F32)
    ere[...] = jnp.dot(u, mre_ref[0, 0], preferred_element_type=F32)
    eim[...] = jnp.dot(u, mim_ref[0, 0], preferred_element_type=F32)
    lr = lre_ref[0]
    li = lim_ref[0]

    def body(c, carry):
        xr, xi = carry
        row = pl.ds(c, 1)
        xre[row, :] = xr
        xim[row, :] = xi
        er = ere[row, :]
        ei = eim[row, :]
        return lr * xr - li * xi + er, lr * xi + li * xr + ei

    zero = jnp.zeros((1, ere.shape[1]), F32)
    lax.fori_loop(0, n_chunks, body, (zero, zero), unroll=8)
    y = y + jnp.dot(xre[...].astype(BF16), cre_ref[0, 0], preferred_element_type=F32)
    y = y + jnp.dot(xim[...].astype(BF16), cim_ref[0, 0], preferred_element_type=F32)
    d = d_ref[0]
    for t in range(S5_CHUNK):
        y_ref[tok(t), :] = y[:, t * LANES:(t + 1) * LANES] + d * u_ref[tok(t), :]


def _s5_matrices(a_re, a_im, b_re, b_im, c_re, c_im, d_skip, log_dt):
    t = S5_CHUNK
    hp = lax.Precision.HIGHEST
    dt = jnp.exp(log_dt)[:, None]
    lam_re = jnp.minimum(a_re, -1e-4)
    lam_im = a_im
    mag = jnp.exp(lam_re * dt)
    ang = lam_im * dt
    lb_re = mag * jnp.cos(ang)
    lb_im = mag * jnp.sin(ang)
    den = lam_re * lam_re + lam_im * lam_im
    nr = lb_re - 1.0
    coef_re = (nr * lam_re + lb_im * lam_im) / den
    coef_im = (lb_im * lam_re - nr * lam_im) / den
    bb_re = coef_re[..., None] * b_re - coef_im[..., None] * b_im
    bb_im = coef_re[..., None] * b_im + coef_im[..., None] * b_re
    j = jnp.arange(t + 1, dtype=F32)[:, None, None]
    pmag = jnp.exp(j * (lam_re * dt)[None])
    pw_re = pmag * jnp.cos(j * ang[None])
    pw_im = pmag * jnp.sin(j * ang[None])
    cl_re = c_re[None] * pw_re[:, :, None, :] - c_im[None] * pw_im[:, :, None, :]
    cl_im = c_re[None] * pw_im[:, :, None, :] + c_im[None] * pw_re[:, :, None, :]
    kern = (jnp.einsum('jgop,gpi->gjoi', cl_re[:t], bb_re, precision=hp)
            - jnp.einsum('jgop,gpi->gjoi', cl_im[:t], bb_im, precision=hp))
    eye = jnp.eye(S5_LB, dtype=F32)
    nlb = S5_GROUPS // S5_LB
    lagk = jnp.einsum('qgjoi,gh->qjgiho', kern.reshape(nlb, S5_LB, t, S5_GROUP, S5_GROUP), eye,
                      precision=hp).reshape(nlb, t, LANES, LANES)
    rv_re = pw_re[t - 1 - np.arange(t)]
    rv_im = pw_im[t - 1 - np.arange(t)]
    m_re = rv_re[..., None] * bb_re[None] - rv_im[..., None] * bb_im[None]
    m_im = rv_re[..., None] * bb_im[None] + rv_im[..., None] * bb_re[None]

    def lay_m(m):
        m = m.reshape(2, t, nlb, S5_LB, S5_STATE, S5_GROUP)
        return jnp.einsum('csqgpi,gh->cqsgihp', m, eye, precision=hp).reshape(
            2, nlb, t * LANES, S5_LB * S5_STATE)

    def lay_c(c):
        c = c.reshape(2, t, nlb, S5_LB, S5_GROUP, S5_STATE)
        return jnp.einsum('ctqgop,gh->cqgptho', c, eye, precision=hp).reshape(
            2, nlb, S5_LB * S5_STATE, t * LANES)

    lam_t_re = pw_re[t].reshape(nlb, 1, S5_LB * S5_STATE)
    lam_t_im = pw_im[t].reshape(nlb, 1, S5_LB * S5_STATE)
    d_row = d_skip.reshape(nlb, 1, LANES)
    m_ops = lay_m(jnp.stack([m_re, m_im])).astype(BF16)
    c_ops = lay_c(jnp.stack([cl_re[1:], -cl_im[1:]])).astype(BF16)
    return lagk.astype(BF16), m_ops, c_ops, lam_t_re, lam_t_im, d_row


def _s5(zs, mats, batch, seq):
    n_chunks = seq // S5_CHUNK
    lagk, m_ops, c_ops, l_re, l_im, d_row = mats
    nlb = lagk.shape[0]
    wcol = S5_CHUNK * LANES
    wst = S5_LB * S5_STATE
    kernel = functools.partial(_s5_kernel, n_chunks=n_chunks)
    p3 = lambda r, c: pl.BlockSpec((1, r, c), lambda q, b: (q, 0, 0))
    part = lambda k, r, c: pl.BlockSpec((1, 1, r, c), lambda q, b, k=k: (k, q, 0, 0))
    return pl.pallas_call(
        kernel,
        grid=(nlb, batch),
        in_specs=[pl.BlockSpec((seq, LANES), lambda q, b: (b, q)),
                  pl.BlockSpec((1, S5_CHUNK, LANES, LANES), lambda q, b: (q, 0, 0, 0)),
                  part(0, wcol, wst), part(1, wcol, wst), part(0, wst, wcol), part(1, wst, wcol),
                  p3(1, wst), p3(1, wst), p3(1, LANES)],
        out_specs=pl.BlockSpec((seq, LANES), lambda q, b: (b, q)),
        out_shape=jax.ShapeDtypeStruct(zs.shape, F32),
        scratch_shapes=[pltpu.VMEM((n_chunks, wst), F32) for _ in range(4)]
        + [pltpu.VMEM((wcol, wcol), BF16)],
        compiler_params=_cparams(("arbitrary", "arbitrary")),
        name="s5",
    )(zs, lagk, m_ops, m_ops, c_ops, c_ops, l_re, l_im, d_row)


def _merge_kernel(oa_ref, ys_ref, zm_ref, x_ref, mod_ref, wn_ref, wg_ref, wo_ref, g_ref, b_ref, o_ref,
                  *, alpha):
    d = x_ref.shape[-1]
    y_a = jnp.dot(oa_ref[...], wn_ref[...], preferred_element_type=F32)
    gl = jax.nn.gelu(ys_ref[...], approximate=True)
    zz = jnp.dot(gl.astype(BF16), wg_ref[...], preferred_element_type=F32)
    y_b = zz[:, :d] * jax.nn.sigmoid(zz[:, d:])
    zm = zm_ref[...]
    mix_in = jax.nn.sigmoid(zm[:, :d]) * y_a + jax.nn.sigmoid(zm[:, d:]) * y_b
    mix = jnp.dot(mix_in.astype(BF16), wo_ref[...], preferred_element_type=F32)
    gate = mod_ref[0, 2:3, :]
    r = alpha * x_ref[...] + gate * mix
    o_ref[...] = _layer_norm(r) * g_ref[...] + b_ref[...]


def _merge(oa, ys, zm, x2, mod3, wn_pad, wg, wo, ln_g, ln_b, seq, alpha):
    n, d = x2.shape
    tiles_per_batch = seq // TM_MERGE
    row = lambda w: pl.BlockSpec((TM_MERGE, w), lambda i: (i, 0))
    return pl.pallas_call(
        functools.partial(_merge_kernel, alpha=alpha),
        grid=(n // TM_MERGE,),
        in_specs=[row(oa.shape[1]), row(ys.shape[1]), row(zm.shape[1]), row(d),
                  pl.BlockSpec((1, 6, d), lambda i: (i // tiles_per_batch, 0, 0)),
                  _const_spec(wn_pad.shape), _const_spec(wg.shape), _const_spec(wo.shape),
                  _const_spec((1, d)), _const_spec((1, d))],
        out_specs=row(d),
        out_shape=jax.ShapeDtypeStruct((n, d), F32),
        compiler_params=_cparams(("arbitrary",)),
        name="merge",
    )(oa, ys, zm, x2, mod3, wn_pad, wg, wo, ln_g.reshape(1, d), ln_b.reshape(1, d))


def _pad_nsa_out(w):
    rows = []
    for h in range(HPG):
        for g in range(KV_GROUPS):
            hd = g * HPG + h
            rows.append(w[hd * HEAD_DIM:(hd + 1) * HEAD_DIM])
    return jnp.concatenate(rows, axis=0).astype(BF16)


def _ffn_kernel(x_ref, mod_ref, wup_ref, cw_ref, cb_ref, wdn_ref, g_ref, b_ref, o_ref, tail_ref, act_scr,
                *, alpha, tiles_per_batch):
    i = pl.program_id(0)
    tm = x_ref.shape[0]

    @pl.when(i % tiles_per_batch == 0)
    def _():
        tail_ref[...] = jnp.zeros(tail_ref.shape, F32)

    shift = mod_ref[0, 3:4, :]
    scale = mod_ref[0, 4:5, :]
    gate = mod_ref[0, 5:6, :]
    row = lax.broadcasted_iota(jnp.int32, (SUBLANES, FF_CHUNK), 0)

    def shift_rows(a, prev_rows):
        rolled = pltpu.roll(a, 1, axis=0)
        head = jnp.where(row == 0, prev_rows[SUBLANES - 1:SUBLANES], rolled[:SUBLANES])
        return jnp.concatenate([head, rolled[SUBLANES:]], axis=0)

    part_rows = tm // FFN_PARTS
    for part in range(FFN_PARTS):
        rows = slice(part * part_rows, (part + 1) * part_rows)
        x = x_ref[rows, :]
        h2 = (_layer_norm(x) * (1.0 + scale) + shift).astype(BF16)

        def conv_cols(c0):
            cols = slice(c0, c0 + FF_CHUNK)
            up = jnp.dot(h2, wup_ref[:, cols], preferred_element_type=F32)
            tail = tail_ref[:, cols]
            tail_ref[:, cols] = up[part_rows - SUBLANES:part_rows]
            w = cw_ref[:, cols]
            inner = shift_rows(w[0:1] * up, w[0:1] * tail) + w[1:2] * up
            prev_inner = pltpu.roll(w[0:1] * tail, 1, axis=0) + w[1:2] * tail
            return shift_rows(inner, prev_inner) + w[2:3] * up + cb_ref[:, cols]

        for k in range(D_FF // FF_CHUNK):
            val = conv_cols(k * FF_CHUNK)
            gte = conv_cols(D_FF + k * FF_CHUNK)
            act_scr[rows, k * FF_CHUNK:(k + 1) * FF_CHUNK] = (gte * jax.nn.sigmoid(gte) * val).astype(BF16)
        ff = jnp.dot(act_scr[rows, :], wdn_ref[...], preferred_element_type=F32)
        r = alpha * x + gate * ff
        o_ref[rows, :] = _layer_norm(r) * g_ref[...] + b_ref[...]


def _ffn(x1, mod3, wup, conv_w, conv_b, wdn, ln_g, ln_b, seq, alpha):
    n, d = x1.shape
    tiles_per_batch = seq // TM_FFN
    ff2 = wup.shape[1]
    return pl.pallas_call(
        functools.partial(_ffn_kernel, alpha=alpha, tiles_per_batch=tiles_per_batch),
        grid=(n // TM_FFN,),
        in_specs=[pl.BlockSpec((TM_FFN, d), lambda i: (i, 0)),
                  pl.BlockSpec((1, 6, d), lambda i: (i // tiles_per_batch, 0, 0)),
                  _const_spec(wup.shape), _const_spec(conv_w.shape), _const_spec((1, ff2)),
                  _const_spec(wdn.shape), _const_spec((1, d)), _const_spec((1, d))],
        out_specs=pl.BlockSpec((TM_FFN, d), lambda i: (i, 0)),
        out_shape=jax.ShapeDtypeStruct((n, d), F32),
        scratch_shapes=[pltpu.VMEM((SUBLANES, ff2), F32), pltpu.VMEM((TM_FFN, ff2 // 2), BF16)],
        compiler_params=_cparams(("arbitrary",)),
        name="ffn",
    )(x1, mod3, wup, conv_w, conv_b.reshape(1, ff2), wdn, ln_g.reshape(1, d), ln_b.reshape(1, d))


def _merge_ffn_kernel(oa_ref, ys_ref, zm_ref, x_ref, mod_ref, wn_ref, wg_ref, wo_ref, g1_ref, b1_ref,
                      wup_ref, cw_ref, cb_ref, wdn_ref, g2_ref, b2_ref, o_ref, x1_scr, tail_ref, act_scr,
                      *, alpha, tiles_per_batch):
    _merge_kernel(oa_ref, ys_ref, zm_ref, x_ref, mod_ref, wn_ref, wg_ref, wo_ref, g1_ref, b1_ref, x1_scr,
                  alpha=alpha)
    _ffn_kernel(x1_scr, mod_ref, wup_ref, cw_ref, cb_ref, wdn_ref, g2_ref, b2_ref, o_ref, tail_ref, act_scr,
                alpha=alpha, tiles_per_batch=tiles_per_batch)


def _merge_ffn(oa, ys, zm, x2, mod3, wn_pad, wg, wo, ln1_g, ln1_b, wup, conv_w, conv_b, wdn, ln2_g, ln2_b,
               seq, alpha):
    n, d = x2.shape
    tiles_per_batch = seq // TM_FFN
    ff2 = wup.shape[1]
    row = lambda w: pl.BlockSpec((TM_FFN, w), lambda i: (i, 0))
    vec = lambda v: v.reshape(1, -1)
    return pl.pallas_call(
        functools.partial(_merge_ffn_kernel, alpha=alpha, tiles_per_batch=tiles_per_batch),
        grid=(n // TM_FFN,),
        in_specs=[row(oa.shape[1]), row(ys.shape[1]), row(zm.shape[1]), row(d),
                  pl.BlockSpec((1, 6, d), lambda i: (i // tiles_per_batch, 0, 0)),
                  _const_spec(wn_pad.shape), _const_spec(wg.shape), _const_spec(wo.shape),
                  _const_spec((1, d)), _const_spec((1, d)),
                  _const_spec(wup.shape), _const_spec(conv_w.shape), _const_spec((1, ff2)),
                  _const_spec(wdn.shape), _const_spec((1, d)), _const_spec((1, d))],
        out_specs=row(d),
        out_shape=jax.ShapeDtypeStruct((n, d), F32),
        scratch_shapes=[pltpu.VMEM((TM_FFN, d), F32), pltpu.VMEM((SUBLANES, ff2), F32),
                        pltpu.VMEM((TM_FFN, ff2 // 2), BF16)],
        compiler_params=_cparams(("arbitrary",)),
        name="merge_ffn",
    )(oa, ys, zm, x2, mod3, wn_pad, wg, wo, vec(ln1_g), vec(ln1_b), wup, conv_w, vec(conv_b), wdn,
      vec(ln2_g), vec(ln2_b))


def kernel(x, c, w_ada, b_ada, w_in, pe_ck, w_ck1, w_ck2, pe_cv, w_cv1, w_cv2, w_nsa_out,
           s5_a_re, s5_a_im, s5_b_re, s5_b_im, s5_c_re, s5_c_im, s5_d, s5_log_dt, w_s5_glu,
           w_o, ln1_g, ln1_b, w_up, conv_w, conv_b, w_down, ln2_g, ln2_b):
    batch, seq, d = x.shape
    depth = w_ada.shape[0]
    alpha = (2.0 * depth) ** 0.25
    n = batch * seq
    n_chunk16 = seq // CMP_STRIDE
    xf = x.reshape(n, d)
    for l in range(depth):
        mod3 = _ada(c, w_ada[l], b_ada[l]).reshape(batch, 6, d)

        zq, zc, zkv, zg, zs, zm = _inproj(xf, mod3, _build_w_all(w_in[l]), seq)

        wk1, pek, wk2 = _cmp_weights(pe_ck[l], w_ck1[l], w_ck2[l])
        wv1, pev, wv2 = _cmp_weights(pe_cv[l], w_cv1[l], w_cv2[l])
        ch = zc.reshape(2, batch, n_chunk16, CMP_STRIDE * KV_SLAB)
        kvc = _compress(ch, jnp.stack([wk1, wv1]), jnp.stack([pek, pev]), jnp.stack([wk2, wv2]))

        oa = _attention(zq, kvc, zkv, zg, batch, seq)

        mats = _s5_matrices(s5_a_re[l], s5_a_im[l], s5_b_re[l], s5_b_im[l], s5_c_re[l], s5_c_im[l],
                            s5_d[l], s5_log_dt[l])
        ys = _s5(zs, mats, batch, seq)

        xf = _merge_ffn(oa, ys, zm, xf, mod3, _pad_nsa_out(w_nsa_out[l]), w_s5_glu[l].astype(BF16),
                        w_o[l].astype(BF16), ln1_g[l], ln1_b[l], w_up[l].astype(BF16), conv_w[l], conv_b[l],
                        w_down[l].astype(BF16), ln2_g[l], ln2_b[l], seq, alpha)
    return xf.reshape(batch, seq, d)
```
